```python
import math
import jax, jax.numpy as jnp
from jax import lax
import numpy as np

D_MODEL = 1024
BATCH = 4
SEQ = 4096
DEPTH = 4

D_FF = 2816
NSA_HEADS = 8
NSA_GROUPS = 2
NSA_HPG = NSA_HEADS // NSA_GROUPS
NSA_DH = 64
CMP_BLOCK = 32
CMP_STRIDE = 16
CMP_HIDDEN = 256
SEL_BLOCK = 64
SEL_TOPN = 16
WINDOW = 512
Q_BLOCK = 64
GLA_HEADS = 4
GLA_DK = 128
GLA_DV = 256
GLA_RANK = 16
GLA_TAU = 16.0
GLA_CHUNK = 64
REL_BUCKETS = 32
REL_MAX_DIST = 1024
EPS = 1e-6
NEG = -1e30

W_IN_SPLITS = (
    NSA_HEADS * NSA_DH,
    6 * NSA_GROUPS * NSA_DH,
    3 * NSA_HEADS,
    GLA_HEADS * GLA_DK,
    GLA_HEADS * GLA_DK,
    GLA_HEADS * GLA_DV,
    GLA_RANK,
    GLA_HEADS * GLA_DV,
    2 * D_MODEL,
)
D_IN = sum(W_IN_SPLITS)

kernel_name = "hybrid_nsa_gla_macaron_t5bias"


def rmsnorm(x, g):
    xf = x.astype(jnp.float32)
    y = xf * lax.rsqrt(jnp.mean(xf * xf, axis=-1, keepdims=True) + EPS)
    return (y * g.astype(jnp.float32)).astype(x.dtype)


def swiglu(x, w_gate, w_up, w_down):
    return (jax.nn.silu(x @ w_gate) * (x @ w_up)) @ w_down


def rel_bucket(dist):
    n = jnp.maximum(dist, 0)
    exact = REL_BUCKETS // 2
    nf = jnp.maximum(n, 1).astype(jnp.float32)
    log_b = exact + (jnp.log(nf / exact) / math.log(REL_MAX_DIST / exact)
                     * (REL_BUCKETS - exact)).astype(jnp.int32)
    return jnp.where(n < exact, n, jnp.minimum(log_b, REL_BUCKETS - 1))


def masked_softmax(logits, mask):
    p = jax.nn.softmax(jnp.where(mask, logits, NEG), axis=-1)
    return jnp.where(mask, p, 0.0)


def nsa_compress(k, pos, w1, w2):
    B, S, G, dh = k.shape
    nc = (S - CMP_BLOCK) // CMP_STRIDE + 1
    idx = jnp.arange(nc)[:, None] * CMP_STRIDE + jnp.arange(CMP_BLOCK)[None, :]
    blocks = k[:, idx] + pos[None, None, :, None, :]
    blocks = blocks.transpose(0, 1, 3, 2, 4).reshape(B, nc, G, CMP_BLOCK * dh)
    return jax.nn.silu(blocks @ w1) @ w2


def nsa_attention(q, k_cmp, v_cmp, k_slc, v_slc, k_win, v_win, gates, rel_table):
    B, S, H, dh = q.shape
    G, HPG = NSA_GROUPS, NSA_HPG
    nqb = S // Q_BLOCK
    ns = S // SEL_BLOCK
    nc = k_cmp.shape[1]
    n_sel = min(SEL_TOPN, ns)
    kw_len = WINDOW + Q_BLOCK

    c_start = jnp.arange(nc) * CMP_STRIDE
    cmp_end = c_start + CMP_BLOCK - 1
    sb_start = jnp.arange(ns) * SEL_BLOCK
    overlap = ((c_start[:, None] < sb_start[None, :] + SEL_BLOCK)
               & (c_start[:, None] + CMP_BLOCK > sb_start[None, :])).astype(jnp.float32)
    js = jnp.arange(ns)
    table_g = rel_table.reshape(REL_BUCKETS, G, HPG).transpose(1, 0, 2)

    kb = k_slc.reshape(B, ns, SEL_BLOCK, G, dh).transpose(0, 3, 1, 2, 4)
    vb = v_slc.reshape(B, ns, SEL_BLOCK, G, dh).transpose(0, 3, 1, 2, 4)
    pad = ((0, 0), (WINDOW, 0), (0, 0), (0, 0))
    kw = jnp.pad(k_win, pad)
    vw = jnp.pad(v_win, pad)

    def to_blocks(a):
        return a.reshape(B, nqb, Q_BLOCK, *a.shape[2:]).swapaxes(0, 1)

    qs = to_blocks(q * (dh ** -0.5))
    gs = to_blocks(gates)
    bi = jnp.arange(B)[:, None, None, None]
    gi = jnp.arange(G)[None, :, None, None]

    def block_fn(args):
        c, qc, gc = args
        t = c * Q_BLOCK + jnp.arange(Q_BLOCK)
        qg = qc.reshape(B, Q_BLOCK, G, HPG, dh)

        lc = jnp.einsum('bqghd,bcgd->bghqc', qg, k_cmp).astype(jnp.float32)
        bias_c = rel_table[rel_bucket(t[:, None] - cmp_end[None, :])]
        lc = lc + bias_c.reshape(Q_BLOCK, nc, G, HPG).transpose(2, 3, 0, 1)
        p_cmp = masked_softmax(lc, cmp_end[None, :] <= t[:, None])
        o_cmp = jnp.einsum('bghqc,bcgd->bqghd', p_cmp.astype(v_cmp.dtype), v_cmp)

        imp = jnp.einsum('bghqc,cs->bgqs', p_cmp, overlap)
        jcur = t // SEL_BLOCK
        forced = (js[None, :] == 0) | (js[None, :] == jcur[:, None]) | (js[None, :] == jcur[:, None] - 1)
        valid = js[None, :] <= jcur[:, None]
        score = jnp.where(forced, 1e6, jnp.where(valid, imp, -1e6))
        _, sel = lax.top_k(score, n_sel)
        ks = kb[bi, gi, sel]
        vs = vb[bi, gi, sel]
        s_pos = sel[..., None] * SEL_BLOCK + jnp.arange(SEL_BLOCK)
        ls = jnp.einsum('bqghd,bgqnrd->bghqnr', qg, ks).astype(jnp.float32)
        bias_s = table_g[gi[..., None], rel_bucket(t[:, None, None] - s_pos)]
        ls = ls + bias_s.transpose(0, 1, 5, 2, 3, 4)
        ksel = n_sel * SEL_BLOCK
        ls = ls.reshape(B, G, HPG, Q_BLOCK, ksel)
        ms = (s_pos <= t[:, None, None]).reshape(B, G, Q_BLOCK, ksel)[:, :, None]
        p_s = masked_softmax(ls, ms)
        o_slc = jnp.einsum('bghqk,bgqkd->bqghd', p_s.astype(vs.dtype),
                           vs.reshape(B, G, Q_BLOCK, ksel, dh))

        kwc = lax.dynamic_slice_in_dim(kw, c * Q_BLOCK, kw_len, axis=1)
        vwc = lax.dynamic_slice_in_dim(vw, c * Q_BLOCK, kw_len, axis=1)
        s_w = c * Q_BLOCK - WINDOW + jnp.arange(kw_len)
        dist = t[:, None] - s_w[None, :]
        mw = (s_w[None, :] >= 0) & (dist >= 0) & (dist < WINDOW)
        lw = jnp.einsum('bqghd,bkgd->bghqk', qg, kwc).astype(jnp.float32)
        lw = lw + rel_table[rel_bucket(dist)].reshape(Q_BLOCK, kw_len, G, HPG).transpose(2, 3, 0, 1)
        p_w = masked_softmax(lw, mw)
        o_win = jnp.einsum('bghqk,bkgd->bqghd', p_w.astype(vwc.dtype), vwc)

        o = jnp.stack([o_cmp, o_slc, o_win], axis=-1)
        g = gc.reshape(B, Q_BLOCK, G, HPG, 3)[..., None, :]
        return jnp.sum(o * g, axis=-1).reshape(B, Q_BLOCK, H * dh).astype(q.dtype)

    out = lax.map(block_fn, (jnp.arange(nqb), qs, gs))
    return out.swapaxes(0, 1).reshape(B, S, H * dh)


def gla_attention(q, k, v, log_a):
    B, S, Hb, dk = q.shape
    dv = v.shape[-1]
    C = GLA_CHUNK
    n = S // C

    def chunks(a):
        return a.reshape(B, n, C, *a.shape[2:])

    qc, kc, vc, la = chunks(q * (dk ** -0.5)), chunks(k), chunks(v), chunks(log_a)
    b = jnp.cumsum(la.astype(jnp.float32), axis=2)
    b_last = b[:, :, -1:]
    q_dec = qc * jnp.exp(b)
    k_intra = kc * jnp.exp(-b)
    k_state = kc * jnp.exp(b_last - b)
    causal = jnp.tril(jnp.ones((C, C), dtype=bool))
    A = jnp.where(causal, jnp.einsum('bnihd,bnjhd->bnhij', q_dec, k_intra), 0.0)
    o_intra = jnp.einsum('bnhij,bnjhv->bnihv', A, vc.astype(jnp.float32))
    decay_chunk = jnp.exp(b_last[:, :, 0])

    def step(state, inp):
        qd, ksd, vcc, dch = inp
        o = jnp.einsum('bihd,bhdv->bihv', qd, state)
        state = dch[..., None] * state + jnp.einsum('bjhd,bjhv->bhdv', ksd, vcc.astype(jnp.float32))
        return state, o

    s0 = jnp.zeros((B, Hb, dk, dv), jnp.float32)
    _, o_inter = lax.scan(step, s0, (q_dec.swapaxes(0, 1), k_state.swapaxes(0, 1),
                                     vc.swapaxes(0, 1), decay_chunk.swapaxes(0, 1)))
    o = o_intra + o_inter.swapaxes(0, 1)
    return o.reshape(B, S, Hb, dv).astype(v.dtype)


def setup_inputs(seed: int = 0) -> dict:
    key = jax.random.key(seed)
    ks = jax.random.split(key, 32)
    f32 = jnp.float32
    L, D = DEPTH, D_MODEL

    def w(k, shape, fan_in):
        return jax.random.normal(k, shape, f32) * fan_in ** -0.5

    def gain(k, shape):
        return 1.0 + 0.05 * jax.random.normal(k, shape, f32)

    return {
        "x": jax.random.normal(ks[0], (BATCH, SEQ, D), f32),
        "rel_table": 0.2 * jax.random.normal(ks[1], (REL_BUCKETS, NSA_HEADS), f32),
        "ffn1_norm": gain(ks[2], (L, D)),
        "ffn1_w_gate": w(ks[3], (L, D, D_FF), D),
        "ffn1_w_up": w(ks[4], (L, D, D_FF), D),
        "ffn1_w_down": w(ks[5], (L, D_FF, D), D_FF),
        "mix_norm": gain(ks[6], (L, D)),
        "w_in": w(ks[7], (L, D, D_IN), D),
        "cmp_pos_k": 0.5 * jax.random.normal(ks[8], (L, CMP_BLOCK, NSA_DH), f32),
        "cmp_pos_v": 0.5 * jax.random.normal(ks[9], (L, CMP_BLOCK, NSA_DH), f32),
        "cmp_k_w1": w(ks[10], (L, CMP_BLOCK * NSA_DH, CMP_HIDDEN), CMP_BLOCK * NSA_DH),
        "cmp_k_w2": w(ks[11], (L, CMP_HIDDEN, NSA_DH), CMP_HIDDEN),
        "cmp_v_w1": w(ks[12], (L, CMP_BLOCK * NSA_DH, CMP_HIDDEN), CMP_BLOCK * NSA_DH),
        "cmp_v_w2": w(ks[13], (L, CMP_HIDDEN, NSA_DH), CMP_HIDDEN),
        "gla_a_w2": w(ks[14], (L, GLA_RANK, GLA_HEADS * GLA_DK), GLA_RANK),
        "gla_a_b": 0.1 * jax.random.normal(ks[15], (L, GLA_HEADS * GLA_DK), f32),
        "gla_out_norm": gain(ks[16], (L, GLA_HEADS * GLA_DV)),
        "w_branch_nsa": w(ks[17], (L, NSA_HEADS * NSA_DH, D), NSA_HEADS * NSA_DH),
        "w_branch_gla": w(ks[18], (L, GLA_HEADS * GLA_DV, D), GLA_HEADS * GLA_DV),
        "w_out": w(ks[19], (L, D, D), D),
        "ffn2_norm": gain(ks[20], (L, D)),
        "ffn2_w_gate": w(ks[21], (L, D, D_FF), D),
        "ffn2_w_up": w(ks[22], (L, D, D_FF), D),
        "ffn2_w_down": w(ks[23], (L, D_FF, D), D_FF),
        "final_norm": gain(ks[24], (D,)),
    }


def reference(x, rel_table, ffn1_norm, ffn1_w_gate, ffn1_w_up, ffn1_w_down, mix_norm, w_in,
              cmp_pos_k, cmp_pos_v, cmp_k_w1, cmp_k_w2, cmp_v_w1, cmp_v_w2, gla_a_w2, gla_a_b,
              gla_out_norm, w_branch_nsa, w_branch_gla, w_out, ffn2_norm, ffn2_w_gate, ffn2_w_up,
              ffn2_w_down, final_norm):
    B, S, D = x.shape
    offsets = np.cumsum(W_IN_SPLITS)[:-1].tolist()
    for l in range(DEPTH):
        x = x + 0.5 * swiglu(rmsnorm(x, ffn1_norm[l]), ffn1_w_gate[l], ffn1_w_up[l], ffn1_w_down[l])

        h = rmsnorm(x, mix_norm[l])
        proj = h @ w_in[l]
        q_a, kv_a, g_a, q_b, k_b, v_b, a_lr, r_b, g_merge = jnp.split(proj, offsets, axis=-1)

        q_a = q_a.reshape(B, S, NSA_HEADS, NSA_DH)
        kv_a = kv_a.reshape(B, S, 6, NSA_GROUPS, NSA_DH)
        k_cmp = nsa_compress(kv_a[:, :, 0], cmp_pos_k[l], cmp_k_w1[l], cmp_k_w2[l])
        v_cmp = nsa_compress(kv_a[:, :, 1], cmp_pos_v[l], cmp_v_w1[l], cmp_v_w2[l])
        gates_a = jax.nn.sigmoid(g_a).reshape(B, S, NSA_HEADS, 3)
        o_a = nsa_attention(q_a, k_cmp, v_cmp, kv_a[:, :, 2], kv_a[:, :, 3],
                            kv_a[:, :, 4], kv_a[:, :, 5], gates_a, rel_table)

        log_a = jax.nn.log_sigmoid((a_lr @ gla_a_w2[l] + gla_a_b[l]).astype(jnp.float32)) / GLA_TAU
        o_b = gla_attention(q_b.reshape(B, S, GLA_HEADS, GLA_DK),
                            k_b.reshape(B, S, GLA_HEADS, GLA_DK),
                            v_b.reshape(B, S, GLA_HEADS, GLA_DV),
                            log_a.reshape(B, S, GLA_HEADS, GLA_DK))
        o_b = rmsnorm(o_b, gla_out_norm[l].reshape(GLA_HEADS, GLA_DV))
        o_b = o_b.reshape(B, S, GLA_HEADS * GLA_DV) * jax.nn.silu(r_b)

        gm = jax.nn.sigmoid(g_merge).reshape(B, S, 2, D)
        y = gm[:, :, 0] * (o_a @ w_branch_nsa[l]) + gm[:, :, 1] * (o_b @ w_branch_gla[l])
        x = x + y @ w_out[l]

        x = x + 0.5 * swiglu(rmsnorm(x, ffn2_norm[l]), ffn2_w_gate[l], ffn2_w_up[l], ffn2_w_down[l])
    return rmsnorm(x, final_norm)
```

```python
import functools
import math

import numpy as np
import jax
import jax.numpy as jnp
from jax import lax
from jax.experimental import pallas as pl
from jax.experimental.pallas import tpu as pltpu

F32 = jnp.float32
BF16 = jnp.bfloat16

NSA_HEADS = 8
NSA_GROUPS = 2
NSA_HPG = NSA_HEADS // NSA_GROUPS
NSA_DH = 64
CMP_BLOCK = 32
CMP_STRIDE = 16
SEL_BLOCK = 64
SEL_SHIFT = 6
SEL_TOPN = 16
WINDOW = 512
GLA_HEADS = 4
GLA_DK = 128
GLA_DV = 256
GLA_RANK = 16
GLA_TAU = 16.0
GLA_CHUNK = 64
REL_BUCKETS = 32
REL_MAX_DIST = 1024
EPS = 1e-6
NEG = -1e30

LANES = 128
VMEM_LIMIT = 56 * 1024 * 1024

ATT_TILE = 256
N_BIAS_TILES = REL_MAX_DIST // ATT_TILE + 2
GLA_STEP = 256

D_MODEL = 1024
C_GM = 0
C_VB = 2048
C_RB = 3072
C_QA = 4096
C_QB = 4608
C_KB = 5120
C_KV = 5632
C_GA = 6400
C_AL = 6528
N_PROJ = 6656


def _dot(a, b, precision=None):
    return lax.dot_general(a, b, (((1,), (0,)), ((), ())), precision=precision,
                           preferred_element_type=F32)


def _dot_nt(a, b, precision=None):
    return lax.dot_general(a, b, (((1,), (1,)), ((), ())), precision=precision,
                           preferred_element_type=F32)


def _dot_tn(a, b, precision=None):
    return lax.dot_general(a, b, (((0,), (0,)), ((), ())), precision=precision,
                           preferred_element_type=F32)


def _rms(x, g):
    return x * lax.rsqrt(jnp.mean(x * x, axis=-1, keepdims=True) + EPS) * g


def _params(*sem):
    return pltpu.CompilerParams(dimension_semantics=sem, vmem_limit_bytes=VMEM_LIMIT)


def _ffn_body(x_ref, g_ref, wg_ref, wu_ref, wd_ref, *rest, final):
    if final:
        fg_ref, o_ref, h_ref, acc_ref = rest
    else:
        o_ref, h_ref, acc_ref = rest
    j = pl.program_id(1)

    @pl.when(j == 0)
    def _():
        h_ref[...] = _rms(x_ref[...], g_ref[...]).astype(BF16)
        acc_ref[...] = jnp.zeros_like(acc_ref)

    h = h_ref[...]
    gate = _dot(h, wg_ref[...])
    up = _dot(h, wu_ref[...])
    act = (gate * jax.nn.sigmoid(gate) * up).astype(BF16)
    acc_ref[...] += _dot(act, wd_ref[...])

    @pl.when(j == pl.num_programs(1) - 1)
    def _():
        y = x_ref[...] + 0.5 * acc_ref[...]
        if final:
            y = _rms(y, fg_ref[...])
        o_ref[...] = y


def _ffn(x, g, wg, wu, wd, layer, final_g=None, tm=512, tf=1408):
    T, D = x.shape
    F = wg.shape[-1]
    assert T % tm == 0 and F % tf == 0
    final = final_g is not None
    in_specs = [
        pl.BlockSpec((tm, D), lambda i, j: (i, 0)),
        pl.BlockSpec((None, 1, D), lambda i, j: (layer, 0, 0)),
        pl.BlockSpec((None, D, tf), lambda i, j: (layer, 0, j)),
        pl.BlockSpec((None, D, tf), lambda i, j: (layer, 0, j)),
        pl.BlockSpec((None, tf, D), lambda i, j: (layer, j, 0)),
    ]
    args = [x, g, wg, wu, wd]
    if final:
        in_specs.append(pl.BlockSpec((1, D), lambda i, j: (0, 0)))
        args.append(final_g)
    return pl.pallas_call(
        functools.partial(_ffn_body, final=final),
        grid=(T // tm, F // tf),
        in_specs=in_specs,
        out_specs=pl.BlockSpec((tm, D), lambda i, j: (i, 0)),
        out_shape=jax.ShapeDtypeStruct((T, D), F32),
        scratch_shapes=[pltpu.VMEM((tm, D), BF16), pltpu.VMEM((tm, D), F32)],
        compiler_params=_params("parallel", "arbitrary"),
        name="ffn",
    )(*args)


def _proj_body(x_ref, g_ref, w_ref, o_ref, h_ref):
    @pl.when(pl.program_id(1) == 0)
    def _():
        h_ref[...] = _rms(x_ref[...], g_ref[...]).astype(BF16)

    o_ref[...] = _dot(h_ref[...], w_ref[...])


def _proj(x, g, w, layer, tm=1024, tn=1664):
    T, D = x.shape
    N = w.shape[-1]
    assert T % tm == 0 and N % tn == 0
    return pl.pallas_call(
        _proj_body,
        grid=(T // tm, N // tn),
        in_specs=[
            pl.BlockSpec((tm, D), lambda i, j: (i, 0)),
            pl.BlockSpec((None, 1, D), lambda i, j: (layer, 0, 0)),
            pl.BlockSpec((None, D, tn), lambda i, j: (layer, 0, j)),
        ],
        out_specs=pl.BlockSpec((tm, tn), lambda i, j: (i, j)),
        out_shape=jax.ShapeDtypeStruct((T, N), F32),
        scratch_shapes=[pltpu.VMEM((tm, D), BF16)],
        compiler_params=_params("parallel", "arbitrary"),
        name="proj",
    )(x, g, w)


def _compress_body(x_ref, pos_ref, w1_ref, w2_ref, o_ref):
    x = x_ref[...]
    half = x.shape[1]
    lo = (x + pos_ref[:, :half]).astype(BF16)
    hi = (x + pos_ref[:, half:]).astype(BF16)
    h_lo = _dot(lo, w1_ref[:half, :])
    h_hi = _dot(hi, w1_ref[half:, :])
    nch = x.shape[0]
    hid = h_lo + pltpu.roll(h_hi, nch - 1, 0)
    act = (hid * jax.nn.sigmoid(hid)).astype(BF16)
    o_ref[...] = _dot(act, w2_ref[...]).astype(o_ref.dtype)


def _compress(xc, pos, w1, w2, layer):
    _, B, G, NCH, CW = xc.shape
    HC = w1.shape[-1]
    dh = w2.shape[-1]
    return pl.pallas_call(
        _compress_body,
        grid=(2, B, G),
        in_specs=[
            pl.BlockSpec((None, None, None, NCH, CW), lambda s, b, g: (s, b, g, 0, 0)),
            pl.BlockSpec((None, None, 1, 2 * CW), lambda s, b, g: (s, layer, 0, 0)),
            pl.BlockSpec((None, None, 2 * CW, HC), lambda s, b, g: (s, layer, 0, 0)),
            pl.BlockSpec((None, None, HC, dh), lambda s, b, g: (s, layer, 0, 0)),
        ],
        out_specs=pl.BlockSpec((None, None, None, NCH, dh), lambda s, b, g: (s, b, g, 0, 0)),
        out_shape=jax.ShapeDtypeStruct((2, B, G, NCH, dh), BF16),
        compiler_params=_params("parallel", "parallel", "parallel"),
        name="compress",
    )(xc, pos, w1, w2)


def _bias_lookup(n, thr_ref, tab_ref, h):
    val = jnp.full(n.shape, tab_ref[0, h], F32)
    for k in range(1, REL_BUCKETS):
        val = jnp.where(n >= thr_ref[k], tab_ref[k, h], val)
    return val


def _toeplitz_body(thr_ref, tab_ref, o_ref, *, rows):
    dd = pl.program_id(0)
    h = pl.program_id(1)
    T = o_ref.shape[-1]
    for r0 in range(0, T, rows):
        a = lax.broadcasted_iota(jnp.int32, (rows, T), 0) + r0
        b = lax.broadcasted_iota(jnp.int32, (rows, T), 1)
        n = jnp.clip(dd * T + a - b, 0, REL_MAX_DIST)
        o_ref[r0:r0 + rows, :] = _bias_lookup(n, thr_ref, tab_ref, h)


def _cmpbias_body(thr_ref, tab_ref, o_ref, *, rows):
    h = pl.program_id(0)
    i = pl.program_id(1)
    TQ, NC = o_ref.shape
    for r0 in range(0, TQ, rows):
        t = lax.broadcasted_iota(jnp.int32, (rows, NC), 0) + (i * TQ + r0)
        c = lax.broadcasted_iota(jnp.int32, (rows, NC), 1)
        n = jnp.clip(t - (c * CMP_STRIDE + (CMP_BLOCK - 1)), 0, REL_MAX_DIST)
        o_ref[r0:r0 + rows, :] = _bias_lookup(n, thr_ref, tab_ref, h)


def _bias_tables(thr, rel_table, S, ncp):
    T = ATT_TILE
    smem = pl.BlockSpec(memory_space=pltpu.SMEM)
    toep = pl.pallas_call(
        functools.partial(_toeplitz_body, rows=32),
        grid=(N_BIAS_TILES, NSA_HEADS),
        in_specs=[smem, smem],
        out_specs=pl.BlockSpec((None, None, T, T), lambda d, h: (d, h, 0, 0)),
        out_shape=jax.ShapeDtypeStruct((N_BIAS_TILES, NSA_HEADS, T, T), F32),
        compiler_params=_params("parallel", "parallel"),
        name="bias_toeplitz",
    )(thr, rel_table)
    cmpb = pl.pallas_call(
        functools.partial(_cmpbias_body, rows=32),
        grid=(NSA_HEADS, S // T),
        in_specs=[smem, smem],
        out_specs=pl.BlockSpec((None, T, ncp), lambda h, i: (h, i, 0)),
        out_shape=jax.ShapeDtypeStruct((NSA_HEADS, S, ncp), F32),
        compiler_params=_params("parallel", "parallel"),
        name="bias_cmp",
    )(thr, rel_table)
    return toep, cmpb


def _nsa_body(q_ref, gate_ref, kc_ref, vc_ref, ks_ref, vs_ref, kw_ref, vw_ref, bc_ref, bt_ref,
              ovl_ref, o_ref, m_ref, l_ref, acc_ref, *, n_sel_blocks, n_top):
    i = pl.program_id(2)
    HPG, TQ, DH = q_ref.shape
    TK = TQ
    R = HPG * TQ
    NCP = kc_ref.shape[0]
    NSP = ovl_ref.shape[0]
    t0 = i * TQ

    q = (q_ref[...] * (DH ** -0.5)).astype(BF16).reshape(R, DH)

    a_c = lax.broadcasted_iota(jnp.int32, (TQ, NCP), 0)
    c_c = lax.broadcasted_iota(jnp.int32, (TQ, NCP), 1)
    mc = (c_c * CMP_STRIDE + (CMP_BLOCK - 1)) <= (t0 + a_c)
    lc = _dot_nt(q, kc_ref[...]).reshape(HPG, TQ, NCP) + bc_ref[...]
    lc = jnp.where(mc[None], lc, NEG)
    pc = jnp.exp(lc - jnp.max(lc, axis=-1, keepdims=True))
    pc = jnp.where(mc[None], pc, 0.0)
    den = jnp.sum(pc, axis=-1, keepdims=True)
    pc = pc * jnp.where(den > 0.0, 1.0 / den, 0.0)
    o_cmp = _dot(pc.reshape(R, NCP).astype(BF16), vc_ref[...])

    p_sum = jnp.sum(pc, axis=0)
    imp_t = _dot_nt(ovl_ref[...], p_sum, precision=lax.Precision.HIGHEST)
    s_io = lax.broadcasted_iota(jnp.int32, (NSP, TQ), 0)
    jcur = (t0 + lax.broadcasted_iota(jnp.int32, (NSP, TQ), 1)) >> SEL_SHIFT
    forced = (s_io == 0) | (s_io == jcur) | (s_io == jcur - 1)
    score = jnp.where(forced, 1e6, jnp.where(s_io <= jcur, imp_t, -1e6))
    cnt = jnp.zeros((NSP, TQ), jnp.int32)
    for sp in range(n_sel_blocks):
        row = score[sp:sp + 1, :]
        beats = (row > score) | ((row == score) & (s_io > sp))
        cnt = cnt + jnp.where(beats, 1, 0)
    sel_t = ((cnt < n_top) & (s_io < n_sel_blocks)).astype(F32)
    sel = sel_t.T.astype(BF16)

    a_io = lax.broadcasted_iota(jnp.int32, (TQ, TK), 0)
    b_io = lax.broadcasted_iota(jnp.int32, (TQ, TK), 1)
    e_row = lax.broadcasted_iota(jnp.int32, (NSP, TK), 0)
    e_col = lax.broadcasted_iota(jnp.int32, (NSP, TK), 1) >> SEL_SHIFT

    def reset():
        m_ref[...] = jnp.full(m_ref.shape, NEG, F32)
        l_ref[...] = jnp.zeros(l_ref.shape, F32)
        acc_ref[...] = jnp.zeros(acc_ref.shape, F32)

    def attend(j, k_ref, v_ref, mask):
        k = k_ref[pl.ds(pl.multiple_of(j * TK, TK), TK), :]
        v = v_ref[pl.ds(pl.multiple_of(j * TK, TK), TK), :]
        dd = jnp.minimum(i - j, N_BIAS_TILES - 1)
        s = _dot_nt(q, k).reshape(HPG, TQ, TK) + bt_ref[dd]
        s = jnp.where(mask[None], s, NEG).reshape(R, TK)
        m_old = m_ref[...]
        m_new = jnp.maximum(m_old, jnp.max(s, axis=-1, keepdims=True))
        alpha = jnp.exp(m_old - m_new)
        p = jnp.exp(s - m_new)
        l_ref[...] = alpha * l_ref[...] + jnp.sum(p, axis=-1, keepdims=True)
        acc_ref[...] = alpha * acc_ref[...] + _dot(p.astype(BF16), v)
        m_ref[...] = m_new

    def sel_mask(j):
        expand = (e_row == (j * (TK // SEL_BLOCK) + e_col)).astype(BF16)
        return _dot(sel, expand) > 0.5

    reset()

    def sel_step(j, carry):
        attend(j, ks_ref, vs_ref, sel_mask(j))
        return carry

    lax.fori_loop(0, i, sel_step, 0)
    attend(i, ks_ref, vs_ref, sel_mask(i) & (b_io <= a_io))
    o_slc = acc_ref[...] * (1.0 / l_ref[...])

    reset()

    def win_step(j, carry):
        dist = (i - j) * TK + a_io - b_io
        attend(j, kw_ref, vw_ref, (dist >= 0) & (dist < WINDOW))
        return carry

    lax.fori_loop(jnp.maximum(i - WINDOW // TK, 0), i + 1, win_step, 0)
    o_win = acc_ref[...] * (1.0 / l_ref[...])

    gate = jax.nn.sigmoid(gate_ref[...])
    for h in range(HPG):
        rows = slice(h * TQ, (h + 1) * TQ)
        o_h = (o_cmp[rows] * gate[:, 3 * h:3 * h + 1]
               + o_slc[rows] * gate[:, 3 * h + 1:3 * h + 2]
               + o_win[rows] * gate[:, 3 * h + 2:3 * h + 3])
        o_ref[h] = o_h.astype(o_ref.dtype)


def _nsa(q, gates, kvc, kv, bias_cmp, bias_toep, ovl_t):
    B, G, HPG, S, DH = q.shape
    TQ = ATT_TILE
    NCP = kvc.shape[3]
    NSP = ovl_t.shape[0]
    ns = S // SEL_BLOCK
    assert S % TQ == 0 and ns <= NSP and WINDOW % TQ == 0

    def seq_spec(idx):
        return pl.BlockSpec((None, None, None, S, DH), lambda b, g, i: (idx, b, g, 0, 0))

    def cmp_spec(idx):
        return pl.BlockSpec((None, None, None, NCP, DH), lambda b, g, i: (idx, b, g, 0, 0))

    return pl.pallas_call(
        functools.partial(_nsa_body, n_sel_blocks=ns, n_top=min(SEL_TOPN, ns)),
        grid=(B, G, S // TQ),
        in_specs=[
            pl.BlockSpec((None, None, HPG, TQ, DH), lambda b, g, i: (b, g, 0, i, 0)),
            pl.BlockSpec((None, None, TQ, 3 * HPG), lambda b, g, i: (b, g, i, 0)),
            cmp_spec(0), cmp_spec(1),
            seq_spec(0), seq_spec(1), seq_spec(2), seq_spec(3),
            pl.BlockSpec((HPG, TQ, NCP), lambda b, g, i: (g, i, 0)),
            pl.BlockSpec((N_BIAS_TILES, HPG, TQ, TQ), lambda b, g, i: (0, g, 0, 0)),
            pl.BlockSpec((NSP, NCP), lambda b, g, i: (0, 0)),
        ],
        out_specs=pl.BlockSpec((None, None, HPG, TQ, DH), lambda b, g, i: (b, g, 0, i, 0)),
        out_shape=jax.ShapeDtypeStruct((B, G, HPG, S, DH), BF16),
        scratch_shapes=[pltpu.VMEM((HPG * TQ, 1), F32), pltpu.VMEM((HPG * TQ, 1), F32),
                        pltpu.VMEM((HPG * TQ, DH), F32)],
        compiler_params=_params("parallel", "parallel", "arbitrary"),
        name="nsa",
    )(q, gates, kvc, kvc, kv, kv, kv, kv, bias_cmp, bias_toep, ovl_t)


def _gla_body(q_ref, k_ref, v_ref, r_ref, al_ref, w2_ref, ab_ref, gn_ref, o_ref, st_ref):
    @pl.when(pl.program_id(1) == 0)
    def _():
        st_ref[...] = jnp.zeros_like(st_ref)

    C = GLA_CHUNK
    n_chunks = q_ref.shape[0] // C
    hi = lax.Precision.HIGHEST
    pre = _dot(al_ref[...], w2_ref[...], precision=hi) + ab_ref[...]
    la = (jnp.minimum(pre, 0.0) - jnp.log1p(jnp.exp(-jnp.abs(pre)))) * (1.0 / GLA_TAU)
    r_io = lax.broadcasted_iota(jnp.int32, (C, C), 0)
    c_io = lax.broadcasted_iota(jnp.int32, (C, C), 1)
    causal = c_io <= r_io
    tri = causal.astype(F32)
    for c in range(n_chunks):
        rows = slice(c * C, (c + 1) * C)
        for h in range(GLA_HEADS):
            kc = slice(h * GLA_DK, (h + 1) * GLA_DK)
            vc = slice(h * GLA_DV, (h + 1) * GLA_DV)
            b = _dot(tri, la[rows, kc], precision=hi)
            b_last = b[C - 1:C, :]
            k = k_ref[rows, kc]
            q_dec = (q_ref[rows, kc] * (GLA_DK ** -0.5) * jnp.exp(b)).astype(BF16)
            k_intra = (k * jnp.exp(-b)).astype(BF16)
            k_state = (k * jnp.exp(b_last - b)).astype(BF16)
            v = v_ref[rows, vc].astype(BF16)
            st = st_ref[h]
            a = jnp.where(causal, _dot_nt(q_dec, k_intra), 0.0)
            o = _dot(a.astype(BF16), v) + _dot_nt(q_dec, st.astype(BF16))
            st_ref[h] = st * jnp.exp(b_last) + _dot_tn(v, k_state)
            o = _rms(o, gn_ref[:, vc])
            r = r_ref[rows, vc]
            o_ref[rows, vc] = (o * (r * jax.nn.sigmoid(r))).astype(o_ref.dtype)


def _gla(proj3, w2, ab, gn, layer):
    B, S, _ = proj3.shape
    RB = GLA_STEP
    HK = GLA_HEADS * GLA_DK
    HV = GLA_HEADS * GLA_DV
    assert S % RB == 0 and RB % GLA_CHUNK == 0

    def col(width, offset):
        assert offset % width == 0
        return pl.BlockSpec((None, RB, width), lambda b, s: (b, s, offset // width))

    return pl.pallas_call(
        _gla_body,
        grid=(B, S // RB),
        in_specs=[
            col(HK, C_QB), col(HK, C_KB), col(HV, C_VB), col(HV, C_RB), col(LANES, C_AL),
            pl.BlockSpec((None, LANES, HK), lambda b, s: (layer, 0, 0)),
            pl.BlockSpec((None, 1, HK), lambda b, s: (layer, 0, 0)),
            pl.BlockSpec((None, 1, HV), lambda b, s: (layer, 0, 0)),
        ],
        out_specs=pl.BlockSpec((None, RB, HV), lambda b, s: (b, s, 0)),
        out_shape=jax.ShapeDtypeStruct((B, S, HV), BF16),
        scratch_shapes=[pltpu.VMEM((GLA_HEADS, GLA_DV, GLA_DK), F32)],
        compiler_params=_params("parallel", "arbitrary"),
        name="gla",
    )(proj3, proj3, proj3, proj3, proj3, w2, ab, gn)


def _merge_body(x_ref, oa_ref, ob_ref, gm_ref, wa_ref, wb_ref, wo_ref, o_ref):
    D = x_ref.shape[1]
    gm = gm_ref[...]
    y = (jax.nn.sigmoid(gm[:, :D]) * _dot(oa_ref[...], wa_ref[...])
         + jax.nn.sigmoid(gm[:, D:]) * _dot(ob_ref[...], wb_ref[...]))
    o_ref[...] = x_ref[...] + _dot(y.astype(BF16), wo_ref[...])


def _merge(x, o_a, o_b, proj, wa, wb, wo, layer, tm=512):
    T, D = x.shape
    DA = o_a.shape[1]
    DB = o_b.shape[1]
    assert T % tm == 0 and C_GM == 0
    return pl.pallas_call(
        _merge_body,
        grid=(T // tm,),
        in_specs=[
            pl.BlockSpec((tm, D), lambda i: (i, 0)),
            pl.BlockSpec((tm, DA), lambda i: (i, 0)),
            pl.BlockSpec((tm, DB), lambda i: (i, 0)),
            pl.BlockSpec((tm, 2 * D), lambda i: (i, 0)),
            pl.BlockSpec((None, DA, D), lambda i: (layer, 0, 0)),
            pl.BlockSpec((None, DB, D), lambda i: (layer, 0, 0)),
            pl.BlockSpec((None, D, D), lambda i: (layer, 0, 0)),
        ],
        out_specs=pl.BlockSpec((tm, D), lambda i: (i, 0)),
        out_shape=jax.ShapeDtypeStruct((T, D), F32),
        compiler_params=_params("parallel"),
        name="merge",
    )(x, o_a, o_b, proj, wa, wb, wo)


def _rel_bucket(dist):
    n = jnp.maximum(dist, 0)
    exact = REL_BUCKETS // 2
    nf = jnp.maximum(n, 1).astype(jnp.float32)
    log_b = exact + (jnp.log(nf / exact) / math.log(REL_MAX_DIST / exact)
                     * (REL_BUCKETS - exact)).astype(jnp.int32)
    return jnp.where(n < exact, n, jnp.minimum(log_b, REL_BUCKETS - 1))


def _regroup_w_in(w_in):
    widths = (NSA_HEADS * NSA_DH, 6 * NSA_GROUPS * NSA_DH, 3 * NSA_HEADS, GLA_HEADS * GLA_DK,
              GLA_HEADS * GLA_DK, GLA_HEADS * GLA_DV, GLA_RANK, GLA_HEADS * GLA_DV, 2 * D_MODEL)
    offs = np.concatenate([[0], np.cumsum(widths)])
    q_a, kv_a, g_a, q_b, k_b, v_b, a_lr, r_b, g_m = (w_in[..., offs[n]:offs[n + 1]] for n in range(9))

    def pad(w):
        return jnp.pad(w, ((0, 0), (0, 0), (0, LANES - w.shape[-1])))

    out = jnp.concatenate([g_m, v_b, r_b, q_a, q_b, k_b, kv_a, pad(g_a), pad(a_lr)], axis=-1)
    assert out.shape[-1] == N_PROJ
    return out.astype(BF16)


def _overlap_t(ncp, nsp, nc, ns):
    c = np.arange(ncp)[None, :] * CMP_STRIDE
    s = np.arange(nsp)[:, None] * SEL_BLOCK
    ov = (c < s + SEL_BLOCK) & (c + CMP_BLOCK > s) & (np.arange(ncp)[None, :] < nc) & (np.arange(nsp)[:, None] < ns)
    return jnp.asarray(ov.astype(np.float32))


def kernel(x, rel_table, ffn1_norm, ffn1_w_gate, ffn1_w_up, ffn1_w_down, mix_norm, w_in, cmp_pos_k, cmp_pos_v, cmp_k_w1, cmp_k_w2, cmp_v_w1, cmp_v_w2, gla_a_w2, gla_a_b, gla_out_norm, w_branch_nsa, w_branch_gla, w_out, ffn2_norm, ffn2_w_gate, ffn2_w_up, ffn2_w_down, final_norm):
    B, S, D = x.shape
    L = w_in.shape[0]
    T = B * S
    G, HPG, DH = NSA_GROUPS, NSA_HPG, NSA_DH
    nch = S // CMP_STRIDE
    nc = (S - CMP_BLOCK) // CMP_STRIDE + 1
    ns = S // SEL_BLOCK
    assert D == D_MODEL and nc == nch - 1

    w1g, w1u, w1d = ffn1_w_gate.astype(BF16), ffn1_w_up.astype(BF16), ffn1_w_down.astype(BF16)
    w2g, w2u, w2d = ffn2_w_gate.astype(BF16), ffn2_w_up.astype(BF16), ffn2_w_down.astype(BF16)
    w_proj = _regroup_w_in(w_in)
    wa, wb, wo = w_branch_nsa.astype(BF16), w_branch_gla.astype(BF16), w_out.astype(BF16)
    cmp_pos = jnp.stack([cmp_pos_k, cmp_pos_v]).reshape(2, L, 1, CMP_BLOCK * DH)
    cmp_w1 = jnp.stack([cmp_k_w1, cmp_v_w1]).astype(BF16)
    cmp_w2 = jnp.stack([cmp_k_w2, cmp_v_w2]).astype(BF16)
    gla_w2 = jnp.pad(gla_a_w2, ((0, 0), (0, LANES - GLA_RANK), (0, 0)))
    gla_b = gla_a_b.reshape(L, 1, -1)
    gla_gn = gla_out_norm.reshape(L, 1, -1)
    n1 = ffn1_norm.reshape(L, 1, D)
    n2 = ffn2_norm.reshape(L, 1, D)
    nm = mix_norm.reshape(L, 1, D)

    buckets = _rel_bucket(jnp.arange(REL_MAX_DIST + 1, dtype=jnp.int32))
    thr = jnp.searchsorted(buckets, jnp.arange(REL_BUCKETS, dtype=jnp.int32), side="left").astype(jnp.int32)
    bias_toep, bias_cmp = _bias_tables(thr, rel_table, S, nch)
    ovl_t = _overlap_t(nch, LANES, nc, ns)

    xf = x.reshape(T, D)
    for l in range(L):
        xf = _ffn(xf, n1, w1g, w1u, w1d, l)
        proj = _proj(xf, nm, w_proj, l)
        proj3 = proj.reshape(B, S, N_PROJ)

        q = proj3[:, :, C_QA:C_QA + NSA_HEADS * DH].reshape(B, S, G, HPG, DH)
        q = q.transpose(0, 2, 3, 1, 4).astype(BF16)
        kv6 = proj3[:, :, C_KV:C_KV + 6 * G * DH].reshape(B, S, 6, G, DH)
        xc = kv6[:, :, 0:2].reshape(B, nch, CMP_STRIDE, 2, G, DH)
        xc = xc.transpose(3, 0, 4, 1, 2, 5).reshape(2, B, G, nch, CMP_STRIDE * DH)
        kv = kv6[:, :, 2:6].transpose(2, 0, 3, 1, 4).astype(BF16)
        gates = proj3[:, :, C_GA:C_GA + 3 * NSA_HEADS].reshape(B, S, G, 3 * HPG).transpose(0, 2, 1, 3)

        kvc = _compress(xc, cmp_pos, cmp_w1, cmp_w2, l)
        o_a = _nsa(q, gates, kvc, kv, bias_cmp, bias_toep, ovl_t)
        o_a = o_a.transpose(0, 3, 1, 2, 4).reshape(T, NSA_HEADS * DH)

        o_b = _gla(proj3, gla_w2, gla_b, gla_gn, l).reshape(T, GLA_HEADS * GLA_DV)

        xf = _merge(xf, o_a, o_b, proj, wa, wb, wo, l)
        xf = _ffn(xf, n2, w2g, w2u, w2d, l,
                  final_g=final_norm.reshape(1, D) if l == L - 1 else None)
    return xf.reshape(B, S, D)
```

```python
import functools
import math

import numpy as np
import jax
import jax.numpy as jnp
from jax import lax
from jax.experimental import pallas as pl
from jax.experimental.pallas import tpu as pltpu

F32 = jnp.float32
BF16 = jnp.bfloat16

NSA_HEADS = 8
NSA_GROUPS = 2
NSA_HPG = NSA_HEADS // NSA_GROUPS
NSA_DH = 64
CMP_BLOCK = 32
CMP_STRIDE = 16
SEL_BLOCK = 64
SEL_SHIFT = 6
SEL_TOPN = 16
WINDOW = 512
GLA_HEADS = 4
GLA_DK = 128
GLA_DV = 256
GLA_RANK = 16
GLA_TAU = 16.0
GLA_CHUNK = 64
REL_BUCKETS = 32
REL_MAX_DIST = 1024
EPS = 1e-6
NEG = -1e30

LANES = 128
VMEM_LIMIT = 56 * 1024 * 1024

ATT_TILE = 256
N_BIAS_TILES = REL_MAX_DIST // ATT_TILE + 2
N_WIN_TILES = WINDOW // ATT_TILE + 1
NSA_SUB = 128
GLA_STEP = 256

D_MODEL = 1024
C_GM = 0
C_VB = 2048
C_RB = 3072
C_QA = 4096
C_QB = 4608
C_KB = 5120
C_KV = 5632
C_GA = 6400
C_AL = 6528
N_PROJ = 6656


def _dot(a, b, precision=None):
    return lax.dot_general(a, b, (((1,), (0,)), ((), ())), precision=precision,
                           preferred_element_type=F32)


def _dot_nt(a, b, precision=None):
    return lax.dot_general(a, b, (((1,), (1,)), ((), ())), precision=precision,
                           preferred_element_type=F32)


def _dot_tn(a, b, precision=None):
    return lax.dot_general(a, b, (((0,), (0,)), ((), ())), precision=precision,
                           preferred_element_type=F32)


def _rms(x, g):
    return x * lax.rsqrt(jnp.mean(x * x, axis=-1, keepdims=True) + EPS) * g


def _params(*sem):
    return pltpu.CompilerParams(dimension_semantics=sem, vmem_limit_bytes=VMEM_LIMIT)


def _ffn_body(x_ref, g_ref, wg_ref, wu_ref, wd_ref, *rest, final):
    if final:
        fg_ref, o_ref, h_ref, acc_ref = rest
    else:
        o_ref, h_ref, acc_ref = rest
    j = pl.program_id(1)

    @pl.when(j == 0)
    def _():
        h_ref[...] = _rms(x_ref[...], g_ref[...]).astype(BF16)
        acc_ref[...] = jnp.zeros_like(acc_ref)

    h = h_ref[...]
    gate = _dot(h, wg_ref[...])
    up = _dot(h, wu_ref[...])
    act = (gate * jax.nn.sigmoid(gate) * up).astype(BF16)
    acc_ref[...] += _dot(act, wd_ref[...])

    @pl.when(j == pl.num_programs(1) - 1)
    def _():
        y = x_ref[...] + 0.5 * acc_ref[...]
        if final:
            y = _rms(y, fg_ref[...])
        o_ref[...] = y


def _ffn(x, g, wg, wu, wd, layer, final_g=None, tm=512, tf=1408):
    T, D = x.shape
    F = wg.shape[-1]
    assert T % tm == 0 and F % tf == 0
    final = final_g is not None
    in_specs = [
        pl.BlockSpec((tm, D), lambda i, j: (i, 0)),
        pl.BlockSpec((None, 1, D), lambda i, j: (layer, 0, 0)),
        pl.BlockSpec((None, D, tf), lambda i, j: (layer, 0, j)),
        pl.BlockSpec((None, D, tf), lambda i, j: (layer, 0, j)),
        pl.BlockSpec((None, tf, D), lambda i, j: (layer, j, 0)),
    ]
    args = [x, g, wg, wu, wd]
    if final:
        in_specs.append(pl.BlockSpec((1, D), lambda i, j: (0, 0)))
        args.append(final_g)
    return pl.pallas_call(
        functools.partial(_ffn_body, final=final),
        grid=(T // tm, F // tf),
        in_specs=in_specs,
        out_specs=pl.BlockSpec((tm, D), lambda i, j: (i, 0)),
        out_shape=jax.ShapeDtypeStruct((T, D), F32),
        scratch_shapes=[pltpu.VMEM((tm, D), BF16), pltpu.VMEM((tm, D), F32)],
        compiler_params=_params("parallel", "arbitrary"),
        name="ffn",
    )(*args)


def _proj_body(x_ref, g_ref, w_ref, o_ref, h_ref):
    @pl.when(pl.program_id(1) == 0)
    def _():
        h_ref[...] = _rms(x_ref[...], g_ref[...]).astype(BF16)

    o_ref[...] = _dot(h_ref[...], w_ref[...])


def _proj(x, g, w, layer, tm=1024, tn=1664):
    T, D = x.shape
    N = w.shape[-1]
    assert T % tm == 0 and N % tn == 0
    return pl.pallas_call(
        _proj_body,
        grid=(T // tm, N // tn),
        in_specs=[
            pl.BlockSpec((tm, D), lambda i, j: (i, 0)),
            pl.BlockSpec((None, 1, D), lambda i, j: (layer, 0, 0)),
            pl.BlockSpec((None, D, tn), lambda i, j: (layer, 0, j)),
        ],
        out_specs=pl.BlockSpec((tm, tn), lambda i, j: (i, j)),
        out_shape=jax.ShapeDtypeStruct((T, N), F32),
        scratch_shapes=[pltpu.VMEM((tm, D), BF16)],
        compiler_params=_params("parallel", "arbitrary"),
        name="proj",
    )(x, g, w)


def _compress_body(x_ref, pos_ref, w1_ref, w2_ref, o_ref):
    x = x_ref[...]
    half = x.shape[1]
    lo = (x + pos_ref[:, :half]).astype(BF16)
    hi = (x + pos_ref[:, half:]).astype(BF16)
    h_lo = _dot(lo, w1_ref[:half, :])
    h_hi = _dot(hi, w1_ref[half:, :])
    nch = x.shape[0]
    hid = h_lo + pltpu.roll(h_hi, nch - 1, 0)
    act = (hid * jax.nn.sigmoid(hid)).astype(BF16)
    o_ref[...] = _dot(act, w2_ref[...]).astype(o_ref.dtype)


def _compress(xc, pos, w1, w2, layer):
    _, B, G, NCH, CW = xc.shape
    HC = w1.shape[-1]
    dh = w2.shape[-1]
    return pl.pallas_call(
        _compress_body,
        grid=(2, B, G),
        in_specs=[
            pl.BlockSpec((None, None, None, NCH, CW), lambda s, b, g: (s, b, g, 0, 0)),
            pl.BlockSpec((None, None, 1, 2 * CW), lambda s, b, g: (s, layer, 0, 0)),
            pl.BlockSpec((None, None, 2 * CW, HC), lambda s, b, g: (s, layer, 0, 0)),
            pl.BlockSpec((None, None, HC, dh), lambda s, b, g: (s, layer, 0, 0)),
        ],
        out_specs=pl.BlockSpec((None, None, None, NCH, dh), lambda s, b, g: (s, b, g, 0, 0)),
        out_shape=jax.ShapeDtypeStruct((2, B, G, NCH, dh), BF16),
        compiler_params=_params("parallel", "parallel", "parallel"),
        name="compress",
    )(xc, pos, w1, w2)


def _bias_lookup(n, thr_ref, tab_ref, h):
    val = jnp.full(n.shape, tab_ref[0, h], F32)
    for k in range(1, REL_BUCKETS):
        val = jnp.where(n >= thr_ref[k], tab_ref[k, h], val)
    return val


def _toeplitz_body(thr_ref, tab_ref, o_ref, *, rows, window):
    dd = pl.program_id(0)
    h = pl.program_id(1)
    T = o_ref.shape[-1]
    for r0 in range(0, T, rows):
        a = lax.broadcasted_iota(jnp.int32, (rows, T), 0) + r0
        b = lax.broadcasted_iota(jnp.int32, (rows, T), 1)
        dist = dd * T + a - b
        val = _bias_lookup(jnp.clip(dist, 0, REL_MAX_DIST), thr_ref, tab_ref, h)
        if window:
            val = jnp.where((dist >= 0) & (dist < WINDOW), val, NEG)
        o_ref[r0:r0 + rows, :] = val


def _cmpbias_body(thr_ref, tab_ref, o_ref, *, rows):
    h = pl.program_id(0)
    i = pl.program_id(1)
    TQ, NC = o_ref.shape
    for r0 in range(0, TQ, rows):
        t = lax.broadcasted_iota(jnp.int32, (rows, NC), 0) + (i * TQ + r0)
        c = lax.broadcasted_iota(jnp.int32, (rows, NC), 1)
        n = jnp.clip(t - (c * CMP_STRIDE + (CMP_BLOCK - 1)), 0, REL_MAX_DIST)
        o_ref[r0:r0 + rows, :] = _bias_lookup(n, thr_ref, tab_ref, h)


def _bias_tables(thr, rel_table, S, ncp):
    T = ATT_TILE
    smem = pl.BlockSpec(memory_space=pltpu.SMEM)
    def toeplitz(n_tiles, window, name):
        return pl.pallas_call(
            functools.partial(_toeplitz_body, rows=32, window=window),
            grid=(n_tiles, NSA_HEADS),
            in_specs=[smem, smem],
            out_specs=pl.BlockSpec((None, None, T, T), lambda d, h: (d, h, 0, 0)),
            out_shape=jax.ShapeDtypeStruct((n_tiles, NSA_HEADS, T, T), F32),
            compiler_params=_params("parallel", "parallel"),
            name=name,
        )(thr, rel_table)

    toep = toeplitz(N_BIAS_TILES, False, "bias_toeplitz")
    toep_win = toeplitz(N_WIN_TILES, True, "bias_window")
    cmpb = pl.pallas_call(
        functools.partial(_cmpbias_body, rows=32),
        grid=(NSA_HEADS, S // T),
        in_specs=[smem, smem],
        out_specs=pl.BlockSpec((None, T, ncp), lambda h, i: (h, i, 0)),
        out_shape=jax.ShapeDtypeStruct((NSA_HEADS, S, ncp), F32),
        compiler_params=_params("parallel", "parallel"),
        name="bias_cmp",
    )(thr, rel_table)
    return toep, toep_win, cmpb


def _nsa_body(q_ref, gate_ref, kc_ref, vc_ref, ks_ref, vs_ref, kw_ref, vw_ref, bc_ref, bts_ref, btw_ref,
              ovl_ref, o_ref, qs_ref, mask_ref, m_ref, acc_ref, oacc_ref, *, n_sel_blocks, n_top):
    i = pl.program_id(2)
    HPG, TQ, DH = q_ref.shape
    TK = TQ
    SB = NSA_SUB
    NCP = kc_ref.shape[0]
    NSP = ovl_ref.shape[0]
    t0 = i * TQ
    subs = [(h, a0) for a0 in range(0, TQ, SB) for h in range(HPG)]

    qs_ref[...] = (q_ref[...] * (DH ** -0.5)).astype(BF16)

    def gate_col(a0, col):
        return jax.nn.sigmoid(gate_ref[a0:a0 + SB, col:col + 1])

    kc = kc_ref[...]
    vc = vc_ref[...]
    c_end = lax.broadcasted_iota(jnp.int32, (SB, NCP), 1) * CMP_STRIDE + (CMP_BLOCK - 1)
    r_c = lax.broadcasted_iota(jnp.int32, (SB, NCP), 0)
    p_parts = []
    for a0 in range(0, TQ, SB):
        mc = c_end <= (t0 + a0 + r_c)
        p_heads = None
        for h in range(HPG):
            lc = _dot_nt(qs_ref[h, a0:a0 + SB, :], kc) + bc_ref[h, a0:a0 + SB, :]
            lc = jnp.where(mc, lc, NEG)
            pc = jnp.where(mc, jnp.exp(lc - jnp.max(lc, axis=-1, keepdims=True)), 0.0)
            den = jnp.sum(pc, axis=-1, keepdims=True)
            pc = pc * jnp.where(den > 0.0, 1.0 / den, 0.0)
            oacc_ref[h, a0:a0 + SB, :] = gate_col(a0, 3 * h) * _dot(pc.astype(BF16), vc)
            p_heads = pc if p_heads is None else p_heads + pc
        p_parts.append(p_heads)
    p_sum = jnp.concatenate(p_parts, axis=0)

    imp_t = _dot_nt(ovl_ref[...], p_sum, precision=lax.Precision.HIGHEST)
    s_io = lax.broadcasted_iota(jnp.int32, (NSP, TQ), 0)
    jcur = (t0 + lax.broadcasted_iota(jnp.int32, (NSP, TQ), 1)) >> SEL_SHIFT
    forced = (s_io == 0) | (s_io == jcur) | (s_io == jcur - 1)
    score = jnp.where(forced, 1e6, jnp.where(s_io <= jcur, imp_t, -1e6))
    cnt = jnp.zeros((NSP, TQ), jnp.int32)
    for sp in range(n_sel_blocks):
        row = score[sp:sp + 1, :]
        beats = (row > score) | ((row == score) & (s_io > sp))
        cnt = cnt + jnp.where(beats, 1, 0)
    sel_t = ((cnt < n_top) & (s_io < n_sel_blocks)).astype(F32)
    sel = sel_t.T.astype(BF16)

    a_io = lax.broadcasted_iota(jnp.int32, (TQ, TK), 0)
    b_io = lax.broadcasted_iota(jnp.int32, (TQ, TK), 1)
    e_row = lax.broadcasted_iota(jnp.int32, (NSP, TK), 0)
    e_col = lax.broadcasted_iota(jnp.int32, (NSP, TK), 1) >> SEL_SHIFT

    def fill_mask(j, diag):
        expand = (e_row == (j * (TK // SEL_BLOCK) + e_col)).astype(BF16)
        keep = _dot(sel, expand) > 0.5
        if diag:
            keep = keep & (b_io <= a_io)
        mask_ref[j] = jnp.where(keep, 0.0, NEG)

    def fill_step(j, carry):
        fill_mask(j, False)
        return carry

    lax.fori_loop(0, i, fill_step, 0)
    fill_mask(i, True)

    def sweep(k_ref, v_ref, j_lo, logit_add, gate_off):
        m_ref[...] = jnp.full(m_ref.shape, NEG, F32)
        acc_ref[...] = jnp.zeros(acc_ref.shape, F32)

        def step(j, carry):
            off = pl.multiple_of(j * TK, TK)
            k = k_ref[pl.ds(off, TK), :]
            v = v_ref[pl.ds(off, TK), :]
            for n, (h, a0) in enumerate(subs):
                rows = slice(n * SB, (n + 1) * SB)
                s = _dot_nt(qs_ref[h, a0:a0 + SB, :], k) + logit_add(j, h, a0)
                m_old = m_ref[rows, :]
                m_new = jnp.maximum(m_old, jnp.max(s, axis=-1, keepdims=True))
                alpha = jnp.exp(m_old - m_new)
                p = jnp.exp(s - jnp.concatenate([m_new] * (TK // LANES), axis=1))
                acc_ref[rows, :] = alpha * acc_ref[rows, :] + _dot(p.astype(BF16), v)
                m_ref[rows, :] = m_new
            return carry

        lax.fori_loop(j_lo, i + 1, step, 0)
        for n, (h, a0) in enumerate(subs):
            acc = acc_ref[n * SB:(n + 1) * SB, :]
            o = (acc * (1.0 / pltpu.roll(acc, DH, 1)))[:, :DH]
            oacc_ref[h, a0:a0 + SB, :] += gate_col(a0, 3 * h + gate_off) * o

    def sel_add(j, h, a0):
        dd = jnp.minimum(i - j, N_BIAS_TILES - 1)
        return bts_ref[dd, h, a0:a0 + SB, :] + mask_ref[j, a0:a0 + SB, :]

    sweep(ks_ref, vs_ref, 0, sel_add, 1)

    def win_add(j, h, a0):
        return btw_ref[i - j, h, a0:a0 + SB, :]

    sweep(kw_ref, vw_ref, jnp.maximum(i - (N_WIN_TILES - 1), 0), win_add, 2)

    o_ref[...] = oacc_ref[...].astype(o_ref.dtype)


def _nsa(q, gates, kvc, ksw, vsw, bias_cmp, bias_toep, bias_win, ovl_t):
    B, G, HPG, S, DH = q.shape
    TQ = ATT_TILE
    NCP = kvc.shape[3]
    NSP = ovl_t.shape[0]
    ns = S // SEL_BLOCK
    assert S % TQ == 0 and TQ % NSA_SUB == 0 and ns <= NSP and WINDOW % TQ == 0 and 2 * DH == LANES

    def seq_spec(idx, width):
        return pl.BlockSpec((None, None, None, S, width), lambda b, g, i: (idx, b, g, 0, 0))

    def cmp_spec(idx):
        return pl.BlockSpec((None, None, None, NCP, DH), lambda b, g, i: (idx, b, g, 0, 0))

    return pl.pallas_call(
        functools.partial(_nsa_body, n_sel_blocks=ns, n_top=min(SEL_TOPN, ns)),
        grid=(B, G, S // TQ),
        in_specs=[
            pl.BlockSpec((None, None, HPG, TQ, DH), lambda b, g, i: (b, g, 0, i, 0)),
            pl.BlockSpec((None, None, TQ, 3 * HPG), lambda b, g, i: (b, g, i, 0)),
            cmp_spec(0), cmp_spec(1),
            seq_spec(0, DH), seq_spec(0, 2 * DH), seq_spec(1, DH), seq_spec(1, 2 * DH),
            pl.BlockSpec((HPG, TQ, NCP), lambda b, g, i: (g, i, 0)),
            pl.BlockSpec((N_BIAS_TILES, HPG, TQ, TQ), lambda b, g, i: (0, g, 0, 0)),
            pl.BlockSpec((N_WIN_TILES, HPG, TQ, TQ), lambda b, g, i: (0, g, 0, 0)),
            pl.BlockSpec((NSP, NCP), lambda b, g, i: (0, 0)),
        ],
        out_specs=pl.BlockSpec((None, None, HPG, TQ, DH), lambda b, g, i: (b, g, 0, i, 0)),
        out_shape=jax.ShapeDtypeStruct((B, G, HPG, S, DH), BF16),
        scratch_shapes=[
            pltpu.VMEM((HPG, TQ, DH), BF16),
            pltpu.VMEM((S // TQ, TQ, TQ), F32),
            pltpu.VMEM((HPG * TQ, LANES), F32),
            pltpu.VMEM((HPG * TQ, 2 * DH), F32),
            pltpu.VMEM((HPG, TQ, DH), F32),
        ],
        compiler_params=_params("parallel", "parallel", "arbitrary"),
        name="nsa",
    )(q, gates, kvc, kvc, ksw, vsw, ksw, vsw, bias_cmp, bias_toep, bias_win, ovl_t)


def _gla_body(q_ref, k_ref, v_ref, r_ref, al_ref, w2_ref, ab_ref, gn_ref, o_ref, st_ref):
    @pl.when(pl.program_id(1) == 0)
    def _():
        st_ref[...] = jnp.zeros_like(st_ref)

    C = GLA_CHUNK
    n_chunks = q_ref.shape[0] // C
    hi = lax.Precision.HIGHEST
    pre = _dot(al_ref[...], w2_ref[...], precision=hi) + ab_ref[...]
    la = (jnp.minimum(pre, 0.0) - jnp.log1p(jnp.exp(-jnp.abs(pre)))) * (1.0 / GLA_TAU)
    r_io = lax.broadcasted_iota(jnp.int32, (C, C), 0)
    c_io = lax.broadcasted_iota(jnp.int32, (C, C), 1)
    causal = c_io <= r_io
    tri = causal.astype(F32)
    for c in range(n_chunks):
        rows = slice(c * C, (c + 1) * C)
        for h in range(GLA_HEADS):
            kc = slice(h * GLA_DK, (h + 1) * GLA_DK)
            vc = slice(h * GLA_DV, (h + 1) * GLA_DV)
            b = _dot(tri, la[rows, kc], precision=hi)
            b_last = b[C - 1:C, :]
            k = k_ref[rows, kc]
            q_dec = (q_ref[rows, kc] * (GLA_DK ** -0.5) * jnp.exp(b)).astype(BF16)
            k_intra = (k * jnp.exp(-b)).astype(BF16)
            k_state = (k * jnp.exp(b_last - b)).astype(BF16)
            v = v_ref[rows, vc].astype(BF16)
            st = st_ref[h]
            a = jnp.where(causal, _dot_nt(q_dec, k_intra), 0.0)
            o = _dot(a.astype(BF16), v) + _dot_nt(q_dec, st.astype(BF16))
            st_ref[h] = st * jnp.exp(b_last) + _dot_tn(v, k_state)
            o = _rms(o, gn_ref[:, vc])
            r = r_ref[rows, vc]
            o_ref[rows, vc] = (o * (r * jax.nn.sigmoid(r))).astype(o_ref.dtype)


def _gla(proj3, w2, ab, gn, layer):
    B, S, _ = proj3.shape
    RB = GLA_STEP
    HK = GLA_HEADS * GLA_DK
    HV = GLA_HEADS * GLA_DV
    assert S % RB == 0 and RB % GLA_CHUNK == 0

    def col(width, offset):
        assert offset % width == 0
        return pl.BlockSpec((None, RB, width), lambda b, s: (b, s, offset // width))

    return pl.pallas_call(
        _gla_body,
        grid=(B, S // RB),
        in_specs=[
            col(HK, C_QB), col(HK, C_KB), col(HV, C_VB), col(HV, C_RB), col(LANES, C_AL),
            pl.BlockSpec((None, LANES, HK), lambda b, s: (layer, 0, 0)),
            pl.BlockSpec((None, 1, HK), lambda b, s: (layer, 0, 0)),
            pl.BlockSpec((None, 1, HV), lambda b, s: (layer, 0, 0)),
        ],
        out_specs=pl.BlockSpec((None, RB, HV), lambda b, s: (b, s, 0)),
        out_shape=jax.ShapeDtypeStruct((B, S, HV), BF16),
        scratch_shapes=[pltpu.VMEM((GLA_HEADS, GLA_DV, GLA_DK), F32)],
        compiler_params=_params("parallel", "arbitrary"),
        name="gla",
    )(proj3, proj3, proj3, proj3, proj3, w2, ab, gn)


def _merge_body(x_ref, oa_ref, ob_ref, gm_ref, wa_ref, wb_ref, wo_ref, o_ref):
    D = x_ref.shape[1]
    gm = gm_ref[...]
    y = (jax.nn.sigmoid(gm[:, :D]) * _dot(oa_ref[...], wa_ref[...])
         + jax.nn.sigmoid(gm[:, D:]) * _dot(ob_ref[...], wb_ref[...]))
    o_ref[...] = x_ref[...] + _dot(y.astype(BF16), wo_ref[...])


def _merge(x, o_a, o_b, proj, wa, wb, wo, layer, tm=512):
    T, D = x.shape
    DA = o_a.shape[1]
    DB = o_b.shape[1]
    assert T % tm == 0 and C_GM == 0
    return pl.pallas_call(
        _merge_body,
        grid=(T // tm,),
        in_specs=[
            pl.BlockSpec((tm, D), lambda i: (i, 0)),
            pl.BlockSpec((tm, DA), lambda i: (i, 0)),
            pl.BlockSpec((tm, DB), lambda i: (i, 0)),
            pl.BlockSpec((tm, 2 * D), lambda i: (i, 0)),
            pl.BlockSpec((None, DA, D), lambda i: (layer, 0, 0)),
            pl.BlockSpec((None, DB, D), lambda i: (layer, 0, 0)),
            pl.BlockSpec((None, D, D), lambda i: (layer, 0, 0)),
        ],
        out_specs=pl.BlockSpec((tm, D), lambda i: (i, 0)),
        out_shape=jax.ShapeDtypeStruct((T, D), F32),
        compiler_params=_params("parallel"),
        name="merge",
    )(x, o_a, o_b, proj, wa, wb, wo)


def _rel_bucket(dist):
    n = jnp.maximum(dist, 0)
    exact = REL_BUCKETS // 2
    nf = jnp.maximum(n, 1).astype(jnp.float32)
    log_b = exact + (jnp.log(nf / exact) / math.log(REL_MAX_DIST / exact)
                     * (REL_BUCKETS - exact)).astype(jnp.int32)
    return jnp.where(n < exact, n, jnp.minimum(log_b, REL_BUCKETS - 1))


def _regroup_w_in(w_in):
    widths = (NSA_HEADS * NSA_DH, 6 * NSA_GROUPS * NSA_DH, 3 * NSA_HEADS, GLA_HEADS * GLA_DK,
              GLA_HEADS * GLA_DK, GLA_HEADS * GLA_DV, GLA_RANK, GLA_HEADS * GLA_DV, 2 * D_MODEL)
    offs = np.concatenate([[0], np.cumsum(widths)])
    q_a, kv_a, g_a, q_b, k_b, v_b, a_lr, r_b, g_m = (w_in[..., offs[n]:offs[n + 1]] for n in range(9))

    def pad(w):
        return jnp.pad(w, ((0, 0), (0, 0), (0, LANES - w.shape[-1])))

    out = jnp.concatenate([g_m, v_b, r_b, q_a, q_b, k_b, kv_a, pad(g_a), pad(a_lr)], axis=-1)
    assert out.shape[-1] == N_PROJ
    return out.astype(BF16)


def _overlap_t(ncp, nsp, nc, ns):
    c = np.arange(ncp)[None, :] * CMP_STRIDE
    s = np.arange(nsp)[:, None] * SEL_BLOCK
    ov = (c < s + SEL_BLOCK) & (c + CMP_BLOCK > s) & (np.arange(ncp)[None, :] < nc) & (np.arange(nsp)[:, None] < ns)
    return jnp.asarray(ov.astype(np.float32))


def kernel(x, rel_table, ffn1_norm, ffn1_w_gate, ffn1_w_up, ffn1_w_down, mix_norm, w_in, cmp_pos_k, cmp_pos_v, cmp_k_w1, cmp_k_w2, cmp_v_w1, cmp_v_w2, gla_a_w2, gla_a_b, gla_out_norm, w_branch_nsa, w_branch_gla, w_out, ffn2_norm, ffn2_w_gate, ffn2_w_up, ffn2_w_down, final_norm):
    B, S, D = x.shape
    L = w_in.shape[0]
    T = B * S
    G, HPG, DH = NSA_GROUPS, NSA_HPG, NSA_DH
    nch = S // CMP_STRIDE
    nc = (S - CMP_BLOCK) // CMP_STRIDE + 1
    ns = S // SEL_BLOCK
    assert D == D_MODEL and nc == nch - 1

    w1g, w1u, w1d = ffn1_w_gate.astype(BF16), ffn1_w_up.astype(BF16), ffn1_w_down.astype(BF16)
    w2g, w2u, w2d = ffn2_w_gate.astype(BF16), ffn2_w_up.astype(BF16), ffn2_w_down.astype(BF16)
    w_proj = _regroup_w_in(w_in)
    wa, wb, wo = w_branch_nsa.astype(BF16), w_branch_gla.astype(BF16), w_out.astype(BF16)
    cmp_pos = jnp.stack([cmp_pos_k, cmp_pos_v]).reshape(2, L, 1, CMP_BLOCK * DH)
    cmp_w1 = jnp.stack([cmp_k_w1, cmp_v_w1]).astype(BF16)
    cmp_w2 = jnp.stack([cmp_k_w2, cmp_v_w2]).astype(BF16)
    gla_w2 = jnp.pad(gla_a_w2, ((0, 0), (0, LANES - GLA_RANK), (0, 0)))
    gla_b = gla_a_b.reshape(L, 1, -1)
    gla_gn = gla_out_norm.reshape(L, 1, -1)
    n1 = ffn1_norm.reshape(L, 1, D)
    n2 = ffn2_norm.reshape(L, 1, D)
    nm = mix_norm.reshape(L, 1, D)

    buckets = _rel_bucket(jnp.arange(REL_MAX_DIST + 1, dtype=jnp.int32))
    thr = jnp.searchsorted(buckets, jnp.arange(REL_BUCKETS, dtype=jnp.int32), side="left").astype(jnp.int32)
    bias_toep, bias_win, bias_cmp = _bias_tables(thr, rel_table, S, nch)
    ovl_t = _overlap_t(nch, LANES, nc, ns)

    xf = x.reshape(T, D)
    for l in range(L):
        xf = _ffn(xf, n1, w1g, w1u, w1d, l)
        proj = _proj(xf, nm, w_proj, l)
        proj3 = proj.reshape(B, S, N_PROJ)

        q = proj3[:, :, C_QA:C_QA + NSA_HEADS * DH].reshape(B, S, G, HPG, DH)
        q = q.transpose(0, 2, 3, 1, 4).astype(BF16)
        kv6 = proj3[:, :, C_KV:C_KV + 6 * G * DH].reshape(B, S, 6, G, DH)
        xc = kv6[:, :, 0:2].reshape(B, nch, CMP_STRIDE, 2, G, DH)
        xc = xc.transpose(3, 0, 4, 1, 2, 5).reshape(2, B, G, nch, CMP_STRIDE * DH)
        kv = kv6[:, :, 2:6].transpose(2, 0, 3, 1, 4).astype(BF16)
        ksw = kv[0::2]
        vsw = jnp.concatenate([kv[1::2], jnp.ones_like(kv[1::2])], axis=-1)
        gates = proj3[:, :, C_GA:C_GA + 3 * NSA_HEADS].reshape(B, S, G, 3 * HPG).transpose(0, 2, 1, 3)

        kvc = _compress(xc, cmp_pos, cmp_w1, cmp_w2, l)
        o_a = _nsa(q, gates, kvc, ksw, vsw, bias_cmp, bias_toep, bias_win, ovl_t)
        o_a = o_a.transpose(0, 3, 1, 2, 4).reshape(T, NSA_HEADS * DH)

        o_b = _gla(proj3, gla_w2, gla_b, gla_gn, l).reshape(T, GLA_HEADS * GLA_DV)

        xf = _merge(xf, o_a, o_b, proj, wa, wb, wo, l)
        xf = _ffn(xf, n2, w2g, w2u, w2d, l,
                  final_g=final_norm.reshape(1, D) if l == L - 1 else None)
    return xf.reshape(B, S, D)
```

```python
import functools
import math

import numpy as np
import jax
import jax.numpy as jnp
from jax import lax
from jax.experimental import pallas as pl
from jax.experimental.pallas import tpu as pltpu

F32 = jnp.float32
BF16 = jnp.bfloat16

NSA_HEADS = 8
NSA_GROUPS = 2
NSA_HPG = NSA_HEADS // NSA_GROUPS
NSA_DH = 64
CMP_BLOCK = 32
CMP_STRIDE = 16
SEL_BLOCK = 64
SEL_SHIFT = 6
SEL_TOPN = 16
WINDOW = 512
GLA_HEADS = 4
GLA_DK = 128
GLA_DV = 256
GLA_RANK = 16
GLA_TAU = 16.0
GLA_CHUNK = 64
REL_BUCKETS = 32
REL_MAX_DIST = 1024
EPS = 1e-6
NEG = -1e30
LOG2E = math.log2(math.e)

LANES = 128
VMEM_LIMIT = 56 * 1024 * 1024

ATT_TILE = 256
N_BIAS_TILES = REL_MAX_DIST // ATT_TILE + 2
N_WIN_TILES = WINDOW // ATT_TILE + 1
NSA_SUB = 128
GLA_STEP = 256

D_MODEL = 1024
C_GM = 0
C_VB = 2048
C_RB = 3072
C_QA = 4096
C_QB = 4608
C_KB = 5120
C_KV = 5632
C_GA = 6400
C_AL = 6528
N_PROJ = 6656


def _dot(a, b, precision=None):
    return lax.dot_general(a, b, (((1,), (0,)), ((), ())), precision=precision,
                           preferred_element_type=F32)


def _dot_nt(a, b, precision=None):
    return lax.dot_general(a, b, (((1,), (1,)), ((), ())), precision=precision,
                           preferred_element_type=F32)


def _dot_tn(a, b, precision=None):
    return lax.dot_general(a, b, (((0,), (0,)), ((), ())), precision=precision,
                           preferred_element_type=F32)


def _rms(x, g):
    return x * lax.rsqrt(jnp.mean(x * x, axis=-1, keepdims=True) + EPS) * g


def _params(*sem):
    return pltpu.CompilerParams(dimension_semantics=sem, vmem_limit_bytes=VMEM_LIMIT)


def _ffn_body(x_ref, g_ref, wg_ref, wu_ref, wd_ref, *rest, final):
    if final:
        fg_ref, o_ref, h_ref, acc_ref = rest
    else:
        o_ref, h_ref, acc_ref = rest
    j = pl.program_id(1)

    @pl.when(j == 0)
    def _():
        h_ref[...] = _rms(x_ref[...], g_ref[...]).astype(BF16)
        acc_ref[...] = jnp.zeros_like(acc_ref)

    h = h_ref[...]
    gate = _dot(h, wg_ref[...])
    up = _dot(h, wu_ref[...])
    act = (gate * jax.nn.sigmoid(gate) * up).astype(BF16)
    acc_ref[...] += _dot(act, wd_ref[...])

    @pl.when(j == pl.num_programs(1) - 1)
    def _():
        y = x_ref[...] + 0.5 * acc_ref[...]
        if final:
            y = _rms(y, fg_ref[...])
        o_ref[...] = y


def _ffn(x, g, wg, wu, wd, layer, final_g=None, tm=512, tf=1408):
    T, D = x.shape
    F = wg.shape[-1]
    assert T % tm == 0 and F % tf == 0
    final = final_g is not None
    in_specs = [
        pl.BlockSpec((tm, D), lambda i, j: (i, 0)),
        pl.BlockSpec((None, 1, D), lambda i, j: (layer, 0, 0)),
        pl.BlockSpec((None, D, tf), lambda i, j: (layer, 0, j)),
        pl.BlockSpec((None, D, tf), lambda i, j: (layer, 0, j)),
        pl.BlockSpec((None, tf, D), lambda i, j: (layer, j, 0)),
    ]
    args = [x, g, wg, wu, wd]
    if final:
        in_specs.append(pl.BlockSpec((1, D), lambda i, j: (0, 0)))
        args.append(final_g)
    return pl.pallas_call(
        functools.partial(_ffn_body, final=final),
        grid=(T // tm, F // tf),
        in_specs=in_specs,
        out_specs=pl.BlockSpec((tm, D), lambda i, j: (i, 0)),
        out_shape=jax.ShapeDtypeStruct((T, D), F32),
        scratch_shapes=[pltpu.VMEM((tm, D), BF16), pltpu.VMEM((tm, D), F32)],
        compiler_params=_params("parallel", "arbitrary"),
        name="ffn",
    )(*args)


def _proj_body(x_ref, g_ref, w_ref, o_ref, h_ref):
    @pl.when(pl.program_id(1) == 0)
    def _():
        h_ref[...] = _rms(x_ref[...], g_ref[...]).astype(BF16)

    o_ref[...] = _dot(h_ref[...], w_ref[...])


def _proj(x, g, w, layer, tm=1024, tn=1664):
    T, D = x.shape
    N = w.shape[-1]
    assert T % tm == 0 and N % tn == 0
    return pl.pallas_call(
        _proj_body,
        grid=(T // tm, N // tn),
        in_specs=[
            pl.BlockSpec((tm, D), lambda i, j: (i, 0)),
            pl.BlockSpec((None, 1, D), lambda i, j: (layer, 0, 0)),
            pl.BlockSpec((None, D, tn), lambda i, j: (layer, 0, j)),
        ],
        out_specs=pl.BlockSpec((tm, tn), lambda i, j: (i, j)),
        out_shape=jax.ShapeDtypeStruct((T, N), F32),
        scratch_shapes=[pltpu.VMEM((tm, D), BF16)],
        compiler_params=_params("parallel", "arbitrary"),
        name="proj",
    )(x, g, w)


def _compress_body(x_ref, pos_ref, w1_ref, w2_ref, o_ref):
    x = x_ref[...]
    half = x.shape[1]
    lo = (x + pos_ref[:, :half]).astype(BF16)
    hi = (x + pos_ref[:, half:]).astype(BF16)
    h_lo = _dot(lo, w1_ref[:half, :])
    h_hi = _dot(hi, w1_ref[half:, :])
    nch = x.shape[0]
    hid = h_lo + pltpu.roll(h_hi, nch - 1, 0)
    act = (hid * jax.nn.sigmoid(hid)).astype(BF16)
    o_ref[...] = _dot(act, w2_ref[...]).astype(o_ref.dtype)


def _compress(xc, pos, w1, w2, layer):
    _, B, G, NCH, CW = xc.shape
    HC = w1.shape[-1]
    dh = w2.shape[-1]
    return pl.pallas_call(
        _compress_body,
        grid=(2, B, G),
        in_specs=[
            pl.BlockSpec((None, None, None, NCH, CW), lambda s, b, g: (s, b, g, 0, 0)),
            pl.BlockSpec((None, None, 1, 2 * CW), lambda s, b, g: (s, layer, 0, 0)),
            pl.BlockSpec((None, None, 2 * CW, HC), lambda s, b, g: (s, layer, 0, 0)),
            pl.BlockSpec((None, None, HC, dh), lambda s, b, g: (s, layer, 0, 0)),
        ],
        out_specs=pl.BlockSpec((None, None, None, NCH, dh), lambda s, b, g: (s, b, g, 0, 0)),
        out_shape=jax.ShapeDtypeStruct((2, B, G, NCH, dh), BF16),
        compiler_params=_params("parallel", "parallel", "parallel"),
        name="compress",
    )(xc, pos, w1, w2)


def _bias_lookup(n, thr_ref, tab_ref, h):
    val = jnp.full(n.shape, tab_ref[0, h], F32)
    for k in range(1, REL_BUCKETS):
        val = jnp.where(n >= thr_ref[k], tab_ref[k, h], val)
    return val


def _toeplitz_body(thr_ref, tab_ref, o_ref, *, rows, window):
    dd = pl.program_id(0)
    h = pl.program_id(1)
    T = o_ref.shape[-1]
    for r0 in range(0, T, rows):
        a = lax.broadcasted_iota(jnp.int32, (rows, T), 0) + r0
        b = lax.broadcasted_iota(jnp.int32, (rows, T), 1)
        dist = dd * T + a - b
        val = _bias_lookup(jnp.clip(dist, 0, REL_MAX_DIST), thr_ref, tab_ref, h) * LOG2E
        keep = (dist >= 0) & (dist < WINDOW) if window else dist >= 0
        o_ref[r0:r0 + rows, :] = jnp.where(keep, val, NEG)


def _cmpbias_body(thr_ref, tab_ref, o_ref, *, rows):
    h = pl.program_id(0)
    i = pl.program_id(1)
    TQ, NC = o_ref.shape
    for r0 in range(0, TQ, rows):
        t = lax.broadcasted_iota(jnp.int32, (rows, NC), 0) + (i * TQ + r0)
        c = lax.broadcasted_iota(jnp.int32, (rows, NC), 1)
        n = jnp.clip(t - (c * CMP_STRIDE + (CMP_BLOCK - 1)), 0, REL_MAX_DIST)
        o_ref[r0:r0 + rows, :] = _bias_lookup(n, thr_ref, tab_ref, h)


def _bias_tables(thr, rel_table, S, ncp):
    T = ATT_TILE
    smem = pl.BlockSpec(memory_space=pltpu.SMEM)
    def toeplitz(n_tiles, window, name):
        return pl.pallas_call(
            functools.partial(_toeplitz_body, rows=32, window=window),
            grid=(n_tiles, NSA_HEADS),
            in_specs=[smem, smem],
            out_specs=pl.BlockSpec((None, None, T, T), lambda d, h: (d, h, 0, 0)),
            out_shape=jax.ShapeDtypeStruct((n_tiles, NSA_HEADS, T, T), F32),
            compiler_params=_params("parallel", "parallel"),
            name=name,
        )(thr, rel_table)

    toep = toeplitz(N_BIAS_TILES, False, "bias_toeplitz")
    toep_win = toeplitz(N_WIN_TILES, True, "bias_window")
    cmpb = pl.pallas_call(
        functools.partial(_cmpbias_body, rows=32),
        grid=(NSA_HEADS, S // T),
        in_specs=[smem, smem],
        out_specs=pl.BlockSpec((None, T, ncp), lambda h, i: (h, i, 0)),
        out_shape=jax.ShapeDtypeStruct((NSA_HEADS, S, ncp), F32),
        compiler_params=_params("parallel", "parallel"),
        name="bias_cmp",
    )(thr, rel_table)
    return toep, toep_win, cmpb


def _nsa_body(q_ref, gate_ref, kc_ref, vc_ref, ks_ref, vs_ref, kw_ref, vw_ref, bc_ref, bts_ref, btw_ref,
              ovl_ref, o_ref, qc_ref, qa_ref, m_ref, acc_ref, oacc_ref, sa_ref, sb_ref, alpha_ref,
              *, n_sel_blocks, n_top):
    i = pl.program_id(2)
    HPG, TQ, DH = q_ref.shape
    TK = TQ
    SB = NSA_SUB
    NCP = kc_ref.shape[0]
    t0 = i * TQ
    subs = [(h, a0) for a0 in range(0, TQ, SB) for h in range(HPG)]

    q = q_ref[...].astype(F32)
    qc_ref[...] = (q * (DH ** -0.5)).astype(BF16)
    qa_ref[:, :, :DH] = (q * (DH ** -0.5 * LOG2E)).astype(BF16)

    def gate_col(a0, col):
        return jax.nn.sigmoid(gate_ref[a0:a0 + SB, col:col + 1])

    kc = kc_ref[...]
    vc = vc_ref[...]
    c_end = lax.broadcasted_iota(jnp.int32, (SB, NCP), 1) * CMP_STRIDE + (CMP_BLOCK - 1)
    r_c = lax.broadcasted_iota(jnp.int32, (SB, NCP), 0)
    p_parts = []
    for a0 in range(0, TQ, SB):
        mc = c_end <= (t0 + a0 + r_c)
        p_heads = None
        for h in range(HPG):
            lc = _dot_nt(qc_ref[h, a0:a0 + SB, :], kc) + bc_ref[h, a0:a0 + SB, :]
            lc = jnp.where(mc, lc, NEG)
            pc = jnp.where(mc, jnp.exp(lc - jnp.max(lc, axis=-1, keepdims=True)), 0.0)
            den = jnp.sum(pc, axis=-1, keepdims=True)
            pc = pc * jnp.where(den > 0.0, 1.0 / den, 0.0)
            oacc_ref[h, a0:a0 + SB, :] = gate_col(a0, 3 * h) * _dot(pc.astype(BF16), vc)
            p_heads = pc if p_heads is None else p_heads + pc
        p_parts.append(p_heads)
    p_sum = jnp.concatenate(p_parts, axis=0)

    imp_t = _dot_nt(ovl_ref[...], p_sum, precision=lax.Precision.HIGHEST)
    s_io = lax.broadcasted_iota(jnp.int32, (DH, TQ), 0)
    jcur = (t0 + lax.broadcasted_iota(jnp.int32, (DH, TQ), 1)) >> SEL_SHIFT
    forced = (s_io == 0) | (s_io == jcur) | (s_io == jcur - 1)
    score = jnp.where(forced, 1e6, jnp.where(s_io <= jcur, imp_t, -1e6))
    sub8 = lax.broadcasted_iota(jnp.int32, (8, TQ), 0)
    cnt = [jnp.zeros((8, TQ), jnp.int32) for _ in range(DH // 8)]
    for sp in range(n_sel_blocks):
        row = score[sp:sp + 1, :]
        for g in range(DH // 8):
            blk = score[8 * g:8 * g + 8, :]
            if 8 * g > sp:
                beats = row >= blk
            elif 8 * g + 7 <= sp:
                beats = row > blk
            else:
                beats = (row > blk) | ((row == blk) & (sub8 > sp - 8 * g))
            cnt[g] = cnt[g] + jnp.where(beats, 1, 0)
    rank = jnp.concatenate(cnt, axis=0)
    drop_t = jnp.where((rank < n_top) & (s_io < n_sel_blocks), 0.0, NEG)
    drop = jnp.concatenate([jnp.zeros((DH, TQ), F32), drop_t], axis=0).T.astype(BF16)
    for h in range(HPG):
        qa_ref[h, :, DH:] = drop[:, DH:]

    def sweep(k_ref, v_ref, j_lo, logit_add, gate_off):
        m_ref[...] = jnp.full(m_ref.shape, NEG, F32)
        acc_ref[...] = jnp.zeros(acc_ref.shape, F32)

        def logits(j, s_ref):
            off = pl.multiple_of(j * TK, TK)
            qk = _dot_nt(qa_ref[...].reshape(HPG * TQ, 2 * DH), k_ref[pl.ds(off, TK), :])
            s_ref[...] = (qk.reshape(HPG, TQ, TK) + logit_add(j)).reshape(HPG * TQ, TK)

        def update(j, s_ref):
            v = v_ref[pl.ds(pl.multiple_of(j * TK, TK), TK), :]
            for r0 in range(0, HPG * TQ, SB):
                rows = slice(r0, r0 + SB)
                m_old = m_ref[rows, :]
                m_new = jnp.maximum(m_old, jnp.max(s_ref[rows, :], axis=-1, keepdims=True))
                alpha_ref[rows, :] = jnp.exp2(m_old - m_new)
                m_ref[rows, :] = m_new
            for r0 in range(0, HPG * TQ, SB):
                rows = slice(r0, r0 + SB)
                m_new = m_ref[rows, :]
                p = jnp.exp2(s_ref[rows, :] - jnp.concatenate([m_new] * (TK // LANES), axis=1))
                acc_ref[rows, :] = alpha_ref[rows, :] * acc_ref[rows, :] + _dot(p.astype(BF16), v)

        n_tiles = i + 1 - j_lo
        n_pairs = (n_tiles - 1) // 2
        logits(j_lo, sa_ref)

        def step(pair, carry):
            j = j_lo + 2 * pair
            logits(j + 1, sb_ref)
            update(j, sa_ref)
            logits(j + 2, sa_ref)
            update(j + 1, sb_ref)
            return carry

        lax.fori_loop(0, n_pairs, step, 0)
        j_tail = j_lo + 2 * n_pairs
        update(j_tail, sa_ref)

        @pl.when(j_tail < i)
        def _():
            logits(i, sb_ref)
            update(i, sb_ref)

        for h, a0 in subs:
            acc = acc_ref[h * TQ + a0:h * TQ + a0 + SB, :]
            o = (acc * (1.0 / pltpu.roll(acc, DH, 1)))[:, :DH]
            oacc_ref[h, a0:a0 + SB, :] += gate_col(a0, 3 * h + gate_off) * o

    def sel_add(j):
        return bts_ref[jnp.minimum(i - j, N_BIAS_TILES - 1)]

    sweep(ks_ref, vs_ref, 0, sel_add, 1)

    def win_add(j):
        return btw_ref[i - j]

    sweep(kw_ref, vw_ref, jnp.maximum(i - (N_WIN_TILES - 1), 0), win_add, 2)

    o_ref[...] = oacc_ref[...].astype(o_ref.dtype)


def _nsa(q, gates, kvc, ksw, vsw, bias_cmp, bias_toep, bias_win, ovl_t):
    B, G, HPG, S, DH = q.shape
    TQ = ATT_TILE
    NCP = kvc.shape[3]
    ns = S // SEL_BLOCK
    assert S % TQ == 0 and (HPG * TQ) % NSA_SUB == 0 and WINDOW % TQ == 0
    assert 2 * DH == LANES and ns <= DH and ovl_t.shape == (DH, NCP)

    def seq_spec(idx):
        return pl.BlockSpec((None, None, None, S, 2 * DH), lambda b, g, i: (idx, b, g, 0, 0))

    def cmp_spec(idx):
        return pl.BlockSpec((None, None, None, NCP, DH), lambda b, g, i: (idx, b, g, 0, 0))

    return pl.pallas_call(
        functools.partial(_nsa_body, n_sel_blocks=ns, n_top=min(SEL_TOPN, ns)),
        grid=(B, G, S // TQ),
        in_specs=[
            pl.BlockSpec((None, None, HPG, TQ, DH), lambda b, g, i: (b, g, 0, i, 0)),
            pl.BlockSpec((None, None, TQ, 3 * HPG), lambda b, g, i: (b, g, i, 0)),
            cmp_spec(0), cmp_spec(1),
            seq_spec(0), seq_spec(0), seq_spec(1), seq_spec(1),
            pl.BlockSpec((HPG, TQ, NCP), lambda b, g, i: (g, i, 0)),
            pl.BlockSpec((N_BIAS_TILES, HPG, TQ, TQ), lambda b, g, i: (0, g, 0, 0)),
            pl.BlockSpec((N_WIN_TILES, HPG, TQ, TQ), lambda b, g, i: (0, g, 0, 0)),
            pl.BlockSpec((DH, NCP), lambda b, g, i: (0, 0)),
        ],
        out_specs=pl.BlockSpec((None, None, HPG, TQ, DH), lambda b, g, i: (b, g, 0, i, 0)),
        out_shape=jax.ShapeDtypeStruct((B, G, HPG, S, DH), BF16),
        scratch_shapes=[
            pltpu.VMEM((HPG, TQ, DH), BF16),
            pltpu.VMEM((HPG, TQ, 2 * DH), BF16),
            pltpu.VMEM((HPG * TQ, LANES), F32),
            pltpu.VMEM((HPG * TQ, 2 * DH), F32),
            pltpu.VMEM((HPG, TQ, DH), F32),
            pltpu.VMEM((HPG * TQ, TQ), F32),
            pltpu.VMEM((HPG * TQ, TQ), F32),
            pltpu.VMEM((HPG * TQ, LANES), F32),
        ],
        compiler_params=_params("parallel", "parallel", "arbitrary"),
        name="nsa",
    )(q, gates, kvc, kvc, ksw, vsw, ksw, vsw, bias_cmp, bias_toep, bias_win, ovl_t)


def _gla_body(q_ref, k_ref, v_ref, r_ref, al_ref, w2_ref, ab_ref, gn_ref, o_ref, st_ref):
    @pl.when(pl.program_id(1) == 0)
    def _():
        st_ref[...] = jnp.zeros_like(st_ref)

    C = GLA_CHUNK
    n_chunks = q_ref.shape[0] // C
    hi = lax.Precision.HIGHEST
    pre = _dot(al_ref[...], w2_ref[...], precision=hi) + ab_ref[...]
    la = (jnp.minimum(pre, 0.0) - jnp.log1p(jnp.exp(-jnp.abs(pre)))) * (1.0 / GLA_TAU)
    r_io = lax.broadcasted_iota(jnp.int32, (C, C), 0)
    c_io = lax.broadcasted_iota(jnp.int32, (C, C), 1)
    causal = c_io <= r_io
    tri = causal.astype(F32)
    for c in range(n_chunks):
        rows = slice(c * C, (c + 1) * C)
        for h in range(GLA_HEADS):
            kc = slice(h * GLA_DK, (h + 1) * GLA_DK)
            vc = slice(h * GLA_DV, (h + 1) * GLA_DV)
            b = _dot(tri, la[rows, kc], precision=hi)
            b_last = b[C - 1:C, :]
            k = k_ref[rows, kc]
            q_dec = (q_ref[rows, kc] * (GLA_DK ** -0.5) * jnp.exp(b)).astype(BF16)
            k_intra = (k * jnp.exp(-b)).astype(BF16)
            k_state = (k * jnp.exp(b_last - b)).astype(BF16)
            v = v_ref[rows, vc].astype(BF16)
            st = st_ref[h]
            a = jnp.where(causal, _dot_nt(q_dec, k_intra), 0.0)
            o = _dot(a.astype(BF16), v) + _dot_nt(q_dec, st.astype(BF16))
            st_ref[h] = st * jnp.exp(b_last) + _dot_tn(v, k_state)
            o = _rms(o, gn_ref[:, vc])
            r = r_ref[rows, vc]
            o_ref[rows, vc] = (o * (r * jax.nn.sigmoid(r))).astype(o_ref.dtype)


def _gla(proj3, w2, ab, gn, layer):
    B, S, _ = proj3.shape
    RB = GLA_STEP
    HK = GLA_HEADS * GLA_DK
    HV = GLA_HEADS * GLA_DV
    assert S % RB == 0 and RB % GLA_CHUNK == 0

    def col(width, offset):
        assert offset % width == 0
        return pl.BlockSpec((None, RB, width), lambda b, s: (b, s, offset // width))

    return pl.pallas_call(
        _gla_body,
        grid=(B, S // RB),
        in_specs=[
            col(HK, C_QB), col(HK, C_KB), col(HV, C_VB), col(HV, C_RB), col(LANES, C_AL),
            pl.BlockSpec((None, LANES, HK), lambda b, s: (layer, 0, 0)),
            pl.BlockSpec((None, 1, HK), lambda b, s: (layer, 0, 0)),
            pl.BlockSpec((None, 1, HV), lambda b, s: (layer, 0, 0)),
        ],
        out_specs=pl.BlockSpec((None, RB, HV), lambda b, s: (b, s, 0)),
        out_shape=jax.ShapeDtypeStruct((B, S, HV), BF16),
        scratch_shapes=[pltpu.VMEM((GLA_HEADS, GLA_DV, GLA_DK), F32)],
        compiler_params=_params("parallel", "arbitrary"),
        name="gla",
    )(proj3, proj3, proj3, proj3, proj3, w2, ab, gn)


def _merge_body(x_ref, oa_ref, ob_ref, gm_ref, wa_ref, wb_ref, wo_ref, o_ref):
    D = x_ref.shape[1]
    gm = gm_ref[...]
    y = (jax.nn.sigmoid(gm[:, :D]) * _dot(oa_ref[...], wa_ref[...])
         + jax.nn.sigmoid(gm[:, D:]) * _dot(ob_ref[...], wb_ref[...]))
    o_ref[...] = x_ref[...] + _dot(y.astype(BF16), wo_ref[...])


def _merge(x, o_a, o_b, proj, wa, wb, wo, layer, tm=512):
    T, D = x.shape
    DA = o_a.shape[1]
    DB = o_b.shape[1]
    assert T % tm == 0 and C_GM == 0
    return pl.pallas_call(
        _merge_body,
        grid=(T // tm,),
        in_specs=[
            pl.BlockSpec((tm, D), lambda i: (i, 0)),
            pl.BlockSpec((tm, DA), lambda i: (i, 0)),
            pl.BlockSpec((tm, DB), lambda i: (i, 0)),
            pl.BlockSpec((tm, 2 * D), lambda i: (i, 0)),
            pl.BlockSpec((None, DA, D), lambda i: (layer, 0, 0)),
            pl.BlockSpec((None, DB, D), lambda i: (layer, 0, 0)),
            pl.BlockSpec((None, D, D), lambda i: (layer, 0, 0)),
        ],
        out_specs=pl.BlockSpec((tm, D), lambda i: (i, 0)),
        out_shape=jax.ShapeDtypeStruct((T, D), F32),
        compiler_params=_params("parallel"),
        name="merge",
    )(x, o_a, o_b, proj, wa, wb, wo)


def _rel_bucket(dist):
    n = jnp.maximum(dist, 0)
    exact = REL_BUCKETS // 2
    nf = jnp.maximum(n, 1).astype(jnp.float32)
    log_b = exact + (jnp.log(nf / exact) / math.log(REL_MAX_DIST / exact)
                     * (REL_BUCKETS - exact)).astype(jnp.int32)
    return jnp.where(n < exact, n, jnp.minimum(log_b, REL_BUCKETS - 1))


def _regroup_w_in(w_in):
    widths = (NSA_HEADS * NSA_DH, 6 * NSA_GROUPS * NSA_DH, 3 * NSA_HEADS, GLA_HEADS * GLA_DK,
              GLA_HEADS * GLA_DK, GLA_HEADS * GLA_DV, GLA_RANK, GLA_HEADS * GLA_DV, 2 * D_MODEL)
    offs = np.concatenate([[0], np.cumsum(widths)])
    q_a, kv_a, g_a, q_b, k_b, v_b, a_lr, r_b, g_m = (w_in[..., offs[n]:offs[n + 1]] for n in range(9))

    def pad(w):
        return jnp.pad(w, ((0, 0), (0, 0), (0, LANES - w.shape[-1])))

    out = jnp.concatenate([g_m, v_b, r_b, q_a, q_b, k_b, kv_a, pad(g_a), pad(a_lr)], axis=-1)
    assert out.shape[-1] == N_PROJ
    return out.astype(BF16)


def _overlap_t(ncp, nsp, nc, ns):
    c = np.arange(ncp)[None, :] * CMP_STRIDE
    s = np.arange(nsp)[:, None] * SEL_BLOCK
    ov = (c < s + SEL_BLOCK) & (c + CMP_BLOCK > s) & (np.arange(ncp)[None, :] < nc) & (np.arange(nsp)[:, None] < ns)
    return jnp.asarray(ov.astype(np.float32))


def kernel(x, rel_table, ffn1_norm, ffn1_w_gate, ffn1_w_up, ffn1_w_down, mix_norm, w_in, cmp_pos_k, cmp_pos_v, cmp_k_w1, cmp_k_w2, cmp_v_w1, cmp_v_w2, gla_a_w2, gla_a_b, gla_out_norm, w_branch_nsa, w_branch_gla, w_out, ffn2_norm, ffn2_w_gate, ffn2_w_up, ffn2_w_down, final_norm):
    B, S, D = x.shape
    L = w_in.shape[0]
    T = B * S
    G, HPG, DH = NSA_GROUPS, NSA_HPG, NSA_DH
    nch = S // CMP_STRIDE
    nc = (S - CMP_BLOCK) // CMP_STRIDE + 1
    ns = S // SEL_BLOCK
    assert D == D_MODEL and nc == nch - 1

    w1g, w1u, w1d = ffn1_w_gate.astype(BF16), ffn1_w_up.astype(BF16), ffn1_w_down.astype(BF16)
    w2g, w2u, w2d = ffn2_w_gate.astype(BF16), ffn2_w_up.astype(BF16), ffn2_w_down.astype(BF16)
    w_proj = _regroup_w_in(w_in)
    wa, wb, wo = w_branch_nsa.astype(BF16), w_branch_gla.astype(BF16), w_out.astype(BF16)
    cmp_pos = jnp.stack([cmp_pos_k, cmp_pos_v]).reshape(2, L, 1, CMP_BLOCK * DH)
    cmp_w1 = jnp.stack([cmp_k_w1, cmp_v_w1]).astype(BF16)
    cmp_w2 = jnp.stack([cmp_k_w2, cmp_v_w2]).astype(BF16)
    gla_w2 = jnp.pad(gla_a_w2, ((0, 0), (0, LANES - GLA_RANK), (0, 0)))
    gla_b = gla_a_b.reshape(L, 1, -1)
    gla_gn = gla_out_norm.reshape(L, 1, -1)
    n1 = ffn1_norm.reshape(L, 1, D)
    n2 = ffn2_norm.reshape(L, 1, D)
    nm = mix_norm.reshape(L, 1, D)

    buckets = _rel_bucket(jnp.arange(REL_MAX_DIST + 1, dtype=jnp.int32))
    thr = jnp.searchsorted(buckets, jnp.arange(REL_BUCKETS, dtype=jnp.int32), side="left").astype(jnp.int32)
    bias_toep, bias_win, bias_cmp = _bias_tables(thr, rel_table, S, nch)
    ovl_t = _overlap_t(nch, DH, nc, ns)
    block_onehot = (np.arange(S)[:, None] // SEL_BLOCK == np.arange(DH)[None, :]).astype(np.float32)
    k_tail = jnp.broadcast_to(jnp.asarray(np.stack([block_onehot, np.zeros_like(block_onehot)]),
                                          dtype=BF16)[:, None, None], (2, B, G, S, DH))

    xf = x.reshape(T, D)
    for l in range(L):
        xf = _ffn(xf, n1, w1g, w1u, w1d, l)
        proj = _proj(xf, nm, w_proj, l)
        proj3 = proj.reshape(B, S, N_PROJ)

        q = proj3[:, :, C_QA:C_QA + NSA_HEADS * DH].reshape(B, S, G, HPG, DH)
        q = q.transpose(0, 2, 3, 1, 4)
        kv6 = proj3[:, :, C_KV:C_KV + 6 * G * DH].reshape(B, S, 6, G, DH)
        xc = kv6[:, :, 0:2].reshape(B, nch, CMP_STRIDE, 2, G, DH)
        xc = xc.transpose(3, 0, 4, 1, 2, 5).reshape(2, B, G, nch, CMP_STRIDE * DH)
        kv = kv6[:, :, 2:6].transpose(2, 0, 3, 1, 4).astype(BF16)
        ksw = jnp.concatenate([kv[0::2], k_tail], axis=-1)
        vsw = jnp.concatenate([kv[1::2], jnp.ones_like(kv[1::2])], axis=-1)
        gates = proj3[:, :, C_GA:C_GA + 3 * NSA_HEADS].reshape(B, S, G, 3 * HPG).transpose(0, 2, 1, 3)

        kvc = _compress(xc, cmp_pos, cmp_w1, cmp_w2, l)
        o_a = _nsa(q, gates, kvc, ksw, vsw, bias_cmp, bias_toep, bias_win, ovl_t)
        o_a = o_a.transpose(0, 3, 1, 2, 4).reshape(T, NSA_HEADS * DH)

        o_b = _gla(proj3, gla_w2, gla_b, gla_gn, l).reshape(T, GLA_HEADS * GLA_DV)

        xf = _merge(xf, o_a, o_b, proj, wa, wb, wo, l)
        xf = _ffn(xf, n2, w2g, w2u, w2d, l,
                  final_g=final_norm.reshape(1, D) if l == L - 1 else None)
    return xf.reshape(B, S, D)
```

```python
import functools
import math

import numpy as np
import jax
import jax.numpy as jnp
from jax import lax
from jax.experimental import pallas as pl
from jax.experimental.pallas import tpu as pltpu

F32 = jnp.float32
BF16 = jnp.bfloat16

NSA_HEADS = 8
NSA_GROUPS = 2
NSA_HPG = NSA_HEADS // NSA_GROUPS
NSA_DH = 64
CMP_BLOCK = 32
CMP_STRIDE = 16
SEL_BLOCK = 64
SEL_SHIFT = 6
SEL_TOPN = 16
WINDOW = 512
GLA_HEADS = 4
GLA_DK = 128
GLA_DV = 256
GLA_RANK = 16
GLA_TAU = 16.0
GLA_CHUNK = 64
REL_BUCKETS = 32
REL_MAX_DIST = 1024
EPS = 1e-6
NEG = -1e30
LOG2E = math.log2(math.e)

LANES = 128
VMEM_LIMIT = 56 * 1024 * 1024

ATT_TILE = 256
N_BIAS_TILES = REL_MAX_DIST // ATT_TILE + 2
N_WIN_TILES = WINDOW // ATT_TILE + 1
NSA_SUB = 128
GLA_STEP = 256

D_MODEL = 1024
C_GM = 0
C_VB = 2048
C_RB = 3072
C_QA = 4096
C_QB = 4608
C_KB = 5120
C_KV = 5632
C_GA = 6400
C_AL = 6528
N_PROJ = 6656


def _dot(a, b, precision=None):
    return lax.dot_general(a, b, (((1,), (0,)), ((), ())), precision=precision,
                           preferred_element_type=F32)


def _dot_nt(a, b, precision=None):
    return lax.dot_general(a, b, (((1,), (1,)), ((), ())), precision=precision,
                           preferred_element_type=F32)


def _dot_tn(a, b, precision=None):
    return lax.dot_general(a, b, (((0,), (0,)), ((), ())), precision=precision,
                           preferred_element_type=F32)


def _rms(x, g):
    return x * lax.rsqrt(jnp.mean(x * x, axis=-1, keepdims=True) + EPS) * g


def _params(*sem):
    return pltpu.CompilerParams(dimension_semantics=sem, vmem_limit_bytes=VMEM_LIMIT)


def _ffn_body(x_ref, g_ref, wg_ref, wu_ref, wd_ref, *rest, final):
    if final:
        fg_ref, o_ref, h_ref, acc_ref = rest
    else:
        o_ref, h_ref, acc_ref = rest
    j = pl.program_id(1)

    @pl.when(j == 0)
    def _():
        h_ref[...] = _rms(x_ref[...], g_ref[...]).astype(BF16)
        acc_ref[...] = jnp.zeros_like(acc_ref)

    h = h_ref[...]
    gate = _dot(h, wg_ref[...])
    up = _dot(h, wu_ref[...])
    act = (gate * jax.nn.sigmoid(gate) * up).astype(BF16)
    acc_ref[...] += _dot(act, wd_ref[...])

    @pl.when(j == pl.num_programs(1) - 1)
    def _():
        y = x_ref[...] + 0.5 * acc_ref[...]
        if final:
            y = _rms(y, fg_ref[...])
        o_ref[...] = y


def _ffn(x, g, wg, wu, wd, layer, final_g=None, tm=512, tf=1408):
    T, D = x.shape
    F = wg.shape[-1]
    assert T % tm == 0 and F % tf == 0
    final = final_g is not None
    in_specs = [
        pl.BlockSpec((tm, D), lambda i, j: (i, 0)),
        pl.BlockSpec((None, 1, D), lambda i, j: (layer, 0, 0)),
        pl.BlockSpec((None, D, tf), lambda i, j: (layer, 0, j)),
        pl.BlockSpec((None, D, tf), lambda i, j: (layer, 0, j)),
        pl.BlockSpec((None, tf, D), lambda i, j: (layer, j, 0)),
    ]
    args = [x, g, wg, wu, wd]
    if final:
        in_specs.append(pl.BlockSpec((1, D), lambda i, j: (0, 0)))
        args.append(final_g)
    return pl.pallas_call(
        functools.partial(_ffn_body, final=final),
        grid=(T // tm, F // tf),
        in_specs=in_specs,
        out_specs=pl.BlockSpec((tm, D), lambda i, j: (i, 0)),
        out_shape=jax.ShapeDtypeStruct((T, D), F32),
        scratch_shapes=[pltpu.VMEM((tm, D), BF16), pltpu.VMEM((tm, D), F32)],
        compiler_params=_params("parallel", "arbitrary"),
        name="ffn",
    )(*args)


def _proj_body(x_ref, g_ref, w_ref, o_ref, h_ref):
    @pl.when(pl.program_id(1) == 0)
    def _():
        h_ref[...] = _rms(x_ref[...], g_ref[...]).astype(BF16)

    o_ref[...] = _dot(h_ref[...], w_ref[...])


def _proj(x, g, w, layer, tm=1024, tn=1664):
    T, D = x.shape
    N = w.shape[-1]
    assert T % tm == 0 and N % tn == 0
    return pl.pallas_call(
        _proj_body,
        grid=(T // tm, N // tn),
        in_specs=[
            pl.BlockSpec((tm, D), lambda i, j: (i, 0)),
            pl.BlockSpec((None, 1, D), lambda i, j: (layer, 0, 0)),
            pl.BlockSpec((None, D, tn), lambda i, j: (layer, 0, j)),
        ],
        out_specs=pl.BlockSpec((tm, tn), lambda i, j: (i, j)),
        out_shape=jax.ShapeDtypeStruct((T, N), F32),
        scratch_shapes=[pltpu.VMEM((tm, D), BF16)],
        compiler_params=_params("parallel", "arbitrary"),
        name="proj",
    )(x, g, w)


def _kvprep_body(kc_ref, vc_ref, ks_ref, vs_ref, kw_ref, vw_ref, xc_ref, ksw_ref, vsw_ref):
    TS = ks_ref.shape[0]
    DH = NSA_DH
    t = pl.program_id(1) * TS + lax.broadcasted_iota(jnp.int32, (TS, DH), 0)
    onehot = ((t >> SEL_SHIFT) == lax.broadcasted_iota(jnp.int32, (TS, DH), 1)).astype(BF16)
    ones = jnp.ones((TS, DH), BF16)
    zeros = jnp.zeros((TS, DH), BF16)
    for g in range(NSA_GROUPS):
        cols = slice(g * DH, (g + 1) * DH)
        ksw_ref[0, g] = jnp.concatenate([ks_ref[:, cols].astype(BF16), onehot], axis=1)
        ksw_ref[1, g] = jnp.concatenate([kw_ref[:, cols].astype(BF16), zeros], axis=1)
        vsw_ref[0, g] = jnp.concatenate([vs_ref[:, cols].astype(BF16), ones], axis=1)
        vsw_ref[1, g] = jnp.concatenate([vw_ref[:, cols].astype(BF16), ones], axis=1)
    for s, src in enumerate((kc_ref, vc_ref)):
        for l in range(CMP_STRIDE):
            x = src[pl.ds(l, TS // CMP_STRIDE, stride=CMP_STRIDE), :]
            for g in range(NSA_GROUPS):
                xc_ref[s, g, :, l * DH:(l + 1) * DH] = x[:, g * DH:(g + 1) * DH]


def _kvprep(proj3, ts=512):
    B, S, _ = proj3.shape
    G, DH = NSA_GROUPS, NSA_DH
    GW = G * DH
    assert S % ts == 0 and ts % (8 * CMP_STRIDE) == 0 and C_KV % GW == 0

    def col(n):
        return pl.BlockSpec((None, ts, GW), lambda b, s: (b, s, C_KV // GW + n))

    def out(width, rows):
        return pl.BlockSpec((2, None, G, rows, width), lambda b, s: (0, b, 0, s, 0))

    nch = S // CMP_STRIDE
    return pl.pallas_call(
        _kvprep_body,
        grid=(B, S // ts),
        in_specs=[col(n) for n in range(6)],
        out_specs=[out(CMP_STRIDE * DH, ts // CMP_STRIDE), out(2 * DH, ts), out(2 * DH, ts)],
        out_shape=[jax.ShapeDtypeStruct((2, B, G, nch, CMP_STRIDE * DH), F32),
                   jax.ShapeDtypeStruct((2, B, G, S, 2 * DH), BF16),
                   jax.ShapeDtypeStruct((2, B, G, S, 2 * DH), BF16)],
        compiler_params=_params("parallel", "parallel"),
        name="kvprep",
    )(*([proj3] * 6))


def _compress_body(x_ref, pos_ref, w1_ref, w2_ref, o_ref):
    x = x_ref[...]
    half = x.shape[1]
    lo = (x + pos_ref[:, :half]).astype(BF16)
    hi = (x + pos_ref[:, half:]).astype(BF16)
    h_lo = _dot(lo, w1_ref[:half, :])
    h_hi = _dot(hi, w1_ref[half:, :])
    nch = x.shape[0]
    hid = h_lo + pltpu.roll(h_hi, nch - 1, 0)
    act = (hid * jax.nn.sigmoid(hid)).astype(BF16)
    o_ref[...] = _dot(act, w2_ref[...]).astype(o_ref.dtype)


def _compress(xc, pos, w1, w2, layer):
    _, B, G, NCH, CW = xc.shape
    HC = w1.shape[-1]
    dh = w2.shape[-1]
    return pl.pallas_call(
        _compress_body,
        grid=(2, B, G),
        in_specs=[
            pl.BlockSpec((None, None, None, NCH, CW), lambda s, b, g: (s, b, g, 0, 0)),
            pl.BlockSpec((None, None, 1, 2 * CW), lambda s, b, g: (s, layer, 0, 0)),
            pl.BlockSpec((None, None, 2 * CW, HC), lambda s, b, g: (s, layer, 0, 0)),
            pl.BlockSpec((None, None, HC, dh), lambda s, b, g: (s, layer, 0, 0)),
        ],
        out_specs=pl.BlockSpec((None, None, None, NCH, dh), lambda s, b, g: (s, b, g, 0, 0)),
        out_shape=jax.ShapeDtypeStruct((2, B, G, NCH, dh), BF16),
        compiler_params=_params("parallel", "parallel", "parallel"),
        name="compress",
    )(xc, pos, w1, w2)


def _bias_lookup(n, thr_ref, tab_ref, h):
    val = jnp.full(n.shape, tab_ref[0, h], F32)
    for k in range(1, REL_BUCKETS):
        val = jnp.where(n >= thr_ref[k], tab_ref[k, h], val)
    return val


def _toeplitz_body(thr_ref, tab_ref, o_ref, *, rows, window):
    dd = pl.program_id(0)
    h = pl.program_id(1)
    T = o_ref.shape[-1]
    for r0 in range(0, T, rows):
        a = lax.broadcasted_iota(jnp.int32, (rows, T), 0) + r0
        b = lax.broadcasted_iota(jnp.int32, (rows, T), 1)
        dist = dd * T + a - b
        val = _bias_lookup(jnp.clip(dist, 0, REL_MAX_DIST), thr_ref, tab_ref, h) * LOG2E
        keep = (dist >= 0) & (dist < WINDOW) if window else dist >= 0
        o_ref[r0:r0 + rows, :] = jnp.where(keep, val, NEG)


def _cmpbias_body(thr_ref, tab_ref, o_ref, *, rows):
    h = pl.program_id(0)
    i = pl.program_id(1)
    TQ, NC = o_ref.shape
    for r0 in range(0, TQ, rows):
        t = lax.broadcasted_iota(jnp.int32, (rows, NC), 0) + (i * TQ + r0)
        c = lax.broadcasted_iota(jnp.int32, (rows, NC), 1)
        n = jnp.clip(t - (c * CMP_STRIDE + (CMP_BLOCK - 1)), 0, REL_MAX_DIST)
        o_ref[r0:r0 + rows, :] = _bias_lookup(n, thr_ref, tab_ref, h)


def _bias_tables(thr, rel_table, S, ncp):
    T = ATT_TILE
    smem = pl.BlockSpec(memory_space=pltpu.SMEM)
    def toeplitz(n_tiles, window, name):
        return pl.pallas_call(
            functools.partial(_toeplitz_body, rows=32, window=window),
            grid=(n_tiles, NSA_HEADS),
            in_specs=[smem, smem],
            out_specs=pl.BlockSpec((None, None, T, T), lambda d, h: (d, h, 0, 0)),
            out_shape=jax.ShapeDtypeStruct((n_tiles, NSA_HEADS, T, T), F32),
            compiler_params=_params("parallel", "parallel"),
            name=name,
        )(thr, rel_table)

    toep = toeplitz(N_BIAS_TILES, False, "bias_toeplitz")
    toep_win = toeplitz(N_WIN_TILES, True, "bias_window")
    cmpb = pl.pallas_call(
        functools.partial(_cmpbias_body, rows=32),
        grid=(NSA_HEADS, S // T),
        in_specs=[smem, smem],
        out_specs=pl.BlockSpec((None, T, ncp), lambda h, i: (h, i, 0)),
        out_shape=jax.ShapeDtypeStruct((NSA_HEADS, S, ncp), F32),
        compiler_params=_params("parallel", "parallel"),
        name="bias_cmp",
    )(thr, rel_table)
    return toep, toep_win, cmpb


def _nsa_body(q_ref, gate_ref, kc_ref, vc_ref, ks_ref, vs_ref, kw_ref, vw_ref, bc_ref, bts_ref, btw_ref,
              ovl_ref, o_ref, qc_ref, qa_ref, m_ref, acc_ref, oacc_ref, sa_ref, sb_ref, alpha_ref,
              *, n_sel_blocks, n_top):
    group = pl.program_id(1)
    i = pl.program_id(2)
    HPG, TQ, DH = qc_ref.shape
    TK = TQ
    SB = NSA_SUB
    NCP = kc_ref.shape[0]
    t0 = i * TQ
    subs = [(h, a0) for a0 in range(0, TQ, SB) for h in range(HPG)]

    for h in range(HPG):
        q = q_ref[:, h * DH:(h + 1) * DH]
        qc_ref[h] = (q * (DH ** -0.5)).astype(BF16)
        qa_ref[h, :, :DH] = (q * (DH ** -0.5 * LOG2E)).astype(BF16)

    def gate_col(a0, col):
        g0 = gate_ref[a0:a0 + SB, col:col + 1]
        g1 = gate_ref[a0:a0 + SB, 3 * HPG + col:3 * HPG + col + 1]
        return jax.nn.sigmoid(jnp.where(group == 0, g0, g1))

    kc = kc_ref[...]
    vc = vc_ref[...]
    c_end = lax.broadcasted_iota(jnp.int32, (SB, NCP), 1) * CMP_STRIDE + (CMP_BLOCK - 1)
    r_c = lax.broadcasted_iota(jnp.int32, (SB, NCP), 0)
    p_parts = []
    for a0 in range(0, TQ, SB):
        mc = c_end <= (t0 + a0 + r_c)
        p_heads = None
        for h in range(HPG):
            lc = _dot_nt(qc_ref[h, a0:a0 + SB, :], kc) + bc_ref[h, a0:a0 + SB, :]
            lc = jnp.where(mc, lc, NEG)
            pc = jnp.where(mc, jnp.exp(lc - jnp.max(lc, axis=-1, keepdims=True)), 0.0)
            den = jnp.sum(pc, axis=-1, keepdims=True)
            pc = pc * jnp.where(den > 0.0, 1.0 / den, 0.0)
            oacc_ref[h, a0:a0 + SB, :] = gate_col(a0, 3 * h) * _dot(pc.astype(BF16), vc)
            p_heads = pc if p_heads is None else p_heads + pc
        p_parts.append(p_heads)
    p_sum = jnp.concatenate(p_parts, axis=0)

    imp_t = _dot_nt(ovl_ref[...], p_sum, precision=lax.Precision.HIGHEST)
    s_io = lax.broadcasted_iota(jnp.int32, (DH, TQ), 0)
    jcur = (t0 + lax.broadcasted_iota(jnp.int32, (DH, TQ), 1)) >> SEL_SHIFT
    forced = (s_io == 0) | (s_io == jcur) | (s_io == jcur - 1)
    score = jnp.where(forced, 1e6, jnp.where(s_io <= jcur, imp_t, -1e6))
    sub8 = lax.broadcasted_iota(jnp.int32, (8, TQ), 0)
    cnt = [jnp.zeros((8, TQ), jnp.int32) for _ in range(DH // 8)]
    for sp in range(n_sel_blocks):
        row = score[sp:sp + 1, :]
        for g in range(DH // 8):
            blk = score[8 * g:8 * g + 8, :]
            if 8 * g > sp:
                beats = row >= blk
            elif 8 * g + 7 <= sp:
                beats = row > blk
            else:
                beats = (row > blk) | ((row == blk) & (sub8 > sp - 8 * g))
            cnt[g] = cnt[g] + jnp.where(beats, 1, 0)
    rank = jnp.concatenate(cnt, axis=0)
    drop_t = jnp.where((rank < n_top) & (s_io < n_sel_blocks), 0.0, NEG)
    drop = jnp.concatenate([jnp.zeros((DH, TQ), F32), drop_t], axis=0).T.astype(BF16)
    for h in range(HPG):
        qa_ref[h, :, DH:] = drop[:, DH:]

    def sweep(k_ref, v_ref, j_lo, logit_add, gate_off):
        m_ref[...] = jnp.full(m_ref.shape, NEG, F32)
        acc_ref[...] = jnp.zeros(acc_ref.shape, F32)

        def logits(j, s_ref):
            off = pl.multiple_of(j * TK, TK)
            qk = _dot_nt(qa_ref[...].reshape(HPG * TQ, 2 * DH), k_ref[pl.ds(off, TK), :])
            s_ref[...] = (qk.reshape(HPG, TQ, TK) + logit_add(j)).reshape(HPG * TQ, TK)

        def update(j, s_ref):
            v = v_ref[pl.ds(pl.multiple_of(j * TK, TK), TK), :]
            for r0 in range(0, HPG * TQ, SB):
                rows = slice(r0, r0 + SB)
                m_old = m_ref[rows, :]
                m_new = jnp.maximum(m_old, jnp.max(s_ref[rows, :], axis=-1, keepdims=True))
                alpha_ref[rows, :] = jnp.exp2(m_old - m_new)
                m_ref[rows, :] = m_new
            for r0 in range(0, HPG * TQ, SB):
                rows = slice(r0, r0 + SB)
                m_new = m_ref[rows, :]
                p = jnp.exp2(s_ref[rows, :] - jnp.concatenate([m_new] * (TK // LANES), axis=1))
                acc_ref[rows, :] = alpha_ref[rows, :] * acc_ref[rows, :] + _dot(p.astype(BF16), v)

        n_tiles = i + 1 - j_lo
        n_pairs = (n_tiles - 1) // 2
        logits(j_lo, sa_ref)

        def step(pair, carry):
            j = j_lo + 2 * pair
            logits(j + 1, sb_ref)
            update(j, sa_ref)
            logits(j + 2, sa_ref)
            update(j + 1, sb_ref)
            return carry

        lax.fori_loop(0, n_pairs, step, 0)
        j_tail = j_lo + 2 * n_pairs
        update(j_tail, sa_ref)

        @pl.when(j_tail < i)
        def _():
            logits(i, sb_ref)
            update(i, sb_ref)

        for h, a0 in subs:
            acc = acc_ref[h * TQ + a0:h * TQ + a0 + SB, :]
            o = (acc * (1.0 / pltpu.roll(acc, DH, 1)))[:, :DH]
            oacc_ref[h, a0:a0 + SB, :] += gate_col(a0, 3 * h + gate_off) * o

    def sel_add(j):
        return bts_ref[jnp.minimum(i - j, N_BIAS_TILES - 1)]

    sweep(ks_ref, vs_ref, 0, sel_add, 1)

    def win_add(j):
        return btw_ref[i - j]

    sweep(kw_ref, vw_ref, jnp.maximum(i - (N_WIN_TILES - 1), 0), win_add, 2)

    for h in range(HPG):
        o_ref[:, h * DH:(h + 1) * DH] = oacc_ref[h].astype(o_ref.dtype)


def _nsa(proj3, kvc, ksw, vsw, bias_cmp, bias_toep, bias_win, ovl_t):
    B, S, _ = proj3.shape
    G, HPG, DH = NSA_GROUPS, NSA_HPG, NSA_DH
    GW = HPG * DH
    TQ = ATT_TILE
    NCP = kvc.shape[3]
    ns = S // SEL_BLOCK
    assert S % TQ == 0 and (HPG * TQ) % NSA_SUB == 0 and WINDOW % TQ == 0
    assert 2 * DH == LANES and ns <= DH and ovl_t.shape == (DH, NCP)
    assert C_QA % GW == 0 and C_GA % LANES == 0 and 3 * NSA_HEADS <= LANES

    def seq_spec(idx):
        return pl.BlockSpec((None, None, None, S, 2 * DH), lambda b, g, i: (idx, b, g, 0, 0))

    def cmp_spec(idx):
        return pl.BlockSpec((None, None, None, NCP, DH), lambda b, g, i: (idx, b, g, 0, 0))

    return pl.pallas_call(
        functools.partial(_nsa_body, n_sel_blocks=ns, n_top=min(SEL_TOPN, ns)),
        grid=(B, G, S // TQ),
        in_specs=[
            pl.BlockSpec((None, TQ, GW), lambda b, g, i: (b, i, C_QA // GW + g)),
            pl.BlockSpec((None, TQ, LANES), lambda b, g, i: (b, i, C_GA // LANES)),
            cmp_spec(0), cmp_spec(1),
            seq_spec(0), seq_spec(0), seq_spec(1), seq_spec(1),
            pl.BlockSpec((HPG, TQ, NCP), lambda b, g, i: (g, i, 0)),
            pl.BlockSpec((N_BIAS_TILES, HPG, TQ, TQ), lambda b, g, i: (0, g, 0, 0)),
            pl.BlockSpec((N_WIN_TILES, HPG, TQ, TQ), lambda b, g, i: (0, g, 0, 0)),
            pl.BlockSpec((DH, NCP), lambda b, g, i: (0, 0)),
        ],
        out_specs=pl.BlockSpec((None, TQ, GW), lambda b, g, i: (b, i, g)),
        out_shape=jax.ShapeDtypeStruct((B, S, G * GW), BF16),
        scratch_shapes=[
            pltpu.VMEM((HPG, TQ, DH), BF16),
            pltpu.VMEM((HPG, TQ, 2 * DH), BF16),
            pltpu.VMEM((HPG * TQ, LANES), F32),
            pltpu.VMEM((HPG * TQ, 2 * DH), F32),
            pltpu.VMEM((HPG, TQ, DH), F32),
            pltpu.VMEM((HPG * TQ, TQ), F32),
            pltpu.VMEM((HPG * TQ, TQ), F32),
            pltpu.VMEM((HPG * TQ, LANES), F32),
        ],
        compiler_params=_params("parallel", "parallel", "arbitrary"),
        name="nsa",
    )(proj3, proj3, kvc, kvc, ksw, vsw, ksw, vsw, bias_cmp, bias_toep, bias_win, ovl_t)


def _gla_body(q_ref, k_ref, v_ref, r_ref, al_ref, w2_ref, ab_ref, gn_ref, o_ref, st_ref, oi_ref):
    @pl.when(pl.program_id(1) == 0)
    def _():
        st_ref[...] = jnp.zeros_like(st_ref)

    C = GLA_CHUNK
    RB = q_ref.shape[0]
    n_chunks = RB // C
    pre = _dot(al_ref[...].astype(BF16), w2_ref[...]) + ab_ref[...]
    la = (jnp.minimum(pre, 0.0) - jnp.log1p(jnp.exp(-jnp.abs(pre)))) * (1.0 / GLA_TAU)

    la_hi = la.astype(BF16)
    rest = la - la_hi.astype(F32)
    la_mid = rest.astype(BF16)
    la_lo = (rest - la_mid.astype(F32)).astype(BF16)
    r_io = lax.broadcasted_iota(jnp.int32, (C, 3 * C), 0)
    c_io = lax.broadcasted_iota(jnp.int32, (C, 3 * C), 1)
    tri3 = ((c_io & (C - 1)) <= r_io).astype(BF16)
    b_parts, bl_parts = [], []
    for c in range(n_chunks):
        rows = slice(c * C, (c + 1) * C)
        b_c = _dot(tri3, jnp.concatenate([la_hi[rows], la_mid[rows], la_lo[rows]], axis=0))
        b_parts.append(b_c)
        bl_parts.append(jnp.broadcast_to(b_c[C - 1:C, :], b_c.shape))
    b = jnp.concatenate(b_parts, axis=0)
    b_last = jnp.concatenate(bl_parts, axis=0)

    k = k_ref[...]
    q_dec = (q_ref[...] * (jnp.exp(b) * (GLA_DK ** -0.5))).astype(BF16)
    k_intra = (k * jnp.exp(-b)).astype(BF16)
    k_state = (k * jnp.exp(b_last - b)).astype(BF16)

    rr = lax.broadcasted_iota(jnp.int32, (RB, RB), 0)
    cc = lax.broadcasted_iota(jnp.int32, (RB, RB), 1)
    same_chunk_causal = (cc <= rr) & ((rr & -C) == (cc & -C))
    for h in range(GLA_HEADS):
        kc = slice(h * GLA_DK, (h + 1) * GLA_DK)
        vc = slice(h * GLA_DV, (h + 1) * GLA_DV)
        v = v_ref[:, vc].astype(BF16)
        a = jnp.where(same_chunk_causal, _dot_nt(q_dec[:, kc], k_intra[:, kc]), 0.0)
        o_intra = _dot(a.astype(BF16), v)
        st = st_ref[h]
        for c in range(n_chunks):
            rows = slice(c * C, (c + 1) * C)
            oi_ref[rows, :] = _dot(q_dec[rows, kc], st.astype(BF16))
            decay = jnp.exp(b[(c + 1) * C - 8:(c + 1) * C, kc].T[:, 7:8])
            st = st * decay + _dot_tn(k_state[rows, kc], v[rows])
        st_ref[h] = st
        o = _rms(o_intra + oi_ref[...], gn_ref[:, vc])
        r = r_ref[:, vc]
        o_ref[:, vc] = (o * (r * jax.nn.sigmoid(r))).astype(o_ref.dtype)


def _gla(proj3, w2, ab, gn, layer):
    B, S, _ = proj3.shape
    RB = GLA_STEP
    HK = GLA_HEADS * GLA_DK
    HV = GLA_HEADS * GLA_DV
    assert S % RB == 0 and RB % GLA_CHUNK == 0

    def col(width, offset):
        assert offset % width == 0
        return pl.BlockSpec((None, RB, width), lambda b, s: (b, s, offset // width))

    return pl.pallas_call(
        _gla_body,
        grid=(B, S // RB),
        in_specs=[
            col(HK, C_QB), col(HK, C_KB), col(HV, C_VB), col(HV, C_RB), col(LANES, C_AL),
            pl.BlockSpec((None, LANES, HK), lambda b, s: (layer, 0, 0)),
            pl.BlockSpec((None, 1, HK), lambda b, s: (layer, 0, 0)),
            pl.BlockSpec((None, 1, HV), lambda b, s: (layer, 0, 0)),
        ],
        out_specs=pl.BlockSpec((None, RB, HV), lambda b, s: (b, s, 0)),
        out_shape=jax.ShapeDtypeStruct((B, S, HV), BF16),
        scratch_shapes=[pltpu.VMEM((GLA_HEADS, GLA_DK, GLA_DV), F32),
                        pltpu.VMEM((RB, GLA_DV), F32)],
        compiler_params=_params("parallel", "arbitrary"),
        name="gla",
    )(proj3, proj3, proj3, proj3, proj3, w2, ab, gn)


def _merge_body(x_ref, oa_ref, ob_ref, gm_ref, wa_ref, wb_ref, wo_ref, o_ref):
    D = x_ref.shape[1]
    gm = gm_ref[...]
    y = (jax.nn.sigmoid(gm[:, :D]) * _dot(oa_ref[...], wa_ref[...])
         + jax.nn.sigmoid(gm[:, D:]) * _dot(ob_ref[...], wb_ref[...]))
    o_ref[...] = x_ref[...] + _dot(y.astype(BF16), wo_ref[...])


def _merge(x, o_a, o_b, proj, wa, wb, wo, layer, tm=512):
    T, D = x.shape
    DA = o_a.shape[1]
    DB = o_b.shape[1]
    assert T % tm == 0 and C_GM == 0
    return pl.pallas_call(
        _merge_body,
        grid=(T // tm,),
        in_specs=[
            pl.BlockSpec((tm, D), lambda i: (i, 0)),
            pl.BlockSpec((tm, DA), lambda i: (i, 0)),
            pl.BlockSpec((tm, DB), lambda i: (i, 0)),
            pl.BlockSpec((tm, 2 * D), lambda i: (i, 0)),
            pl.BlockSpec((None, DA, D), lambda i: (layer, 0, 0)),
            pl.BlockSpec((None, DB, D), lambda i: (layer, 0, 0)),
            pl.BlockSpec((None, D, D), lambda i: (layer, 0, 0)),
        ],
        out_specs=pl.BlockSpec((tm, D), lambda i: (i, 0)),
        out_shape=jax.ShapeDtypeStruct((T, D), F32),
        compiler_params=_params("parallel"),
        name="merge",
    )(x, o_a, o_b, proj, wa, wb, wo)


def _rel_bucket(dist):
    n = jnp.maximum(dist, 0)
    exact = REL_BUCKETS // 2
    nf = jnp.maximum(n, 1).astype(jnp.float32)
    log_b = exact + (jnp.log(nf / exact) / math.log(REL_MAX_DIST / exact)
                     * (REL_BUCKETS - exact)).astype(jnp.int32)
    return jnp.where(n < exact, n, jnp.minimum(log_b, REL_BUCKETS - 1))


def _regroup_w_in(w_in):
    widths = (NSA_HEADS * NSA_DH, 6 * NSA_GROUPS * NSA_DH, 3 * NSA_HEADS, GLA_HEADS * GLA_DK,
              GLA_HEADS * GLA_DK, GLA_HEADS * GLA_DV, GLA_RANK, GLA_HEADS * GLA_DV, 2 * D_MODEL)
    offs = np.concatenate([[0], np.cumsum(widths)])
    q_a, kv_a, g_a, q_b, k_b, v_b, a_lr, r_b, g_m = (w_in[..., offs[n]:offs[n + 1]] for n in range(9))

    def pad(w):
        return jnp.pad(w, ((0, 0), (0, 0), (0, LANES - w.shape[-1])))

    out = jnp.concatenate([g_m, v_b, r_b, q_a, q_b, k_b, kv_a, pad(g_a), pad(a_lr)], axis=-1)
    assert out.shape[-1] == N_PROJ
    return out.astype(BF16)


def _overlap_t(ncp, nsp, nc, ns):
    c = np.arange(ncp)[None, :] * CMP_STRIDE
    s = np.arange(nsp)[:, None] * SEL_BLOCK
    ov = (c < s + SEL_BLOCK) & (c + CMP_BLOCK > s) & (np.arange(ncp)[None, :] < nc) & (np.arange(nsp)[:, None] < ns)
    return jnp.asarray(ov.astype(np.float32))


def kernel(x, rel_table, ffn1_norm, ffn1_w_gate, ffn1_w_up, ffn1_w_down, mix_norm, w_in, cmp_pos_k, cmp_pos_v, cmp_k_w1, cmp_k_w2, cmp_v_w1, cmp_v_w2, gla_a_w2, gla_a_b, gla_out_norm, w_branch_nsa, w_branch_gla, w_out, ffn2_norm, ffn2_w_gate, ffn2_w_up, ffn2_w_down, final_norm):
    B, S, D = x.shape
    L = w_in.shape[0]
    T = B * S
    G, HPG, DH = NSA_GROUPS, NSA_HPG, NSA_DH
    nch = S // CMP_STRIDE
    nc = (S - CMP_BLOCK) // CMP_STRIDE + 1
    ns = S // SEL_BLOCK
    assert D == D_MODEL and nc == nch - 1

    w1g, w1u, w1d = ffn1_w_gate.astype(BF16), ffn1_w_up.astype(BF16), ffn1_w_down.astype(BF16)
    w2g, w2u, w2d = ffn2_w_gate.astype(BF16), ffn2_w_up.astype(BF16), ffn2_w_down.astype(BF16)
    w_proj = _regroup_w_in(w_in)
    wa, wb, wo = w_branch_nsa.astype(BF16), w_branch_gla.astype(BF16), w_out.astype(BF16)
    cmp_pos = jnp.stack([cmp_pos_k, cmp_pos_v]).reshape(2, L, 1, CMP_BLOCK * DH)
    cmp_w1 = jnp.stack([cmp_k_w1, cmp_v_w1]).astype(BF16)
    cmp_w2 = jnp.stack([cmp_k_w2, cmp_v_w2]).astype(BF16)
    gla_w2 = jnp.pad(gla_a_w2, ((0, 0), (0, LANES - GLA_RANK), (0, 0))).astype(BF16)
    gla_b = gla_a_b.reshape(L, 1, -1)
    gla_gn = gla_out_norm.reshape(L, 1, -1)
    n1 = ffn1_norm.reshape(L, 1, D)
    n2 = ffn2_norm.reshape(L, 1, D)
    nm = mix_norm.reshape(L, 1, D)

    buckets = _rel_bucket(jnp.arange(REL_MAX_DIST + 1, dtype=jnp.int32))
    thr = jnp.searchsorted(buckets, jnp.arange(REL_BUCKETS, dtype=jnp.int32), side="left").astype(jnp.int32)
    bias_toep, bias_win, bias_cmp = _bias_tables(thr, rel_table, S, nch)
    ovl_t = _overlap_t(nch, DH, nc, ns)

    xf = x.reshape(T, D)
    for l in range(L):
        xf = _ffn(xf, n1, w1g, w1u, w1d, l)
        proj = _proj(xf, nm, w_proj, l)
        proj3 = proj.reshape(B, S, N_PROJ)

        xc, ksw, vsw = _kvprep(proj3)
        kvc = _compress(xc, cmp_pos, cmp_w1, cmp_w2, l)
        o_a = _nsa(proj3, kvc, ksw, vsw, bias_cmp, bias_toep, bias_win, ovl_t)
        o_a = o_a.reshape(T, NSA_HEADS * DH)

        o_b = _gla(proj3, gla_w2, gla_b, gla_gn, l).reshape(T, GLA_HEADS * GLA_DV)

        xf = _merge(xf, o_a, o_b, proj, wa, wb, wo, l)
        xf = _ffn(xf, n2, w2g, w2u, w2d, l,
                  final_g=final_norm.reshape(1, D) if l == L - 1 else None)
    return xf.reshape(B, S, D)
```

```python
import functools
import math

import numpy as np
import jax
import jax.numpy as jnp
from jax import lax
from jax.experimental import pallas as pl
from jax.experimental.pallas import tpu as pltpu

F32 = jnp.float32
BF16 = jnp.bfloat16

NSA_HEADS = 8
NSA_GROUPS = 2
NSA_HPG = NSA_HEADS // NSA_GROUPS
NSA_DH = 64
CMP_BLOCK = 32
CMP_STRIDE = 16
SEL_BLOCK = 64
SEL_SHIFT = 6
SEL_TOPN = 16
WINDOW = 512
GLA_HEADS = 4
GLA_DK = 128
GLA_DV = 256
GLA_RANK = 16
GLA_TAU = 16.0
GLA_CHUNK = 64
REL_BUCKETS = 32
REL_MAX_DIST = 1024
EPS = 1e-6
NEG = -1e30
LOG2E = math.log2(math.e)

LANES = 128
VMEM_LIMIT = 56 * 1024 * 1024

ATT_TILE = 256
N_BIAS_TILES = REL_MAX_DIST // ATT_TILE + 2
N_WIN_TILES = WINDOW // ATT_TILE + 1
NSA_SUB = 128
GLA_STEP = 256

D_MODEL = 1024
C_GM = 0
C_VB = 2048
C_RB = 3072
C_QA = 4096
C_QB = 4608
C_KB = 5120
C_KV = 5632
C_GA = 6400
C_AL = 6528
N_PROJ = 6656


def _dot(a, b, precision=None):
    return lax.dot_general(a, b, (((1,), (0,)), ((), ())), precision=precision,
                           preferred_element_type=F32)


def _dot_nt(a, b, precision=None):
    return lax.dot_general(a, b, (((1,), (1,)), ((), ())), precision=precision,
                           preferred_element_type=F32)


def _dot_tn(a, b, precision=None):
    return lax.dot_general(a, b, (((0,), (0,)), ((), ())), precision=precision,
                           preferred_element_type=F32)


def _rms(x, g):
    return x * lax.rsqrt(jnp.mean(x * x, axis=-1, keepdims=True) + EPS) * g


def _params(*sem):
    return pltpu.CompilerParams(dimension_semantics=sem, vmem_limit_bytes=VMEM_LIMIT)


def _ffn_body(x_ref, g_ref, wg_ref, wu_ref, wd_ref, *rest, final):
    if final:
        fg_ref, o_ref, h_ref, acc_ref = rest
    else:
        o_ref, h_ref, acc_ref = rest
    j = pl.program_id(1)

    @pl.when(j == 0)
    def _():
        h_ref[...] = _rms(x_ref[...], g_ref[...]).astype(BF16)
        acc_ref[...] = jnp.zeros_like(acc_ref)

    h = h_ref[...]
    gate = _dot(h, wg_ref[...])
    up = _dot(h, wu_ref[...])
    act = (gate * jax.nn.sigmoid(gate) * up).astype(BF16)
    acc_ref[...] += _dot(act, wd_ref[...])

    @pl.when(j == pl.num_programs(1) - 1)
    def _():
        y = x_ref[...] + 0.5 * acc_ref[...]
        if final:
            y = _rms(y, fg_ref[...])
        o_ref[...] = y


def _ffn(x, g, wg, wu, wd, layer, final_g=None, tm=512, tf=1408):
    T, D = x.shape
    F = wg.shape[-1]
    assert T % tm == 0 and F % tf == 0
    final = final_g is not None
    in_specs = [
        pl.BlockSpec((tm, D), lambda i, j: (i, 0)),
        pl.BlockSpec((None, 1, D), lambda i, j: (layer, 0, 0)),
        pl.BlockSpec((None, D, tf), lambda i, j: (layer, 0, j)),
        pl.BlockSpec((None, D, tf), lambda i, j: (layer, 0, j)),
        pl.BlockSpec((None, tf, D), lambda i, j: (layer, j, 0)),
    ]
    args = [x, g, wg, wu, wd]
    if final:
        in_specs.append(pl.BlockSpec((1, D), lambda i, j: (0, 0)))
        args.append(final_g)
    return pl.pallas_call(
        functools.partial(_ffn_body, final=final),
        grid=(T // tm, F // tf),
        in_specs=in_specs,
        out_specs=pl.BlockSpec((tm, D), lambda i, j: (i, 0)),
        out_shape=jax.ShapeDtypeStruct((T, D), F32),
        scratch_shapes=[pltpu.VMEM((tm, D), BF16), pltpu.VMEM((tm, D), F32)],
        compiler_params=_params("parallel", "arbitrary"),
        name="ffn",
    )(*args)


def _proj_body(x_ref, g_ref, w_ref, o_ref, h_ref):
    @pl.when(pl.program_id(1) == 0)
    def _():
        h_ref[...] = _rms(x_ref[...], g_ref[...]).astype(BF16)

    o_ref[...] = _dot(h_ref[...], w_ref[...])


def _proj(x, g, w, layer, tm=1024, tn=1664):
    T, D = x.shape
    N = w.shape[-1]
    assert T % tm == 0 and N % tn == 0
    return pl.pallas_call(
        _proj_body,
        grid=(T // tm, N // tn),
        in_specs=[
            pl.BlockSpec((tm, D), lambda i, j: (i, 0)),
            pl.BlockSpec((None, 1, D), lambda i, j: (layer, 0, 0)),
            pl.BlockSpec((None, D, tn), lambda i, j: (layer, 0, j)),
        ],
        out_specs=pl.BlockSpec((tm, tn), lambda i, j: (i, j)),
        out_shape=jax.ShapeDtypeStruct((T, N), F32),
        scratch_shapes=[pltpu.VMEM((tm, D), BF16)],
        compiler_params=_params("parallel", "arbitrary"),
        name="proj",
    )(x, g, w)


def _kvprep_body(kc_ref, vc_ref, ks_ref, vs_ref, kw_ref, vw_ref, xc_ref, ksw_ref, vsw_ref):
    TS = ks_ref.shape[0]
    DH = NSA_DH
    t = pl.program_id(1) * TS + lax.broadcasted_iota(jnp.int32, (TS, DH), 0)
    onehot = ((t >> SEL_SHIFT) == lax.broadcasted_iota(jnp.int32, (TS, DH), 1)).astype(BF16)
    ones = jnp.ones((TS, DH), BF16)
    zeros = jnp.zeros((TS, DH), BF16)
    for g in range(NSA_GROUPS):
        cols = slice(g * DH, (g + 1) * DH)
        ksw_ref[0, g] = jnp.concatenate([ks_ref[:, cols].astype(BF16), onehot], axis=1)
        ksw_ref[1, g] = jnp.concatenate([kw_ref[:, cols].astype(BF16), zeros], axis=1)
        vsw_ref[0, g] = jnp.concatenate([vs_ref[:, cols].astype(BF16), ones], axis=1)
        vsw_ref[1, g] = jnp.concatenate([vw_ref[:, cols].astype(BF16), ones], axis=1)
    for s, src in enumerate((kc_ref, vc_ref)):
        for l in range(CMP_STRIDE):
            x = src[pl.ds(l, TS // CMP_STRIDE, stride=CMP_STRIDE), :]
            for g in range(NSA_GROUPS):
                xc_ref[s, g, :, l * DH:(l + 1) * DH] = x[:, g * DH:(g + 1) * DH]


def _kvprep(proj3, ts=512):
    B, S, _ = proj3.shape
    G, DH = NSA_GROUPS, NSA_DH
    GW = G * DH
    assert S % ts == 0 and ts % (8 * CMP_STRIDE) == 0 and C_KV % GW == 0

    def col(n):
        return pl.BlockSpec((None, ts, GW), lambda b, s: (b, s, C_KV // GW + n))

    def out(width, rows):
        return pl.BlockSpec((2, None, G, rows, width), lambda b, s: (0, b, 0, s, 0))

    nch = S // CMP_STRIDE
    return pl.pallas_call(
        _kvprep_body,
        grid=(B, S // ts),
        in_specs=[col(n) for n in range(6)],
        out_specs=[out(CMP_STRIDE * DH, ts // CMP_STRIDE), out(2 * DH, ts), out(2 * DH, ts)],
        out_shape=[jax.ShapeDtypeStruct((2, B, G, nch, CMP_STRIDE * DH), F32),
                   jax.ShapeDtypeStruct((2, B, G, S, 2 * DH), BF16),
                   jax.ShapeDtypeStruct((2, B, G, S, 2 * DH), BF16)],
        compiler_params=_params("parallel", "parallel"),
        name="kvprep",
    )(*([proj3] * 6))


def _compress_body(x_ref, pos_ref, w1_ref, w2_ref, o_ref):
    x = x_ref[...]
    half = x.shape[1]
    lo = (x + pos_ref[:, :half]).astype(BF16)
    hi = (x + pos_ref[:, half:]).astype(BF16)
    h_lo = _dot(lo, w1_ref[:half, :])
    h_hi = _dot(hi, w1_ref[half:, :])
    nch = x.shape[0]
    hid = h_lo + pltpu.roll(h_hi, nch - 1, 0)
    act = (hid * jax.nn.sigmoid(hid)).astype(BF16)
    o_ref[...] = _dot(act, w2_ref[...]).astype(o_ref.dtype)


def _compress(xc, pos, w1, w2, layer):
    _, B, G, NCH, CW = xc.shape
    HC = w1.shape[-1]
    dh = w2.shape[-1]
    return pl.pallas_call(
        _compress_body,
        grid=(2, B, G),
        in_specs=[
            pl.BlockSpec((None, None, None, NCH, CW), lambda s, b, g: (s, b, g, 0, 0)),
            pl.BlockSpec((None, None, 1, 2 * CW), lambda s, b, g: (s, layer, 0, 0)),
            pl.BlockSpec((None, None, 2 * CW, HC), lambda s, b, g: (s, layer, 0, 0)),
            pl.BlockSpec((None, None, HC, dh), lambda s, b, g: (s, layer, 0, 0)),
        ],
        out_specs=pl.BlockSpec((None, None, None, NCH, dh), lambda s, b, g: (s, b, g, 0, 0)),
        out_shape=jax.ShapeDtypeStruct((2, B, G, NCH, dh), BF16),
        compiler_params=_params("parallel", "parallel", "parallel"),
        name="compress",
    )(xc, pos, w1, w2)


def _bias_lookup(n, thr_ref, tab_ref, h):
    val = jnp.full(n.shape, tab_ref[0, h], F32)
    for k in range(1, REL_BUCKETS):
        val = jnp.where(n >= thr_ref[k], tab_ref[k, h], val)
    return val


def _toeplitz_body(thr_ref, tab_ref, o_ref, *, rows, window):
    dd = pl.program_id(0)
    h = pl.program_id(1)
    T = o_ref.shape[-1]
    for r0 in range(0, T, rows):
        a = lax.broadcasted_iota(jnp.int32, (rows, T), 0) + r0
        b = lax.broadcasted_iota(jnp.int32, (rows, T), 1)
        dist = dd * T + a - b
        val = _bias_lookup(jnp.clip(dist, 0, REL_MAX_DIST), thr_ref, tab_ref, h) * LOG2E
        keep = (dist >= 0) & (dist < WINDOW) if window else dist >= 0
        o_ref[r0:r0 + rows, :] = jnp.where(keep, val, NEG)


def _cmpbias_body(thr_ref, tab_ref, o_ref, *, rows):
    h = pl.program_id(0)
    i = pl.program_id(1)
    TQ, NC = o_ref.shape
    for r0 in range(0, TQ, rows):
        t = lax.broadcasted_iota(jnp.int32, (rows, NC), 0) + (i * TQ + r0)
        c = lax.broadcasted_iota(jnp.int32, (rows, NC), 1)
        n = jnp.clip(t - (c * CMP_STRIDE + (CMP_BLOCK - 1)), 0, REL_MAX_DIST)
        o_ref[r0:r0 + rows, :] = _bias_lookup(n, thr_ref, tab_ref, h)


def _bias_tables(thr, rel_table, S, ncp):
    T = ATT_TILE
    smem = pl.BlockSpec(memory_space=pltpu.SMEM)
    def toeplitz(n_tiles, window, name):
        return pl.pallas_call(
            functools.partial(_toeplitz_body, rows=32, window=window),
            grid=(n_tiles, NSA_HEADS),
            in_specs=[smem, smem],
            out_specs=pl.BlockSpec((None, None, T, T), lambda d, h: (d, h, 0, 0)),
            out_shape=jax.ShapeDtypeStruct((n_tiles, NSA_HEADS, T, T), F32),
            compiler_params=_params("parallel", "parallel"),
            name=name,
        )(thr, rel_table)

    toep = toeplitz(N_BIAS_TILES, False, "bias_toeplitz")
    toep_win = toeplitz(N_WIN_TILES + 1, True, "bias_window")
    cmpb = pl.pallas_call(
        functools.partial(_cmpbias_body, rows=32),
        grid=(NSA_HEADS, S // T),
        in_specs=[smem, smem],
        out_specs=pl.BlockSpec((None, T, ncp), lambda h, i: (h, i, 0)),
        out_shape=jax.ShapeDtypeStruct((NSA_HEADS, S, ncp), F32),
        compiler_params=_params("parallel", "parallel"),
        name="bias_cmp",
    )(thr, rel_table)
    return toep, toep_win, cmpb


def _nsa_body(q_ref, gate_ref, kc_ref, vc_ref, ks_ref, vs_ref, kw_ref, vw_ref, bc_ref, bts_ref, btw_ref,
              ovl_ref, o_ref, qc_ref, qa_ref, m_ref, acc_ref, oacc_ref, sa_ref, sb_ref, sc_ref, alpha_ref,
              gs_ref, lc_ref, pcb_ref,
              *, n_sel_blocks, n_top):
    group = pl.program_id(1)
    i = pl.program_id(2)
    HPG, TQ, DH = qc_ref.shape
    TK = TQ
    SB = NSA_SUB
    NCP = kc_ref.shape[0]
    t0 = i * TQ
    R = HPG * TQ
    subs = [(h, a0) for a0 in range(0, TQ, SB) for h in range(HPG)]

    for h in range(HPG):
        q = q_ref[:, h * DH:(h + 1) * DH]
        qc_ref[h] = (q * (DH ** -0.5)).astype(BF16)
        qa_ref[h, :, :DH] = (q * (DH ** -0.5 * LOG2E)).astype(BF16)
        qa_ref[h, :, DH:] = jnp.zeros((TQ, DH), BF16)

    gates = jax.nn.sigmoid(gate_ref[...])
    gs_ref[...] = jnp.where(group == 0, gates, pltpu.roll(gates, LANES - 3 * HPG, 1))

    def gate_col(a0, col):
        return gs_ref[a0:a0 + SB, col:col + 1]

    lc_all = _dot_nt(qc_ref[...].reshape(R, DH), kc_ref[...])
    lc_ref[...] = (lc_all.reshape(HPG, TQ, NCP) + bc_ref[...]).reshape(R, NCP)
    c_end = lax.broadcasted_iota(jnp.int32, (SB, NCP), 1) * CMP_STRIDE + (CMP_BLOCK - 1)
    r_c = lax.broadcasted_iota(jnp.int32, (SB, NCP), 0)
    p_parts = []
    for a0 in range(0, TQ, SB):
        mc = c_end <= (t0 + a0 + r_c)
        p_heads = None
        for h in range(HPG):
            rows = slice(h * TQ + a0, h * TQ + a0 + SB)
            lc = jnp.where(mc, lc_ref[rows, :], NEG)
            pc = jnp.where(mc, jnp.exp(lc - jnp.max(lc, axis=-1, keepdims=True)), 0.0)
            den = jnp.sum(pc, axis=-1, keepdims=True)
            pc = pc * jnp.where(den > 0.0, 1.0 / den, 0.0)
            pcb_ref[rows, :] = pc.astype(BF16)
            p_heads = pc if p_heads is None else p_heads + pc
        p_parts.append(p_heads)
    p_sum = jnp.concatenate(p_parts, axis=0)
    o_cmp = _dot(pcb_ref[...], vc_ref[...])
    for h in range(HPG):
        oacc_ref[h] = gs_ref[:, 3 * h:3 * h + 1] * o_cmp[h * TQ:(h + 1) * TQ]

    def reset():
        m_ref[...] = jnp.full(m_ref.shape, NEG, F32)
        acc_ref[...] = jnp.zeros(acc_ref.shape, F32)

    def logits(k_ref, j, add, s_ref):
        k = k_ref[pl.ds(pl.multiple_of(j * TK, TK), TK), :]
        qk = _dot_nt(qa_ref[...].reshape(R, 2 * DH), k)
        s_ref[...] = (qk.reshape(HPG, TQ, TK) + add).reshape(R, TK)

    def update(v_ref, j, s_ref):
        v = v_ref[pl.ds(pl.multiple_of(j * TK, TK), TK), :]
        for r0 in range(0, R, SB):
            rows = slice(r0, r0 + SB)
            m_old = m_ref[rows, :]
            m_new = jnp.maximum(m_old, jnp.max(s_ref[rows, :], axis=-1, keepdims=True))
            alpha_ref[rows, :] = jnp.exp2(m_old - m_new)
            m_ref[rows, :] = m_new
        for r0 in range(0, R, SB):
            rows = slice(r0, r0 + SB)
            m_new = m_ref[rows, :]
            p = jnp.exp2(s_ref[rows, :] - jnp.concatenate([m_new] * (TK // LANES), axis=1))
            acc_ref[rows, :] = alpha_ref[rows, :] * acc_ref[rows, :] + _dot(p.astype(BF16), v)

    def finalize(gate_off):
        for h, a0 in subs:
            acc = acc_ref[h * TQ + a0:h * TQ + a0 + SB, :]
            o = (acc * (1.0 / pltpu.roll(acc, DH, 1)))[:, :DH]
            oacc_ref[h, a0:a0 + SB, :] += gate_col(a0, 3 * h + gate_off) * o

    reset()
    win_bufs = (sa_ref, sb_ref, sc_ref)
    win_tiles = []
    for n in range(N_WIN_TILES):
        dd = N_WIN_TILES - 1 - n
        j = jnp.maximum(i - dd, 0)
        logits(kw_ref, j, btw_ref[jnp.where(i < dd, N_WIN_TILES, dd)], win_bufs[n])
        win_tiles.append(j)
    for n in range(N_WIN_TILES):
        update(vw_ref, win_tiles[n], win_bufs[n])
    finalize(2)

    imp_t = _dot_nt(ovl_ref[...], p_sum, precision=lax.Precision.HIGHEST)
    s_io = lax.broadcasted_iota(jnp.int32, (DH, TQ), 0)
    jcur = (t0 + lax.broadcasted_iota(jnp.int32, (DH, TQ), 1)) >> SEL_SHIFT
    forced = (s_io == 0) | (s_io == jcur) | (s_io == jcur - 1)
    score = jnp.where(forced, 1e6, jnp.where(s_io <= jcur, imp_t, -1e6))
    sub8 = lax.broadcasted_iota(jnp.int32, (8, TQ), 0)
    cnt = [jnp.zeros((8, TQ), jnp.int32) for _ in range(DH // 8)]
    for sp in range(n_sel_blocks):
        row = score[sp:sp + 1, :]
        for g in range(DH // 8):
            blk = score[8 * g:8 * g + 8, :]
            if 8 * g > sp:
                beats = row >= blk
            elif 8 * g + 7 <= sp:
                beats = row > blk
            else:
                beats = (row > blk) | ((row == blk) & (sub8 > sp - 8 * g))
            cnt[g] = cnt[g] + jnp.where(beats, 1, 0)
    rank = jnp.concatenate(cnt, axis=0)
    drop_t = jnp.where((rank < n_top) & (s_io < n_sel_blocks), 0.0, NEG)
    drop = jnp.concatenate([jnp.zeros((DH, TQ), F32), drop_t], axis=0).T.astype(BF16)
    for h in range(HPG):
        qa_ref[h, :, DH:] = drop[:, DH:]

    def sel_logits(j, s_ref):
        logits(ks_ref, j, bts_ref[jnp.minimum(i - j, N_BIAS_TILES - 1)], s_ref)

    reset()
    n_pairs = i // 2
    sel_logits(0, sa_ref)

    def step(pair, carry):
        j = 2 * pair
        sel_logits(j + 1, sb_ref)
        update(vs_ref, j, sa_ref)
        sel_logits(j + 2, sa_ref)
        update(vs_ref, j + 1, sb_ref)
        return carry

    lax.fori_loop(0, n_pairs, step, 0)
    j_tail = 2 * n_pairs

    @pl.when(j_tail == i)
    def _():
        update(vs_ref, i, sa_ref)

    @pl.when(j_tail < i)
    def _():
        sel_logits(i, sb_ref)
        update(vs_ref, j_tail, sa_ref)
        update(vs_ref, i, sb_ref)

    finalize(1)

    for h in range(HPG):
        o_ref[:, h * DH:(h + 1) * DH] = oacc_ref[h].astype(o_ref.dtype)


def _nsa(proj3, kvc, ksw, vsw, bias_cmp, bias_toep, bias_win, ovl_t):
    B, S, _ = proj3.shape
    G, HPG, DH = NSA_GROUPS, NSA_HPG, NSA_DH
    GW = HPG * DH
    TQ = ATT_TILE
    NCP = kvc.shape[3]
    ns = S // SEL_BLOCK
    assert S % TQ == 0 and (HPG * TQ) % NSA_SUB == 0 and WINDOW % TQ == 0
    assert 2 * DH == LANES and ns <= DH and ovl_t.shape == (DH, NCP)
    assert C_QA % GW == 0 and C_GA % LANES == 0 and 3 * NSA_HEADS <= LANES

    def seq_spec(idx):
        return pl.BlockSpec((None, None, None, S, 2 * DH), lambda b, g, i: (idx, b, g, 0, 0))

    def cmp_spec(idx):
        return pl.BlockSpec((None, None, None, NCP, DH), lambda b, g, i: (idx, b, g, 0, 0))

    return pl.pallas_call(
        functools.partial(_nsa_body, n_sel_blocks=ns, n_top=min(SEL_TOPN, ns)),
        grid=(B, G, S // TQ),
        in_specs=[
            pl.BlockSpec((None, TQ, GW), lambda b, g, i: (b, i, C_QA // GW + g)),
            pl.BlockSpec((None, TQ, LANES), lambda b, g, i: (b, i, C_GA // LANES)),
            cmp_spec(0), cmp_spec(1),
            seq_spec(0), seq_spec(0), seq_spec(1), seq_spec(1),
            pl.BlockSpec((HPG, TQ, NCP), lambda b, g, i: (g, i, 0)),
            pl.BlockSpec((N_BIAS_TILES, HPG, TQ, TQ), lambda b, g, i: (0, g, 0, 0)),
            pl.BlockSpec((N_WIN_TILES + 1, HPG, TQ, TQ), lambda b, g, i: (0, g, 0, 0)),
            pl.BlockSpec((DH, NCP), lambda b, g, i: (0, 0)),
        ],
        out_specs=pl.BlockSpec((None, TQ, GW), lambda b, g, i: (b, i, g)),
        out_shape=jax.ShapeDtypeStruct((B, S, G * GW), BF16),
        scratch_shapes=[
            pltpu.VMEM((HPG, TQ, DH), BF16),
            pltpu.VMEM((HPG, TQ, 2 * DH), BF16),
            pltpu.VMEM((HPG * TQ, LANES), F32),
            pltpu.VMEM((HPG * TQ, 2 * DH), F32),
            pltpu.VMEM((HPG, TQ, DH), F32),
            pltpu.VMEM((HPG * TQ, TQ), F32),
            pltpu.VMEM((HPG * TQ, TQ), F32),
            pltpu.VMEM((HPG * TQ, TQ), F32),
            pltpu.VMEM((HPG * TQ, LANES), F32),
            pltpu.VMEM((TQ, LANES), F32),
            pltpu.VMEM((HPG * TQ, NCP), F32),
            pltpu.VMEM((HPG * TQ, NCP), BF16),
        ],
        compiler_params=_params("parallel", "parallel", "arbitrary"),
        name="nsa",
    )(proj3, proj3, kvc, kvc, ksw, vsw, ksw, vsw, bias_cmp, bias_toep, bias_win, ovl_t)


def _gla_body(q_ref, k_ref, v_ref, r_ref, al_ref, w2_ref, ab_ref, gn_ref, o_ref, st_ref, oi_ref):
    @pl.when(pl.program_id(1) == 0)
    def _():
        st_ref[...] = jnp.zeros_like(st_ref)

    C = GLA_CHUNK
    RB = q_ref.shape[0]
    n_chunks = RB // C
    pre = _dot(al_ref[...].astype(BF16), w2_ref[...]) + ab_ref[...]
    la = (jnp.minimum(pre, 0.0) - jnp.log1p(jnp.exp(-jnp.abs(pre)))) * (1.0 / GLA_TAU)

    la_hi = la.astype(BF16)
    rest = la - la_hi.astype(F32)
    la_mid = rest.astype(BF16)
    la_lo = (rest - la_mid.astype(F32)).astype(BF16)
    r_io = lax.broadcasted_iota(jnp.int32, (C, 3 * C), 0)
    c_io = lax.broadcasted_iota(jnp.int32, (C, 3 * C), 1)
    tri3 = ((c_io & (C - 1)) <= r_io).astype(BF16)
    b_parts, bl_parts = [], []
    for c in range(n_chunks):
        rows = slice(c * C, (c + 1) * C)
        b_c = _dot(tri3, jnp.concatenate([la_hi[rows], la_mid[rows], la_lo[rows]], axis=0))
        b_parts.append(b_c)
        bl_parts.append(jnp.broadcast_to(b_c[C - 1:C, :], b_c.shape))
    b = jnp.concatenate(b_parts, axis=0)
    b_last = jnp.concatenate(bl_parts, axis=0)

    k = k_ref[...]
    q_dec = (q_ref[...] * (jnp.exp(b) * (GLA_DK ** -0.5))).astype(BF16)
    k_intra = (k * jnp.exp(-b)).astype(BF16)
    k_state = (k * jnp.exp(b_last - b)).astype(BF16)

    rr = lax.broadcasted_iota(jnp.int32, (RB, RB), 0)
    cc = lax.broadcasted_iota(jnp.int32, (RB, RB), 1)
    same_chunk_causal = (cc <= rr) & ((rr & -C) == (cc & -C))
    for h in range(GLA_HEADS):
        kc = slice(h * GLA_DK, (h + 1) * GLA_DK)
        vc = slice(h * GLA_DV, (h + 1) * GLA_DV)
        v = v_ref[:, vc].astype(BF16)
        a = jnp.where(same_chunk_causal, _dot_nt(q_dec[:, kc], k_intra[:, kc]), 0.0)
        o_intra = _dot(a.astype(BF16), v)
        st = st_ref[h]
        for c in range(n_chunks):
            rows = slice(c * C, (c + 1) * C)
            oi_ref[rows, :] = _dot(q_dec[rows, kc], st.astype(BF16))
            decay = jnp.exp(b[(c + 1) * C - 8:(c + 1) * C, kc].T[:, 7:8])
            st = st * decay + _dot_tn(k_state[rows, kc], v[rows])
        st_ref[h] = st
        o = _rms(o_intra + oi_ref[...], gn_ref[:, vc])
        r = r_ref[:, vc]
        o_ref[:, vc] = (o * (r * jax.nn.sigmoid(r))).astype(o_ref.dtype)


def _gla(proj3, w2, ab, gn, layer):
    B, S, _ = proj3.shape
    RB = GLA_STEP
    HK = GLA_HEADS * GLA_DK
    HV = GLA_HEADS * GLA_DV
    assert S % RB == 0 and RB % GLA_CHUNK == 0

    def col(width, offset):
        assert offset % width == 0
        return pl.BlockSpec((None, RB, width), lambda b, s: (b, s, offset // width))

    return pl.pallas_call(
        _gla_body,
        grid=(B, S // RB),
        in_specs=[
            col(HK, C_QB), col(HK, C_KB), col(HV, C_VB), col(HV, C_RB), col(LANES, C_AL),
            pl.BlockSpec((None, LANES, HK), lambda b, s: (layer, 0, 0)),
            pl.BlockSpec((None, 1, HK), lambda b, s: (layer, 0, 0)),
            pl.BlockSpec((None, 1, HV), lambda b, s: (layer, 0, 0)),
        ],
        out_specs=pl.BlockSpec((None, RB, HV), lambda b, s: (b, s, 0)),
        out_shape=jax.ShapeDtypeStruct((B, S, HV), BF16),
        scratch_shapes=[pltpu.VMEM((GLA_HEADS, GLA_DK, GLA_DV), F32),
                        pltpu.VMEM((RB, GLA_DV), F32)],
        compiler_params=_params("parallel", "arbitrary"),
        name="gla",
    )(proj3, proj3, proj3, proj3, proj3, w2, ab, gn)


def _merge_body(x_ref, oa_ref, ob_ref, gm_ref, wa_ref, wb_ref, wo_ref, o_ref):
    D = x_ref.shape[1]
    gm = gm_ref[...]
    y = (jax.nn.sigmoid(gm[:, :D]) * _dot(oa_ref[...], wa_ref[...])
         + jax.nn.sigmoid(gm[:, D:]) * _dot(ob_ref[...], wb_ref[...]))
    o_ref[...] = x_ref[...] + _dot(y.astype(BF16), wo_ref[...])


def _merge(x, o_a, o_b, proj, wa, wb, wo, layer, tm=512):
    T, D = x.shape
    DA = o_a.shape[1]
    DB = o_b.shape[1]
    assert T % tm == 0 and C_GM == 0
    return pl.pallas_call(
        _merge_body,
        grid=(T // tm,),
        in_specs=[
            pl.BlockSpec((tm, D), lambda i: (i, 0)),
            pl.BlockSpec((tm, DA), lambda i: (i, 0)),
            pl.BlockSpec((tm, DB), lambda i: (i, 0)),
            pl.BlockSpec((tm, 2 * D), lambda i: (i, 0)),
            pl.BlockSpec((None, DA, D), lambda i: (layer, 0, 0)),
            pl.BlockSpec((None, DB, D), lambda i: (layer, 0, 0)),
            pl.BlockSpec((None, D, D), lambda i: (layer, 0, 0)),
        ],
        out_specs=pl.BlockSpec((tm, D), lambda i: (i, 0)),
        out_shape=jax.ShapeDtypeStruct((T, D), F32),
        compiler_params=_params("parallel"),
        name="merge",
    )(x, o_a, o_b, proj, wa, wb, wo)


def _rel_bucket(dist):
    n = jnp.maximum(dist, 0)
    exact = REL_BUCKETS // 2
    nf = jnp.maximum(n, 1).astype(jnp.float32)
    log_b = exact + (jnp.log(nf / exact) / math.log(REL_MAX_DIST / exact)
                     * (REL_BUCKETS - exact)).astype(jnp.int32)
    return jnp.where(n < exact, n, jnp.minimum(log_b, REL_BUCKETS - 1))


def _regroup_w_in(w_in):
    widths = (NSA_HEADS * NSA_DH, 6 * NSA_GROUPS * NSA_DH, 3 * NSA_HEADS, GLA_HEADS * GLA_DK,
              GLA_HEADS * GLA_DK, GLA_HEADS * GLA_DV, GLA_RANK, GLA_HEADS * GLA_DV, 2 * D_MODEL)
    offs = np.concatenate([[0], np.cumsum(widths)])
    q_a, kv_a, g_a, q_b, k_b, v_b, a_lr, r_b, g_m = (w_in[..., offs[n]:offs[n + 1]] for n in range(9))

    def pad(w):
        return jnp.pad(w, ((0, 0), (0, 0), (0, LANES - w.shape[-1])))

    out = jnp.concatenate([g_m, v_b, r_b, q_a, q_b, k_b, kv_a, pad(g_a), pad(a_lr)], axis=-1)
    assert out.shape[-1] == N_PROJ
    return out.astype(BF16)


def _overlap_t(ncp, nsp, nc, ns):
    c = np.arange(ncp)[None, :] * CMP_STRIDE
    s = np.arange(nsp)[:, None] * SEL_BLOCK
    ov = (c < s + SEL_BLOCK) & (c + CMP_BLOCK > s) & (np.arange(ncp)[None, :] < nc) & (np.arange(nsp)[:, None] < ns)
    return jnp.asarray(ov.astype(np.float32))


def kernel(x, rel_table, ffn1_norm, ffn1_w_gate, ffn1_w_up, ffn1_w_down, mix_norm, w_in, cmp_pos_k, cmp_pos_v, cmp_k_w1, cmp_k_w2, cmp_v_w1, cmp_v_w2, gla_a_w2, gla_a_b, gla_out_norm, w_branch_nsa, w_branch_gla, w_out, ffn2_norm, ffn2_w_gate, ffn2_w_up, ffn2_w_down, final_norm):
    B, S, D = x.shape
    L = w_in.shape[0]
    T = B * S
    G, HPG, DH = NSA_GROUPS, NSA_HPG, NSA_DH
    nch = S // CMP_STRIDE
    nc = (S - CMP_BLOCK) // CMP_STRIDE + 1
    ns = S // SEL_BLOCK
    assert D == D_MODEL and nc == nch - 1

    w1g, w1u, w1d = ffn1_w_gate.astype(BF16), ffn1_w_up.astype(BF16), ffn1_w_down.astype(BF16)
    w2g, w2u, w2d = ffn2_w_gate.astype(BF16), ffn2_w_up.astype(BF16), ffn2_w_down.astype(BF16)
    w_proj = _regroup_w_in(w_in)
    wa, wb, wo = w_branch_nsa.astype(BF16), w_branch_gla.astype(BF16), w_out.astype(BF16)
    cmp_pos = jnp.stack([cmp_pos_k, cmp_pos_v]).reshape(2, L, 1, CMP_BLOCK * DH)
    cmp_w1 = jnp.stack([cmp_k_w1, cmp_v_w1]).astype(BF16)
    cmp_w2 = jnp.stack([cmp_k_w2, cmp_v_w2]).astype(BF16)
    gla_w2 = jnp.pad(gla_a_w2, ((0, 0), (0, LANES - GLA_RANK), (0, 0))).astype(BF16)
    gla_b = gla_a_b.reshape(L, 1, -1)
    gla_gn = gla_out_norm.reshape(L, 1, -1)
    n1 = ffn1_norm.reshape(L, 1, D)
    n2 = ffn2_norm.reshape(L, 1, D)
    nm = mix_norm.reshape(L, 1, D)

    buckets = _rel_bucket(jnp.arange(REL_MAX_DIST + 1, dtype=jnp.int32))
    thr = jnp.searchsorted(buckets, jnp.arange(REL_BUCKETS, dtype=jnp.int32), side="left").astype(jnp.int32)
    bias_toep, bias_win, bias_cmp = _bias_tables(thr, rel_table, S, nch)
    ovl_t = _overlap_t(nch, DH, nc, ns)

    xf = x.reshape(T, D)
    for l in range(L):
        xf = _ffn(xf, n1, w1g, w1u, w1d, l)
        proj = _proj(xf, nm, w_proj, l)
        proj3 = proj.reshape(B, S, N_PROJ)

        xc, ksw, vsw = _kvprep(proj3)
        kvc = _compress(xc, cmp_pos, cmp_w1, cmp_w2, l)
        o_a = _nsa(proj3, kvc, ksw, vsw, bias_cmp, bias_toep, bias_win, ovl_t)
        o_a = o_a.reshape(T, NSA_HEADS * DH)

        o_b = _gla(proj3, gla_w2, gla_b, gla_gn, l).reshape(T, GLA_HEADS * GLA_DV)

        xf = _merge(xf, o_a, o_b, proj, wa, wb, wo, l)
        xf = _ffn(xf, n2, w2g, w2u, w2d, l,
                  final_g=final_norm.reshape(1, D) if l == L - 1 else None)
    return xf.reshape(B, S, D)
```

```python
import functools
import math

import numpy as np
import jax
import jax.numpy as jnp
from jax import lax
from jax.experimental import pallas as pl
from jax.experimental.pallas import tpu as pltpu

F32 = jnp.float32
BF16 = jnp.bfloat16

NSA_HEADS = 8
NSA_GROUPS = 2
NSA_HPG = NSA_HEADS // NSA_GROUPS
NSA_DH = 64
CMP_BLOCK = 32
CMP_STRIDE = 16
SEL_BLOCK = 64
SEL_SHIFT = 6
SEL_TOPN = 16
WINDOW = 512
GLA_HEADS = 4
GLA_DK = 128
GLA_DV = 256
GLA_RANK = 16
GLA_TAU = 16.0
GLA_CHUNK = 64
REL_BUCKETS = 32
REL_MAX_DIST = 1024
EPS = 1e-6
NEG = -1e30
LOG2E = math.log2(math.e)

LANES = 128
VMEM_LIMIT = 56 * 1024 * 1024

ATT_TILE = 256
N_BIAS_TILES = REL_MAX_DIST // ATT_TILE + 2
N_WIN_TILES = WINDOW // ATT_TILE + 1
NSA_SUB = 128
GLA_STEP = 256

D_MODEL = 1024
C_GM = 0
C_VB = 2048
C_RB = 3072
C_QA = 4096
C_QB = 4608
C_KB = 5120
C_KV = 5632
C_GA = 6400
C_AL = 6528
N_PROJ = 6656


def _dot(a, b, precision=None):
    return lax.dot_general(a, b, (((1,), (0,)), ((), ())), precision=precision,
                           preferred_element_type=F32)


def _dot_nt(a, b, precision=None):
    return lax.dot_general(a, b, (((1,), (1,)), ((), ())), precision=precision,
                           preferred_element_type=F32)


def _dot_tn(a, b, precision=None):
    return lax.dot_general(a, b, (((0,), (0,)), ((), ())), precision=precision,
                           preferred_element_type=F32)


def _rms(x, g):
    return x * lax.rsqrt(jnp.mean(x * x, axis=-1, keepdims=True) + EPS) * g


def _params(*sem):
    return pltpu.CompilerParams(dimension_semantics=sem, vmem_limit_bytes=VMEM_LIMIT)


def _ffn_body(x_ref, g_ref, wg_ref, wu_ref, wd_ref, *rest, final, fc):
    if final:
        fg_ref, o_ref = rest
    else:
        (o_ref,) = rest
    x = x_ref[...]
    h = _rms(x, g_ref[...]).astype(BF16)
    acc = None
    for f0 in range(0, wg_ref.shape[1], fc):
        gate = _dot(h, wg_ref[:, f0:f0 + fc])
        up = _dot(h, wu_ref[:, f0:f0 + fc])
        act = (gate * jax.nn.sigmoid(gate) * up).astype(BF16)
        down = _dot(act, wd_ref[f0:f0 + fc, :])
        acc = down if acc is None else acc + down
    y = x + 0.5 * acc
    if final:
        y = _rms(y, fg_ref[...])
    o_ref[...] = y


def _ffn(x, g, wg, wu, wd, layer, final_g=None, tm=512, fc=704):
    T, D = x.shape
    F = wg.shape[-1]
    assert T % tm == 0 and F % fc == 0
    final = final_g is not None
    resident = pl.Buffered(1)
    in_specs = [
        pl.BlockSpec((tm, D), lambda i: (i, 0)),
        pl.BlockSpec((None, 1, D), lambda i: (layer, 0, 0)),
        pl.BlockSpec((None, D, F), lambda i: (layer, 0, 0), pipeline_mode=resident),
        pl.BlockSpec((None, D, F), lambda i: (layer, 0, 0), pipeline_mode=resident),
        pl.BlockSpec((None, F, D), lambda i: (layer, 0, 0), pipeline_mode=resident),
    ]
    args = [x, g, wg, wu, wd]
    if final:
        in_specs.append(pl.BlockSpec((1, D), lambda i: (0, 0)))
        args.append(final_g)
    return pl.pallas_call(
        functools.partial(_ffn_body, final=final, fc=fc),
        grid=(T // tm,),
        in_specs=in_specs,
        out_specs=pl.BlockSpec((tm, D), lambda i: (i, 0)),
        out_shape=jax.ShapeDtypeStruct((T, D), F32),
        compiler_params=_params("parallel"),
        name="ffn",
    )(*args)


def _proj_body(x_ref, g_ref, w_ref, o_ref, *, nc):
    h = _rms(x_ref[...], g_ref[...]).astype(BF16)
    for n0 in range(0, w_ref.shape[1], nc):
        o_ref[:, n0:n0 + nc] = _dot(h, w_ref[:, n0:n0 + nc])


def _proj(x, g, w, layer, tm=512, nc=1664):
    T, D = x.shape
    N = w.shape[-1]
    assert T % tm == 0 and N % nc == 0
    return pl.pallas_call(
        functools.partial(_proj_body, nc=nc),
        grid=(T // tm,),
        in_specs=[
            pl.BlockSpec((tm, D), lambda i: (i, 0)),
            pl.BlockSpec((None, 1, D), lambda i: (layer, 0, 0)),
            pl.BlockSpec((None, D, N), lambda i: (layer, 0, 0), pipeline_mode=pl.Buffered(1)),
        ],
        out_specs=pl.BlockSpec((tm, N), lambda i: (i, 0)),
        out_shape=jax.ShapeDtypeStruct((T, N), F32),
        compiler_params=_params("parallel"),
        name="proj",
    )(x, g, w)


def _kvprep_body(kc_ref, vc_ref, ks_ref, vs_ref, kw_ref, vw_ref, xc_ref, ksw_ref, vsw_ref):
    TS = ks_ref.shape[0]
    DH = NSA_DH
    t = pl.program_id(1) * TS + lax.broadcasted_iota(jnp.int32, (TS, DH), 0)
    onehot = ((t >> SEL_SHIFT) == lax.broadcasted_iota(jnp.int32, (TS, DH), 1)).astype(BF16)
    ones = jnp.ones((TS, DH), BF16)
    zeros = jnp.zeros((TS, DH), BF16)
    for g in range(NSA_GROUPS):
        cols = slice(g * DH, (g + 1) * DH)
        ksw_ref[0, g] = jnp.concatenate([ks_ref[:, cols].astype(BF16), onehot], axis=1)
        ksw_ref[1, g] = jnp.concatenate([kw_ref[:, cols].astype(BF16), zeros], axis=1)
        vsw_ref[0, g] = jnp.concatenate([vs_ref[:, cols].astype(BF16), ones], axis=1)
        vsw_ref[1, g] = jnp.concatenate([vw_ref[:, cols].astype(BF16), ones], axis=1)
    for s, src in enumerate((kc_ref, vc_ref)):
        for l in range(CMP_STRIDE):
            x = src[pl.ds(l, TS // CMP_STRIDE, stride=CMP_STRIDE), :]
            for g in range(NSA_GROUPS):
                xc_ref[s, g, :, l * DH:(l + 1) * DH] = x[:, g * DH:(g + 1) * DH]


def _kvprep(proj3, ts=512):
    B, S, _ = proj3.shape
    G, DH = NSA_GROUPS, NSA_DH
    GW = G * DH
    assert S % ts == 0 and ts % (8 * CMP_STRIDE) == 0 and C_KV % GW == 0

    def col(n):
        return pl.BlockSpec((None, ts, GW), lambda b, s: (b, s, C_KV // GW + n))

    def out(width, rows):
        return pl.BlockSpec((2, None, G, rows, width), lambda b, s: (0, b, 0, s, 0))

    nch = S // CMP_STRIDE
    return pl.pallas_call(
        _kvprep_body,
        grid=(B, S // ts),
        in_specs=[col(n) for n in range(6)],
        out_specs=[out(CMP_STRIDE * DH, ts // CMP_STRIDE), out(2 * DH, ts), out(2 * DH, ts)],
        out_shape=[jax.ShapeDtypeStruct((2, B, G, nch, CMP_STRIDE * DH), F32),
                   jax.ShapeDtypeStruct((2, B, G, S, 2 * DH), BF16),
                   jax.ShapeDtypeStruct((2, B, G, S, 2 * DH), BF16)],
        compiler_params=_params("parallel", "parallel"),
        name="kvprep",
    )(*([proj3] * 6))


def _compress_body(x_ref, pos_ref, w1_ref, w2_ref, o_ref):
    x = x_ref[...]
    half = x.shape[1]
    lo = (x + pos_ref[:, :half]).astype(BF16)
    hi = (x + pos_ref[:, half:]).astype(BF16)
    h_lo = _dot(lo, w1_ref[:half, :])
    h_hi = _dot(hi, w1_ref[half:, :])
    nch = x.shape[0]
    hid = h_lo + pltpu.roll(h_hi, nch - 1, 0)
    act = (hid * jax.nn.sigmoid(hid)).astype(BF16)
    o_ref[...] = _dot(act, w2_ref[...]).astype(o_ref.dtype)


def _compress(xc, pos, w1, w2, layer):
    _, B, G, NCH, CW = xc.shape
    HC = w1.shape[-1]
    dh = w2.shape[-1]
    return pl.pallas_call(
        _compress_body,
        grid=(2, B, G),
        in_specs=[
            pl.BlockSpec((None, None, None, NCH, CW), lambda s, b, g: (s, b, g, 0, 0)),
            pl.BlockSpec((None, None, 1, 2 * CW), lambda s, b, g: (s, layer, 0, 0)),
            pl.BlockSpec((None, None, 2 * CW, HC), lambda s, b, g: (s, layer, 0, 0)),
            pl.BlockSpec((None, None, HC, dh), lambda s, b, g: (s, layer, 0, 0)),
        ],
        out_specs=pl.BlockSpec((None, None, None, NCH, dh), lambda s, b, g: (s, b, g, 0, 0)),
        out_shape=jax.ShapeDtypeStruct((2, B, G, NCH, dh), BF16),
        compiler_params=_params("parallel", "parallel", "parallel"),
        name="compress",
    )(xc, pos, w1, w2)


def _bias_lookup(n, thr_ref, tab_ref, h):
    val = jnp.full(n.shape, tab_ref[0, h], F32)
    for k in range(1, REL_BUCKETS):
        val = jnp.where(n >= thr_ref[k], tab_ref[k, h], val)
    return val


def _toeplitz_body(thr_ref, tab_ref, o_ref, *, rows, window):
    dd = pl.program_id(0)
    h = pl.program_id(1)
    T = o_ref.shape[-1]
    for r0 in range(0, T, rows):
        a = lax.broadcasted_iota(jnp.int32, (rows, T), 0) + r0
        b = lax.broadcasted_iota(jnp.int32, (rows, T), 1)
        dist = dd * T + a - b
        val = _bias_lookup(jnp.clip(dist, 0, REL_MAX_DIST), thr_ref, tab_ref, h) * LOG2E
        keep = (dist >= 0) & (dist < WINDOW) if window else dist >= 0
        o_ref[r0:r0 + rows, :] = jnp.where(keep, val, NEG)


def _cmpbias_body(thr_ref, tab_ref, o_ref, *, rows):
    h = pl.program_id(0)
    i = pl.program_id(1)
    TQ, NC = o_ref.shape
    for r0 in range(0, TQ, rows):
        t = lax.broadcasted_iota(jnp.int32, (rows, NC), 0) + (i * TQ + r0)
        c = lax.broadcasted_iota(jnp.int32, (rows, NC), 1)
        n = jnp.clip(t - (c * CMP_STRIDE + (CMP_BLOCK - 1)), 0, REL_MAX_DIST)
        o_ref[r0:r0 + rows, :] = _bias_lookup(n, thr_ref, tab_ref, h)


def _bias_tables(thr, rel_table, S, ncp):
    T = ATT_TILE
    smem = pl.BlockSpec(memory_space=pltpu.SMEM)

    def toeplitz(n_tiles, window, name):
        return pl.pallas_call(
            functools.partial(_toeplitz_body, rows=32, window=window),
            grid=(n_tiles, NSA_HEADS),
            in_specs=[smem, smem],
            out_specs=pl.BlockSpec((None, None, T, T), lambda d, h: (d, h, 0, 0)),
            out_shape=jax.ShapeDtypeStruct((n_tiles, NSA_HEADS, T, T), F32),
            compiler_params=_params("parallel", "parallel"),
            name=name,
        )(thr, rel_table)

    toep = toeplitz(N_BIAS_TILES, False, "bias_toeplitz")
    toep_win = toeplitz(N_WIN_TILES + 1, True, "bias_window")
    cmpb = pl.pallas_call(
        functools.partial(_cmpbias_body, rows=32),
        grid=(NSA_HEADS, S // T),
        in_specs=[smem, smem],
        out_specs=pl.BlockSpec((None, T, ncp), lambda h, i: (h, i, 0)),
        out_shape=jax.ShapeDtypeStruct((NSA_HEADS, S, ncp), F32),
        compiler_params=_params("parallel", "parallel"),
        name="bias_cmp",
    )(thr, rel_table)
    return toep, toep_win, cmpb


def _nsa_body(q_ref, gate_ref, kc_ref, vc_ref, ks_ref, vs_ref, kw_ref, vw_ref, bc_ref, bts_ref, btw_ref,
              ovl_ref, o_ref, qc_ref, qa_ref, m_ref, acc_ref, oacc_ref, sa_ref, sb_ref, sc_ref, alpha_ref,
              gs_ref, lc_ref, pcb_ref, ps_ref,
              *, n_sel_blocks, n_top):
    group = pl.program_id(1)
    i = pl.program_id(2)
    HPG, TQ, DH = qc_ref.shape
    TK = TQ
    SB = NSA_SUB
    NCP = kc_ref.shape[0]
    t0 = i * TQ
    R = HPG * TQ
    subs = [(h, a0) for a0 in range(0, TQ, SB) for h in range(HPG)]

    for h in range(HPG):
        q = q_ref[:, h * DH:(h + 1) * DH]
        qc_ref[h] = (q * (DH ** -0.5)).astype(BF16)
        qa_ref[h, :, :DH] = (q * (DH ** -0.5 * LOG2E)).astype(BF16)
        qa_ref[h, :, DH:] = jnp.zeros((TQ, DH), BF16)

    gates = jax.nn.sigmoid(gate_ref[...])
    gs_ref[...] = jnp.where(group == 0, gates, pltpu.roll(gates, LANES - 3 * HPG, 1))

    def gate_col(a0, col):
        return gs_ref[a0:a0 + SB, col:col + 1]

    lc_all = _dot_nt(qc_ref[...].reshape(R, DH), kc_ref[...])
    lc_ref[...] = (lc_all.reshape(HPG, TQ, NCP) + bc_ref[...]).reshape(R, NCP)
    c_end = lax.broadcasted_iota(jnp.int32, (SB, NCP), 1) * CMP_STRIDE + (CMP_BLOCK - 1)
    r_c = lax.broadcasted_iota(jnp.int32, (SB, NCP), 0)
    for a0 in range(0, TQ, SB):
        mc = c_end <= (t0 + a0 + r_c)
        p_heads = None
        for h in range(HPG):
            rows = slice(h * TQ + a0, h * TQ + a0 + SB)
            lc = jnp.where(mc, lc_ref[rows, :], NEG)
            pc = jnp.where(mc, jnp.exp(lc - jnp.max(lc, axis=-1, keepdims=True)), 0.0)
            den = jnp.sum(pc, axis=-1, keepdims=True)
            pc = pc * jnp.where(den > 0.0, 1.0 / den, 0.0)
            pcb_ref[rows, :] = pc.astype(BF16)
            p_heads = pc if p_heads is None else p_heads + pc
        ps_ref[a0:a0 + SB, :] = p_heads
    o_cmp = _dot(pcb_ref[...], vc_ref[...])
    for h in range(HPG):
        oacc_ref[h] = gs_ref[:, 3 * h:3 * h + 1] * o_cmp[h * TQ:(h + 1) * TQ]

    def reset():
        m_ref[...] = jnp.full(m_ref.shape, NEG, F32)
        acc_ref[...] = jnp.zeros(acc_ref.shape, F32)

    def logits(k_ref, j, bias_ref, bias_tile, s_ref):
        k = k_ref[pl.ds(pl.multiple_of(j * TK, TK), TK), :]
        qk = _dot_nt(qa_ref[...].reshape(R, 2 * DH), k)
        s_ref[...] = (qk.reshape(HPG, TQ, TK) + bias_ref[bias_tile]).reshape(R, TK)

    def update(v_ref, j, s_ref):
        v = v_ref[pl.ds(pl.multiple_of(j * TK, TK), TK), :]
        for r0 in range(0, R, SB):
            rows = slice(r0, r0 + SB)
            m_old = m_ref[rows, :]
            m_new = jnp.maximum(m_old, jnp.max(s_ref[rows, :], axis=-1, keepdims=True))
            alpha_ref[rows, :] = jnp.exp2(m_old - m_new)
            m_ref[rows, :] = m_new
        for r0 in range(0, R, SB):
            rows = slice(r0, r0 + SB)
            m_new = m_ref[rows, :]
            p = jnp.exp2(s_ref[rows, :] - jnp.concatenate([m_new] * (TK // LANES), axis=1))
            acc_ref[rows, :] = alpha_ref[rows, :] * acc_ref[rows, :] + _dot(p.astype(BF16), v)

    def finalize(gate_off):
        for h, a0 in subs:
            acc = acc_ref[h * TQ + a0:h * TQ + a0 + SB, :]
            o = (acc * (1.0 / pltpu.roll(acc, DH, 1)))[:, :DH]
            oacc_ref[h, a0:a0 + SB, :] += gate_col(a0, 3 * h + gate_off) * o

    reset()
    win_bufs = (sa_ref, sb_ref, sc_ref)
    win_tiles = []
    for n in range(N_WIN_TILES):
        dd = N_WIN_TILES - 1 - n
        j = jnp.maximum(i - dd, 0)
        logits(kw_ref, j, btw_ref, jnp.where(i < dd, N_WIN_TILES, dd), win_bufs[n])
        win_tiles.append(j)
    for n in range(N_WIN_TILES):
        update(vw_ref, win_tiles[n], win_bufs[n])
    finalize(2)

    imp_t = _dot_nt(ovl_ref[...], ps_ref[...], precision=lax.Precision.HIGHEST)
    s_io = lax.broadcasted_iota(jnp.int32, (DH, TQ), 0)
    jcur = (t0 + lax.broadcasted_iota(jnp.int32, (DH, TQ), 1)) >> SEL_SHIFT
    forced = (s_io == 0) | (s_io == jcur) | (s_io == jcur - 1)
    score = jnp.where(forced, 1e6, jnp.where(s_io <= jcur, imp_t, -1e6))
    sub8 = lax.broadcasted_iota(jnp.int32, (8, TQ), 0)
    cnt = [jnp.zeros((8, TQ), jnp.int32) for _ in range(DH // 8)]
    for sp in range(n_sel_blocks):
        row = score[sp:sp + 1, :]
        for g in range(DH // 8):
            blk = score[8 * g:8 * g + 8, :]
            if 8 * g > sp:
                beats = row >= blk
            elif 8 * g + 7 <= sp:
                beats = row > blk
            else:
                beats = (row > blk) | ((row == blk) & (sub8 > sp - 8 * g))
            cnt[g] = cnt[g] + jnp.where(beats, 1, 0)
    rank = jnp.concatenate(cnt, axis=0)
    drop_t = jnp.where((rank < n_top) & (s_io < n_sel_blocks), 0.0, NEG)
    drop = jnp.concatenate([jnp.zeros((DH, TQ), F32), drop_t], axis=0).T.astype(BF16)
    for h in range(HPG):
        qa_ref[h, :, DH:] = drop[:, DH:]

    def sel_logits(j, s_ref):
        logits(ks_ref, j, bts_ref, jnp.minimum(i - j, N_BIAS_TILES - 1), s_ref)

    reset()
    n_pairs = i // 2
    sel_logits(0, sa_ref)

    def step(pair, carry):
        j = 2 * pair
        sel_logits(j + 1, sb_ref)
        update(vs_ref, j, sa_ref)
        sel_logits(j + 2, sa_ref)
        update(vs_ref, j + 1, sb_ref)
        return carry

    lax.fori_loop(0, n_pairs, step, 0)
    j_tail = 2 * n_pairs

    @pl.when(j_tail == i)
    def _():
        update(vs_ref, i, sa_ref)

    @pl.when(j_tail < i)
    def _():
        sel_logits(i, sb_ref)
        update(vs_ref, j_tail, sa_ref)
        update(vs_ref, i, sb_ref)

    finalize(1)

    for h in range(HPG):
        o_ref[:, h * DH:(h + 1) * DH] = oacc_ref[h].astype(o_ref.dtype)


def _nsa(proj3, kvc, ksw, vsw, bias_cmp, bias_toep, bias_win, ovl_t):
    B, S, _ = proj3.shape
    G, HPG, DH = NSA_GROUPS, NSA_HPG, NSA_DH
    GW = HPG * DH
    TQ = ATT_TILE
    NCP = kvc.shape[3]
    ns = S // SEL_BLOCK
    assert S % TQ == 0 and (HPG * TQ) % NSA_SUB == 0 and WINDOW % TQ == 0
    assert 2 * DH == LANES and ns <= DH and ovl_t.shape == (DH, NCP)
    assert C_QA % GW == 0 and C_GA % LANES == 0 and 3 * NSA_HEADS <= LANES

    def seq_spec(idx):
        return pl.BlockSpec((None, None, None, S, 2 * DH), lambda b, g, i: (idx, b, g, 0, 0))

    def cmp_spec(idx):
        return pl.BlockSpec((None, None, None, NCP, DH), lambda b, g, i: (idx, b, g, 0, 0))

    return pl.pallas_call(
        functools.partial(_nsa_body, n_sel_blocks=ns, n_top=min(SEL_TOPN, ns)),
        grid=(B, G, S // TQ),
        in_specs=[
            pl.BlockSpec((None, TQ, GW), lambda b, g, i: (b, i, C_QA // GW + g)),
            pl.BlockSpec((None, TQ, LANES), lambda b, g, i: (b, i, C_GA // LANES)),
            cmp_spec(0), cmp_spec(1),
            seq_spec(0), seq_spec(0), seq_spec(1), seq_spec(1),
            pl.BlockSpec((HPG, TQ, NCP), lambda b, g, i: (g, i, 0)),
            pl.BlockSpec((N_BIAS_TILES, HPG, TQ, TQ), lambda b, g, i: (0, g, 0, 0)),
            pl.BlockSpec((N_WIN_TILES + 1, HPG, TQ, TQ), lambda b, g, i: (0, g, 0, 0)),
            pl.BlockSpec((DH, NCP), lambda b, g, i: (0, 0)),
        ],
        out_specs=pl.BlockSpec((None, TQ, GW), lambda b, g, i: (b, i, g)),
        out_shape=jax.ShapeDtypeStruct((B, S, G * GW), BF16),
        scratch_shapes=[
            pltpu.VMEM((HPG, TQ, DH), BF16),
            pltpu.VMEM((HPG, TQ, 2 * DH), BF16),
            pltpu.VMEM((HPG * TQ, LANES), F32),
            pltpu.VMEM((HPG * TQ, 2 * DH), F32),
            pltpu.VMEM((HPG, TQ, DH), F32),
            pltpu.VMEM((HPG * TQ, TQ), F32),
            pltpu.VMEM((HPG * TQ, TQ), F32),
            pltpu.VMEM((HPG * TQ, TQ), F32),
            pltpu.VMEM((HPG * TQ, LANES), F32),
            pltpu.VMEM((TQ, LANES), F32),
            pltpu.VMEM((HPG * TQ, NCP), F32),
            pltpu.VMEM((HPG * TQ, NCP), BF16),
            pltpu.VMEM((TQ, NCP), F32),
        ],
        compiler_params=_params("parallel", "parallel", "arbitrary"),
        name="nsa",
    )(proj3, proj3, kvc, kvc, ksw, vsw, ksw, vsw, bias_cmp, bias_toep, bias_win, ovl_t)


def _gla_body(q_ref, k_ref, v_ref, r_ref, al_ref, w2_ref, ab_ref, gn_ref, o_ref, st_ref, oi_ref):
    @pl.when(pl.program_id(1) == 0)
    def _():
        st_ref[...] = jnp.zeros_like(st_ref)

    C = GLA_CHUNK
    RB = q_ref.shape[0]
    n_chunks = RB // C
    pre = _dot(al_ref[...].astype(BF16), w2_ref[...]) + ab_ref[...]
    la = (jnp.minimum(pre, 0.0) - jnp.log1p(jnp.exp(-jnp.abs(pre)))) * (1.0 / GLA_TAU)

    la_hi = la.astype(BF16)
    rest = la - la_hi.astype(F32)
    la_mid = rest.astype(BF16)
    la_lo = (rest - la_mid.astype(F32)).astype(BF16)
    r_io = lax.broadcasted_iota(jnp.int32, (C, 3 * C), 0)
    c_io = lax.broadcasted_iota(jnp.int32, (C, 3 * C), 1)
    tri3 = ((c_io & (C - 1)) <= r_io).astype(BF16)
    b_parts, bl_parts = [], []
    for c in range(n_chunks):
        rows = slice(c * C, (c + 1) * C)
        b_c = _dot(tri3, jnp.concatenate([la_hi[rows], la_mid[rows], la_lo[rows]], axis=0))
        b_parts.append(b_c)
        bl_parts.append(jnp.broadcast_to(b_c[C - 1:C, :], b_c.shape))
    b = jnp.concatenate(b_parts, axis=0)
    b_last = jnp.concatenate(bl_parts, axis=0)

    k = k_ref[...]
    q_dec = (q_ref[...] * (jnp.exp(b) * (GLA_DK ** -0.5))).astype(BF16)
    k_intra = (k * jnp.exp(-b)).astype(BF16)
    k_state = (k * jnp.exp(b_last - b)).astype(BF16)

    rr = lax.broadcasted_iota(jnp.int32, (RB, RB), 0)
    cc = lax.broadcasted_iota(jnp.int32, (RB, RB), 1)
    same_chunk_causal = (cc <= rr) & ((rr & -C) == (cc & -C))
    for h in range(GLA_HEADS):
        kc = slice(h * GLA_DK, (h + 1) * GLA_DK)
        vc = slice(h * GLA_DV, (h + 1) * GLA_DV)
        v = v_ref[:, vc].astype(BF16)
        a = jnp.where(same_chunk_causal, _dot_nt(q_dec[:, kc], k_intra[:, kc]), 0.0)
        o_intra = _dot(a.astype(BF16), v)
        st = st_ref[h]
        for c in range(n_chunks):
            rows = slice(c * C, (c + 1) * C)
            oi_ref[rows, :] = _dot(q_dec[rows, kc], st.astype(BF16))
            decay = jnp.exp(b[(c + 1) * C - 8:(c + 1) * C, kc].T[:, 7:8])
            st = st * decay + _dot_tn(k_state[rows, kc], v[rows])
        st_ref[h] = st
        o = _rms(o_intra + oi_ref[...], gn_ref[:, vc])
        r = r_ref[:, vc]
        o_ref[:, vc] = (o * (r * jax.nn.sigmoid(r))).astype(o_ref.dtype)


def _gla(proj3, w2, ab, gn, layer):
    B, S, _ = proj3.shape
    RB = GLA_STEP
    HK = GLA_HEADS * GLA_DK
    HV = GLA_HEADS * GLA_DV
    assert S % RB == 0 and RB % GLA_CHUNK == 0

    def col(width, offset):
        assert offset % width == 0
        return pl.BlockSpec((None, RB, width), lambda b, s: (b, s, offset // width))

    return pl.pallas_call(
        _gla_body,
        grid=(B, S // RB),
        in_specs=[
            col(HK, C_QB), col(HK, C_KB), col(HV, C_VB), col(HV, C_RB), col(LANES, C_AL),
            pl.BlockSpec((None, LANES, HK), lambda b, s: (layer, 0, 0)),
            pl.BlockSpec((None, 1, HK), lambda b, s: (layer, 0, 0)),
            pl.BlockSpec((None, 1, HV), lambda b, s: (layer, 0, 0)),
        ],
        out_specs=pl.BlockSpec((None, RB, HV), lambda b, s: (b, s, 0)),
        out_shape=jax.ShapeDtypeStruct((B, S, HV), BF16),
        scratch_shapes=[pltpu.VMEM((GLA_HEADS, GLA_DK, GLA_DV), F32),
                        pltpu.VMEM((RB, GLA_DV), F32)],
        compiler_params=_params("parallel", "arbitrary"),
        name="gla",
    )(proj3, proj3, proj3, proj3, proj3, w2, ab, gn)


def _merge_body(x_ref, oa_ref, ob_ref, gm_ref, wa_ref, wb_ref, wo_ref, o_ref):
    D = x_ref.shape[1]
    gm = gm_ref[...]
    y = (jax.nn.sigmoid(gm[:, :D]) * _dot(oa_ref[...], wa_ref[...])
         + jax.nn.sigmoid(gm[:, D:]) * _dot(ob_ref[...], wb_ref[...]))
    o_ref[...] = x_ref[...] + _dot(y.astype(BF16), wo_ref[...])


def _merge(x, o_a, o_b, proj, wa, wb, wo, layer, tm=512):
    T, D = x.shape
    DA = o_a.shape[1]
    DB = o_b.shape[1]
    assert T % tm == 0 and C_GM == 0
    return pl.pallas_call(
        _merge_body,
        grid=(T // tm,),
        in_specs=[
            pl.BlockSpec((tm, D), lambda i: (i, 0)),
            pl.BlockSpec((tm, DA), lambda i: (i, 0)),
            pl.BlockSpec((tm, DB), lambda i: (i, 0)),
            pl.BlockSpec((tm, 2 * D), lambda i: (i, 0)),
            pl.BlockSpec((None, DA, D), lambda i: (layer, 0, 0)),
            pl.BlockSpec((None, DB, D), lambda i: (layer, 0, 0)),
            pl.BlockSpec((None, D, D), lambda i: (layer, 0, 0)),
        ],
        out_specs=pl.BlockSpec((tm, D), lambda i: (i, 0)),
        out_shape=jax.ShapeDtypeStruct((T, D), F32),
        compiler_params=_params("parallel"),
        name="merge",
    )(x, o_a, o_b, proj, wa, wb, wo)


def _rel_bucket(dist):
    n = jnp.maximum(dist, 0)
    exact = REL_BUCKETS // 2
    nf = jnp.maximum(n, 1).astype(jnp.float32)
    log_b = exact + (jnp.log(nf / exact) / math.log(REL_MAX_DIST / exact)
                     * (REL_BUCKETS - exact)).astype(jnp.int32)
    return jnp.where(n < exact, n, jnp.minimum(log_b, REL_BUCKETS - 1))


def _regroup_w_in(w_in):
    widths = (NSA_HEADS * NSA_DH, 6 * NSA_GROUPS * NSA_DH, 3 * NSA_HEADS, GLA_HEADS * GLA_DK,
              GLA_HEADS * GLA_DK, GLA_HEADS * GLA_DV, GLA_RANK, GLA_HEADS * GLA_DV, 2 * D_MODEL)
    offs = np.concatenate([[0], np.cumsum(widths)])
    q_a, kv_a, g_a, q_b, k_b, v_b, a_lr, r_b, g_m = (w_in[..., offs[n]:offs[n + 1]] for n in range(9))

    def pad(w):
        return jnp.pad(w, ((0, 0), (0, 0), (0, LANES - w.shape[-1])))

    out = jnp.concatenate([g_m, v_b, r_b, q_a, q_b, k_b, kv_a, pad(g_a), pad(a_lr)], axis=-1)
    assert out.shape[-1] == N_PROJ
    return out.astype(BF16)


def _overlap_t(ncp, nsp, nc, ns):
    c = np.arange(ncp)[None, :] * CMP_STRIDE
    s = np.arange(nsp)[:, None] * SEL_BLOCK
    ov = (c < s + SEL_BLOCK) & (c + CMP_BLOCK > s) & (np.arange(ncp)[None, :] < nc) & (np.arange(nsp)[:, None] < ns)
    return jnp.asarray(ov.astype(np.float32))


def kernel(x, rel_table, ffn1_norm, ffn1_w_gate, ffn1_w_up, ffn1_w_down, mix_norm, w_in, cmp_pos_k, cmp_pos_v, cmp_k_w1, cmp_k_w2, cmp_v_w1, cmp_v_w2, gla_a_w2, gla_a_b, gla_out_norm, w_branch_nsa, w_branch_gla, w_out, ffn2_norm, ffn2_w_gate, ffn2_w_up, ffn2_w_down, final_norm):
    B, S, D = x.shape
    L = w_in.shape[0]
    T = B * S
    G, HPG, DH = NSA_GROUPS, NSA_HPG, NSA_DH
    nch = S // CMP_STRIDE
    nc = (S - CMP_BLOCK) // CMP_STRIDE + 1
    ns = S // SEL_BLOCK
    assert D == D_MODEL and nc == nch - 1

    w1g, w1u, w1d = ffn1_w_gate.astype(BF16), ffn1_w_up.astype(BF16), ffn1_w_down.astype(BF16)
    w2g, w2u, w2d = ffn2_w_gate.astype(BF16), ffn2_w_up.astype(BF16), ffn2_w_down.astype(BF16)
    w_proj = _regroup_w_in(w_in)
    wa, wb, wo = w_branch_nsa.astype(BF16), w_branch_gla.astype(BF16), w_out.astype(BF16)
    cmp_pos = jnp.stack([cmp_pos_k, cmp_pos_v]).reshape(2, L, 1, CMP_BLOCK * DH)
    cmp_w1 = jnp.stack([cmp_k_w1, cmp_v_w1]).astype(BF16)
    cmp_w2 = jnp.stack([cmp_k_w2, cmp_v_w2]).astype(BF16)
    gla_w2 = jnp.pad(gla_a_w2, ((0, 0), (0, LANES - GLA_RANK), (0, 0))).astype(BF16)
    gla_b = gla_a_b.reshape(L, 1, -1)
    gla_gn = gla_out_norm.reshape(L, 1, -1)
    n1 = ffn1_norm.reshape(L, 1, D)
    n2 = ffn2_norm.reshape(L, 1, D)
    nm = mix_norm.reshape(L, 1, D)

    buckets = _rel_bucket(jnp.arange(REL_MAX_DIST + 1, dtype=jnp.int32))
    thr = jnp.searchsorted(buckets, jnp.arange(REL_BUCKETS, dtype=jnp.int32), side="left").astype(jnp.int32)
    bias_toep, bias_win, bias_cmp = _bias_tables(thr, rel_table, S, nch)
    ovl_t = _overlap_t(nch, DH, nc, ns)

    xf = x.reshape(T, D)
    for l in range(L):
        xf = _ffn(xf, n1, w1g, w1u, w1d, l)
        proj = _proj(xf, nm, w_proj, l)
        proj3 = proj.reshape(B, S, N_PROJ)

        xc, ksw, vsw = _kvprep(proj3)
        kvc = _compress(xc, cmp_pos, cmp_w1, cmp_w2, l)
        o_a = _nsa(proj3, kvc, ksw, vsw, bias_cmp, bias_toep, bias_win, ovl_t)
        o_a = o_a.reshape(T, NSA_HEADS * DH)

        o_b = _gla(proj3, gla_w2, gla_b, gla_gn, l).reshape(T, GLA_HEADS * GLA_DV)

        xf = _merge(xf, o_a, o_b, proj, wa, wb, wo, l)
        xf = _ffn(xf, n2, w2g, w2u, w2d, l,
                  final_g=final_norm.reshape(1, D) if l == L - 1 else None)
    return xf.reshape(B, S, D)
```

```python
import functools
import math

import numpy as np
import jax
import jax.numpy as jnp
from jax import lax
from jax.experimental import pallas as pl
from jax.experimental.pallas import tpu as pltpu

F32 = jnp.float32
BF16 = jnp.bfloat16

NSA_HEADS = 8
NSA_GROUPS = 2
NSA_HPG = NSA_HEADS // NSA_GROUPS
NSA_DH = 64
CMP_BLOCK = 32
CMP_STRIDE = 16
SEL_BLOCK = 64
SEL_SHIFT = 6
SEL_TOPN = 16
WINDOW = 512
GLA_HEADS = 4
GLA_DK = 128
GLA_DV = 256
GLA_RANK = 16
GLA_TAU = 16.0
GLA_CHUNK = 64
REL_BUCKETS = 32
REL_MAX_DIST = 1024
EPS = 1e-6
NEG = -1e30
LOG2E = math.log2(math.e)

LANES = 128
VMEM_LIMIT = 56 * 1024 * 1024

ATT_TILE = 256
N_BIAS_TILES = REL_MAX_DIST // ATT_TILE + 2
N_WIN_TILES = WINDOW // ATT_TILE + 1
NSA_SUB = 128
GLA_STEP = 512

D_MODEL = 1024
C_GM = 0
C_VB = 2048
C_RB = 3072
C_QA = 4096
C_QB = 4608
C_KB = 5120
C_KV = 5632
C_GA = 6400
C_AL = 6528
N_PROJ = 6656


def _dot(a, b, precision=None):
    return lax.dot_general(a, b, (((1,), (0,)), ((), ())), precision=precision,
                           preferred_element_type=F32)


def _dot_nt(a, b, precision=None):
    return lax.dot_general(a, b, (((1,), (1,)), ((), ())), precision=precision,
                           preferred_element_type=F32)


def _dot_tn(a, b, precision=None):
    return lax.dot_general(a, b, (((0,), (0,)), ((), ())), precision=precision,
                           preferred_element_type=F32)


def _rms(x, g):
    return x * lax.rsqrt(jnp.mean(x * x, axis=-1, keepdims=True) + EPS) * g


def _params(*sem):
    return pltpu.CompilerParams(dimension_semantics=sem, vmem_limit_bytes=VMEM_LIMIT)


def _ffn_body(x_ref, g_ref, wg_ref, wu_ref, wd_ref, *rest, final, fc):
    if final:
        fg_ref, o_ref = rest
    else:
        (o_ref,) = rest
    x = x_ref[...]
    h = _rms(x, g_ref[...]).astype(BF16)
    acc = None
    for f0 in range(0, wg_ref.shape[1], fc):
        gate = _dot(h, wg_ref[:, f0:f0 + fc])
        up = _dot(h, wu_ref[:, f0:f0 + fc])
        act = (gate * jax.nn.sigmoid(gate) * up).astype(BF16)
        down = _dot(act, wd_ref[f0:f0 + fc, :])
        acc = down if acc is None else acc + down
    y = x + 0.5 * acc
    if final:
        y = _rms(y, fg_ref[...])
    o_ref[...] = y


def _ffn(x, g, wg, wu, wd, layer, final_g=None, tm=512, fc=704):
    T, D = x.shape
    F = wg.shape[-1]
    assert T % tm == 0 and F % fc == 0
    final = final_g is not None
    resident = pl.Buffered(1)
    in_specs = [
        pl.BlockSpec((tm, D), lambda i: (i, 0)),
        pl.BlockSpec((None, 1, D), lambda i: (layer, 0, 0)),
        pl.BlockSpec((None, D, F), lambda i: (layer, 0, 0), pipeline_mode=resident),
        pl.BlockSpec((None, D, F), lambda i: (layer, 0, 0), pipeline_mode=resident),
        pl.BlockSpec((None, F, D), lambda i: (layer, 0, 0), pipeline_mode=resident),
    ]
    args = [x, g, wg, wu, wd]
    if final:
        in_specs.append(pl.BlockSpec((1, D), lambda i: (0, 0)))
        args.append(final_g)
    return pl.pallas_call(
        functools.partial(_ffn_body, final=final, fc=fc),
        grid=(T // tm,),
        in_specs=in_specs,
        out_specs=pl.BlockSpec((tm, D), lambda i: (i, 0)),
        out_shape=jax.ShapeDtypeStruct((T, D), F32),
        compiler_params=_params("parallel"),
        name="ffn",
    )(*args)


def _proj_body(x_ref, g_ref, w_ref, o_ref, *, nc):
    h = _rms(x_ref[...], g_ref[...]).astype(BF16)
    for n0 in range(0, w_ref.shape[1], nc):
        o_ref[:, n0:n0 + nc] = _dot(h, w_ref[:, n0:n0 + nc])


def _proj(x, g, w, layer, tm=512, nc=1664):
    T, D = x.shape
    N = w.shape[-1]
    assert T % tm == 0 and N % nc == 0
    return pl.pallas_call(
        functools.partial(_proj_body, nc=nc),
        grid=(T // tm,),
        in_specs=[
            pl.BlockSpec((tm, D), lambda i: (i, 0)),
            pl.BlockSpec((None, 1, D), lambda i: (layer, 0, 0)),
            pl.BlockSpec((None, D, N), lambda i: (layer, 0, 0), pipeline_mode=pl.Buffered(1)),
        ],
        out_specs=pl.BlockSpec((tm, N), lambda i: (i, 0)),
        out_shape=jax.ShapeDtypeStruct((T, N), F32),
        compiler_params=_params("parallel"),
        name="proj",
    )(x, g, w)


def _kvprep_body(kc_ref, vc_ref, ks_ref, vs_ref, kw_ref, vw_ref, xc_ref, ksw_ref, vsw_ref):
    TS = ks_ref.shape[0]
    DH = NSA_DH
    t = pl.program_id(1) * TS + lax.broadcasted_iota(jnp.int32, (TS, DH), 0)
    onehot = ((t >> SEL_SHIFT) == lax.broadcasted_iota(jnp.int32, (TS, DH), 1)).astype(BF16)
    ones = jnp.ones((TS, DH), BF16)
    zeros = jnp.zeros((TS, DH), BF16)
    for g in range(NSA_GROUPS):
        cols = slice(g * DH, (g + 1) * DH)
        ksw_ref[0, g] = jnp.concatenate([ks_ref[:, cols].astype(BF16), onehot], axis=1)
        ksw_ref[1, g] = jnp.concatenate([kw_ref[:, cols].astype(BF16), zeros], axis=1)
        vsw_ref[0, g] = jnp.concatenate([vs_ref[:, cols].astype(BF16), ones], axis=1)
        vsw_ref[1, g] = jnp.concatenate([vw_ref[:, cols].astype(BF16), ones], axis=1)
    for s, src in enumerate((kc_ref, vc_ref)):
        for l in range(CMP_STRIDE):
            x = src[pl.ds(l, TS // CMP_STRIDE, stride=CMP_STRIDE), :]
            for g in range(NSA_GROUPS):
                xc_ref[s, g, :, l * DH:(l + 1) * DH] = x[:, g * DH:(g + 1) * DH]


def _kvprep(proj3, ts=512):
    B, S, _ = proj3.shape
    G, DH = NSA_GROUPS, NSA_DH
    GW = G * DH
    assert S % ts == 0 and ts % (8 * CMP_STRIDE) == 0 and C_KV % GW == 0

    def col(n):
        return pl.BlockSpec((None, ts, GW), lambda b, s: (b, s, C_KV // GW + n))

    def out(width, rows):
        return pl.BlockSpec((2, None, G, rows, width), lambda b, s: (0, b, 0, s, 0))

    nch = S // CMP_STRIDE
    return pl.pallas_call(
        _kvprep_body,
        grid=(B, S // ts),
        in_specs=[col(n) for n in range(6)],
        out_specs=[out(CMP_STRIDE * DH, ts // CMP_STRIDE), out(2 * DH, ts), out(2 * DH, ts)],
        out_shape=[jax.ShapeDtypeStruct((2, B, G, nch, CMP_STRIDE * DH), F32),
                   jax.ShapeDtypeStruct((2, B, G, S, 2 * DH), BF16),
                   jax.ShapeDtypeStruct((2, B, G, S, 2 * DH), BF16)],
        compiler_params=_params("parallel", "parallel"),
        name="kvprep",
    )(*([proj3] * 6))


def _compress_body(x_ref, pos_ref, w1_ref, w2_ref, o_ref):
    x = x_ref[...]
    half = x.shape[1]
    lo = (x + pos_ref[:, :half]).astype(BF16)
    hi = (x + pos_ref[:, half:]).astype(BF16)
    h_lo = _dot(lo, w1_ref[:half, :])
    h_hi = _dot(hi, w1_ref[half:, :])
    nch = x.shape[0]
    hid = h_lo + pltpu.roll(h_hi, nch - 1, 0)
    act = (hid * jax.nn.sigmoid(hid)).astype(BF16)
    o_ref[...] = _dot(act, w2_ref[...]).astype(o_ref.dtype)


def _compress(xc, pos, w1, w2, layer):
    _, B, G, NCH, CW = xc.shape
    HC = w1.shape[-1]
    dh = w2.shape[-1]
    return pl.pallas_call(
        _compress_body,
        grid=(2, B, G),
        in_specs=[
            pl.BlockSpec((None, None, None, NCH, CW), lambda s, b, g: (s, b, g, 0, 0)),
            pl.BlockSpec((None, None, 1, 2 * CW), lambda s, b, g: (s, layer, 0, 0)),
            pl.BlockSpec((None, None, 2 * CW, HC), lambda s, b, g: (s, layer, 0, 0)),
            pl.BlockSpec((None, None, HC, dh), lambda s, b, g: (s, layer, 0, 0)),
        ],
        out_specs=pl.BlockSpec((None, None, None, NCH, dh), lambda s, b, g: (s, b, g, 0, 0)),
        out_shape=jax.ShapeDtypeStruct((2, B, G, NCH, dh), BF16),
        compiler_params=_params("parallel", "parallel", "parallel"),
        name="compress",
    )(xc, pos, w1, w2)


def _bias_lookup(n, thr_ref, tab_ref):
    vals = [jnp.full(n.shape, tab_ref[0, h], F32) for h in range(NSA_HEADS)]
    for k in range(1, REL_BUCKETS):
        above = n >= thr_ref[k]
        vals = [jnp.where(above, tab_ref[k, h], v) for h, v in enumerate(vals)]
    return vals


def _toeplitz_body(thr_ref, tab_ref, o_ref, *, rows, window):
    dd = pl.program_id(0)
    T = o_ref.shape[-1]
    for r0 in range(0, T, rows):
        a = lax.broadcasted_iota(jnp.int32, (rows, T), 0) + r0
        b = lax.broadcasted_iota(jnp.int32, (rows, T), 1)
        dist = dd * T + a - b
        keep = (dist >= 0) & (dist < WINDOW) if window else dist >= 0
        for h, val in enumerate(_bias_lookup(jnp.clip(dist, 0, REL_MAX_DIST), thr_ref, tab_ref)):
            o_ref[h, r0:r0 + rows, :] = jnp.where(keep, val * LOG2E, NEG)


def _cmpbias_body(thr_ref, tab_ref, o_ref, *, rows):
    i = pl.program_id(0)
    _, TQ, NC = o_ref.shape
    for r0 in range(0, TQ, rows):
        t = lax.broadcasted_iota(jnp.int32, (rows, NC), 0) + (i * TQ + r0)
        c = lax.broadcasted_iota(jnp.int32, (rows, NC), 1)
        n = jnp.clip(t - (c * CMP_STRIDE + (CMP_BLOCK - 1)), 0, REL_MAX_DIST)
        for h, val in enumerate(_bias_lookup(n, thr_ref, tab_ref)):
            o_ref[h, r0:r0 + rows, :] = val


def _bias_tables(thr, rel_table, S, ncp):
    T = ATT_TILE
    H = NSA_HEADS
    rows = 16
    smem = pl.BlockSpec(memory_space=pltpu.SMEM)

    def toeplitz(n_tiles, window, name):
        return pl.pallas_call(
            functools.partial(_toeplitz_body, rows=rows, window=window),
            grid=(n_tiles,),
            in_specs=[smem, smem],
            out_specs=pl.BlockSpec((None, H, T, T), lambda d: (d, 0, 0, 0)),
            out_shape=jax.ShapeDtypeStruct((n_tiles, H, T, T), F32),
            compiler_params=_params("parallel"),
            name=name,
        )(thr, rel_table)

    toep = toeplitz(N_BIAS_TILES, False, "bias_toeplitz")
    toep_win = toeplitz(N_WIN_TILES + 1, True, "bias_window")
    cmpb = pl.pallas_call(
        functools.partial(_cmpbias_body, rows=rows),
        grid=(S // T,),
        in_specs=[smem, smem],
        out_specs=pl.BlockSpec((H, T, ncp), lambda i: (0, i, 0)),
        out_shape=jax.ShapeDtypeStruct((H, S, ncp), F32),
        compiler_params=_params("parallel"),
        name="bias_cmp",
    )(thr, rel_table)
    return toep, toep_win, cmpb


def _nsa_body(q_ref, gate_ref, kc_ref, vc_ref, ks_ref, vs_ref, kw_ref, vw_ref, bc_ref, bts_ref, btw_ref,
              ovl_ref, o_ref, qc_ref, qa_ref, m_ref, acc_ref, oacc_ref, sa_ref, sb_ref, sc_ref, alpha_ref,
              gs_ref, lc_ref, pcb_ref, ps_ref,
              *, n_sel_blocks, n_top):
    group = pl.program_id(1)
    i = pl.program_id(2)
    HPG, TQ, DH = qc_ref.shape
    TK = TQ
    SB = NSA_SUB
    NCP = kc_ref.shape[0]
    t0 = i * TQ
    R = HPG * TQ
    subs = [(h, a0) for a0 in range(0, TQ, SB) for h in range(HPG)]

    for h in range(HPG):
        q = q_ref[:, h * DH:(h + 1) * DH]
        qc_ref[h] = (q * (DH ** -0.5)).astype(BF16)
        qa_ref[h, :, :DH] = (q * (DH ** -0.5 * LOG2E)).astype(BF16)
        qa_ref[h, :, DH:] = jnp.zeros((TQ, DH), BF16)

    gates = jax.nn.sigmoid(gate_ref[...])
    gs_ref[...] = jnp.where(group == 0, gates, pltpu.roll(gates, LANES - 3 * HPG, 1))

    def gate_col(a0, col):
        return gs_ref[a0:a0 + SB, col:col + 1]

    lc_all = _dot_nt(qc_ref[...].reshape(R, DH), kc_ref[...])
    lc_ref[...] = (lc_all.reshape(HPG, TQ, NCP) + bc_ref[...]).reshape(R, NCP)
    c_end = lax.broadcasted_iota(jnp.int32, (SB, NCP), 1) * CMP_STRIDE + (CMP_BLOCK - 1)
    r_c = lax.broadcasted_iota(jnp.int32, (SB, NCP), 0)
    for a0 in range(0, TQ, SB):
        mc = c_end <= (t0 + a0 + r_c)
        p_heads = None
        for h in range(HPG):
            rows = slice(h * TQ + a0, h * TQ + a0 + SB)
            lc = jnp.where(mc, lc_ref[rows, :], NEG)
            pc = jnp.where(mc, jnp.exp(lc - jnp.max(lc, axis=-1, keepdims=True)), 0.0)
            den = jnp.sum(pc, axis=-1, keepdims=True)
            pc = pc * jnp.where(den > 0.0, 1.0 / den, 0.0)
            pcb_ref[rows, :] = pc.astype(BF16)
            p_heads = pc if p_heads is None else p_heads + pc
        ps_ref[a0:a0 + SB, :] = p_heads
    o_cmp = _dot(pcb_ref[...], vc_ref[...])
    for h in range(HPG):
        oacc_ref[h] = gs_ref[:, 3 * h:3 * h + 1] * o_cmp[h * TQ:(h + 1) * TQ]

    def reset():
        m_ref[...] = jnp.full(m_ref.shape, NEG, F32)
        acc_ref[...] = jnp.zeros(acc_ref.shape, F32)

    def logits(k_ref, j, bias_ref, bias_tile, s_ref):
        k = k_ref[pl.ds(pl.multiple_of(j * TK, TK), TK), :]
        qk = _dot_nt(qa_ref[...].reshape(R, 2 * DH), k)
        s_ref[...] = (qk.reshape(HPG, TQ, TK) + bias_ref[bias_tile]).reshape(R, TK)

    def update_steps(v_ref, j, s_ref):
        chunks = [slice(r0, r0 + SB) for r0 in range(0, R, SB)]

        def pass1(rows):
            m_old = m_ref[rows, :]
            m_new = jnp.maximum(m_old, jnp.max(s_ref[rows, :], axis=-1, keepdims=True))
            alpha_ref[rows, :] = jnp.exp2(m_old - m_new)
            m_ref[rows, :] = m_new

        def pass2(rows):
            v = v_ref[pl.ds(pl.multiple_of(j * TK, TK), TK), :]
            m_new = m_ref[rows, :]
            p = jnp.exp2(s_ref[rows, :] - jnp.concatenate([m_new] * (TK // LANES), axis=1))
            acc_ref[rows, :] = alpha_ref[rows, :] * acc_ref[rows, :] + _dot(p.astype(BF16), v)

        return ([functools.partial(pass1, rows) for rows in chunks]
                + [functools.partial(pass2, rows) for rows in chunks])

    def update(v_ref, j, s_ref):
        for piece in update_steps(v_ref, j, s_ref):
            piece()

    def finalize(gate_off):
        for h, a0 in subs:
            acc = acc_ref[h * TQ + a0:h * TQ + a0 + SB, :]
            o = (acc * (1.0 / pltpu.roll(acc, DH, 1)))[:, :DH]
            oacc_ref[h, a0:a0 + SB, :] += gate_col(a0, 3 * h + gate_off) * o

    reset()
    buf_a, buf_b, buf_c = sa_ref, sb_ref, sc_ref
    win_pieces = []
    for n, buf in enumerate((buf_a, buf_b, buf_c)):
        dd = N_WIN_TILES - 1 - n
        j = jnp.maximum(i - dd, 0)
        logits(kw_ref, j, btw_ref, jnp.where(i < dd, N_WIN_TILES, dd), buf)
        win_pieces += update_steps(vw_ref, j, buf)

    imp_t = _dot_nt(ovl_ref[...], ps_ref[...], precision=lax.Precision.HIGHEST)
    s_io = lax.broadcasted_iota(jnp.int32, (DH, TQ), 0)
    jcur = (t0 + lax.broadcasted_iota(jnp.int32, (DH, TQ), 1)) >> SEL_SHIFT
    forced = (s_io == 0) | (s_io == jcur) | (s_io == jcur - 1)
    score = jnp.where(forced, 1e6, jnp.where(s_io <= jcur, imp_t, -1e6))
    sub8 = lax.broadcasted_iota(jnp.int32, (8, TQ), 0)
    cnt = [jnp.zeros((8, TQ), jnp.int32) for _ in range(DH // 8)]
    emitted = 0
    for sp in range(n_sel_blocks):
        row = score[sp:sp + 1, :]
        for g in range(DH // 8):
            blk = score[8 * g:8 * g + 8, :]
            if 8 * g > sp:
                beats = row >= blk
            elif 8 * g + 7 <= sp:
                beats = row > blk
            else:
                beats = (row > blk) | ((row == blk) & (sub8 > sp - 8 * g))
            cnt[g] = cnt[g] + jnp.where(beats, 1, 0)
        due = (sp + 1) * len(win_pieces) // n_sel_blocks
        for piece in win_pieces[emitted:due]:
            piece()
        emitted = due
    finalize(2)
    rank = jnp.concatenate(cnt, axis=0)
    drop_t = jnp.where((rank < n_top) & (s_io < n_sel_blocks), 0.0, NEG)
    drop = jnp.concatenate([jnp.zeros((DH, TQ), F32), drop_t], axis=0).T.astype(BF16)
    for h in range(HPG):
        qa_ref[h, :, DH:] = drop[:, DH:]

    def sel_logits(j, buf):
        logits(ks_ref, j, bts_ref, jnp.minimum(i - j, N_BIAS_TILES - 1), buf)

    reset()
    n_pairs = i // 2
    sel_logits(0, buf_a)

    def step(pair, carry):
        j = 2 * pair
        sel_logits(j + 1, buf_b)
        update(vs_ref, j, buf_a)
        sel_logits(j + 2, buf_a)
        update(vs_ref, j + 1, buf_b)
        return carry

    lax.fori_loop(0, n_pairs, step, 0)
    j_tail = 2 * n_pairs

    @pl.when(j_tail == i)
    def _():
        update(vs_ref, i, buf_a)

    @pl.when(j_tail < i)
    def _():
        sel_logits(i, buf_b)
        update(vs_ref, j_tail, buf_a)
        update(vs_ref, i, buf_b)

    finalize(1)

    for h in range(HPG):
        o_ref[:, h * DH:(h + 1) * DH] = oacc_ref[h].astype(o_ref.dtype)


def _nsa(proj3, kvc, ksw, vsw, bias_cmp, bias_toep, bias_win, ovl_t):
    B, S, _ = proj3.shape
    G, HPG, DH = NSA_GROUPS, NSA_HPG, NSA_DH
    GW = HPG * DH
    TQ = ATT_TILE
    NCP = kvc.shape[3]
    ns = S // SEL_BLOCK
    assert S % TQ == 0 and (HPG * TQ) % NSA_SUB == 0 and WINDOW % TQ == 0
    assert 2 * DH == LANES and ns <= DH and ovl_t.shape == (DH, NCP)
    assert C_QA % GW == 0 and C_GA % LANES == 0 and 3 * NSA_HEADS <= LANES

    def seq_spec(idx):
        return pl.BlockSpec((None, None, None, S, 2 * DH), lambda b, g, i: (idx, b, g, 0, 0))

    def cmp_spec(idx):
        return pl.BlockSpec((None, None, None, NCP, DH), lambda b, g, i: (idx, b, g, 0, 0))

    return pl.pallas_call(
        functools.partial(_nsa_body, n_sel_blocks=ns, n_top=min(SEL_TOPN, ns)),
        grid=(B, G, S // TQ),
        in_specs=[
            pl.BlockSpec((None, TQ, GW), lambda b, g, i: (b, i, C_QA // GW + g)),
            pl.BlockSpec((None, TQ, LANES), lambda b, g, i: (b, i, C_GA // LANES)),
            cmp_spec(0), cmp_spec(1),
            seq_spec(0), seq_spec(0), seq_spec(1), seq_spec(1),
            pl.BlockSpec((HPG, TQ, NCP), lambda b, g, i: (g, i, 0)),
            pl.BlockSpec((N_BIAS_TILES, HPG, TQ, TQ), lambda b, g, i: (0, g, 0, 0)),
            pl.BlockSpec((N_WIN_TILES + 1, HPG, TQ, TQ), lambda b, g, i: (0, g, 0, 0)),
            pl.BlockSpec((DH, NCP), lambda b, g, i: (0, 0)),
        ],
        out_specs=pl.BlockSpec((None, TQ, GW), lambda b, g, i: (b, i, g)),
        out_shape=jax.ShapeDtypeStruct((B, S, G * GW), BF16),
        scratch_shapes=[
            pltpu.VMEM((HPG, TQ, DH), BF16),
            pltpu.VMEM((HPG, TQ, 2 * DH), BF16),
            pltpu.VMEM((HPG * TQ, LANES), F32),
            pltpu.VMEM((HPG * TQ, 2 * DH), F32),
            pltpu.VMEM((HPG, TQ, DH), F32),
            pltpu.VMEM((HPG * TQ, TQ), F32),
            pltpu.VMEM((HPG * TQ, TQ), F32),
            pltpu.VMEM((HPG * TQ, TQ), F32),
            pltpu.VMEM((HPG * TQ, LANES), F32),
            pltpu.VMEM((TQ, LANES), F32),
            pltpu.VMEM((HPG * TQ, NCP), F32),
            pltpu.VMEM((HPG * TQ, NCP), BF16),
            pltpu.VMEM((TQ, NCP), F32),
        ],
        compiler_params=_params("parallel", "parallel", "arbitrary"),
        name="nsa",
    )(proj3, proj3, kvc, kvc, ksw, vsw, ksw, vsw, bias_cmp, bias_toep, bias_win, ovl_t)


def _gla_body(q_ref, k_ref, v_ref, r_ref, al_ref, w2_ref, ab_ref, gn_ref, o_ref, st_ref, u_ref, sb_ref):
    @pl.when(pl.program_id(1) == 0)
    def _():
        st_ref[...] = jnp.zeros_like(st_ref)

    C = GLA_CHUNK
    RB = q_ref.shape[0]
    n_chunks = RB // C
    pre = _dot(al_ref[...].astype(BF16), w2_ref[...]) + ab_ref[...]
    la = (jnp.minimum(pre, 0.0) - jnp.log(1.0 + jnp.exp(-jnp.abs(pre)))) * (1.0 / GLA_TAU)

    la_hi = la.astype(BF16)
    rest = la - la_hi.astype(F32)
    la_mid = rest.astype(BF16)
    la_lo = (rest - la_mid.astype(F32)).astype(BF16)
    r_io = lax.broadcasted_iota(jnp.int32, (C, 3 * C), 0)
    c_io = lax.broadcasted_iota(jnp.int32, (C, 3 * C), 1)
    tri3 = ((c_io & (C - 1)) <= r_io).astype(BF16)
    b_parts, bl_parts = [], []
    for c in range(n_chunks):
        rows = slice(c * C, (c + 1) * C)
        b_c = _dot(tri3, jnp.concatenate([la_hi[rows], la_mid[rows], la_lo[rows]], axis=0))
        b_parts.append(b_c)
        bl_parts.append(jnp.broadcast_to(b_c[C - 1:C, :], b_c.shape))
    b = jnp.concatenate(b_parts, axis=0)
    b_last = jnp.concatenate(bl_parts, axis=0)

    k = k_ref[...]
    q_dec = (q_ref[...] * (jnp.exp(b) * (GLA_DK ** -0.5))).astype(BF16)
    k_intra = (k * jnp.exp(-b)).astype(BF16)
    k_state = (k * jnp.exp(b_last - b)).astype(BF16)

    rr = lax.broadcasted_iota(jnp.int32, (RB, RB), 0)
    cc = lax.broadcasted_iota(jnp.int32, (RB, RB), 1)
    same_chunk_causal = (cc <= rr) & ((rr & -C) == (cc & -C))
    heads = [(h, slice(h * GLA_DK, (h + 1) * GLA_DK), slice(h * GLA_DV, (h + 1) * GLA_DV))
             for h in range(GLA_HEADS)]
    chunks = [(c, slice(c * C, (c + 1) * C)) for c in range(n_chunks)]
    for h, kc, vc in heads:
        v = v_ref[:, vc].astype(BF16)
        for c, rows in chunks:
            u_ref[h, c] = _dot_tn(k_state[rows, kc], v[rows])
    for h, kc, vc in heads:
        st = st_ref[h]
        for c, rows in chunks:
            sb_ref[h, c] = st.astype(BF16)
            decay = jnp.exp(b[(c + 1) * C - 8:(c + 1) * C, kc].T[:, 7:8])
            st = st * decay + u_ref[h, c]
        st_ref[h] = st
    for h, kc, vc in heads:
        v = v_ref[:, vc].astype(BF16)
        a = jnp.where(same_chunk_causal, _dot_nt(q_dec[:, kc], k_intra[:, kc]), 0.0)
        o_intra = _dot(a.astype(BF16), v)
        o_inter = jnp.concatenate([_dot(q_dec[rows, kc], sb_ref[h, c]) for c, rows in chunks], axis=0)
        o = _rms(o_intra + o_inter, gn_ref[:, vc])
        r = r_ref[:, vc]
        o_ref[:, vc] = (o * (r * jax.nn.sigmoid(r))).astype(o_ref.dtype)


def _gla(proj3, w2, ab, gn, layer):
    B, S, _ = proj3.shape
    RB = GLA_STEP
    HK = GLA_HEADS * GLA_DK
    HV = GLA_HEADS * GLA_DV
    assert S % RB == 0 and RB % GLA_CHUNK == 0

    def col(width, offset):
        assert offset % width == 0
        return pl.BlockSpec((None, RB, width), lambda b, s: (b, s, offset // width))

    return pl.pallas_call(
        _gla_body,
        grid=(B, S // RB),
        in_specs=[
            col(HK, C_QB), col(HK, C_KB), col(HV, C_VB), col(HV, C_RB), col(LANES, C_AL),
            pl.BlockSpec((None, LANES, HK), lambda b, s: (layer, 0, 0)),
            pl.BlockSpec((None, 1, HK), lambda b, s: (layer, 0, 0)),
            pl.BlockSpec((None, 1, HV), lambda b, s: (layer, 0, 0)),
        ],
        out_specs=pl.BlockSpec((None, RB, HV), lambda b, s: (b, s, 0)),
        out_shape=jax.ShapeDtypeStruct((B, S, HV), BF16),
        scratch_shapes=[pltpu.VMEM((GLA_HEADS, GLA_DK, GLA_DV), F32),
                        pltpu.VMEM((GLA_HEADS, RB // GLA_CHUNK, GLA_DK, GLA_DV), F32),
                        pltpu.VMEM((GLA_HEADS, RB // GLA_CHUNK, GLA_DK, GLA_DV), BF16)],
        compiler_params=_params("parallel", "arbitrary"),
        name="gla",
    )(proj3, proj3, proj3, proj3, proj3, w2, ab, gn)


def _merge_body(x_ref, oa_ref, ob_ref, gm_ref, wa_ref, wb_ref, wo_ref, o_ref):
    D = x_ref.shape[1]
    gm = gm_ref[...]
    y = (jax.nn.sigmoid(gm[:, :D]) * _dot(oa_ref[...], wa_ref[...])
         + jax.nn.sigmoid(gm[:, D:]) * _dot(ob_ref[...], wb_ref[...]))
    o_ref[...] = x_ref[...] + _dot(y.astype(BF16), wo_ref[...])


def _merge(x, o_a, o_b, proj, wa, wb, wo, layer, tm=512):
    T, D = x.shape
    DA = o_a.shape[1]
    DB = o_b.shape[1]
    assert T % tm == 0 and C_GM == 0
    return pl.pallas_call(
        _merge_body,
        grid=(T // tm,),
        in_specs=[
            pl.BlockSpec((tm, D), lambda i: (i, 0)),
            pl.BlockSpec((tm, DA), lambda i: (i, 0)),
            pl.BlockSpec((tm, DB), lambda i: (i, 0)),
            pl.BlockSpec((tm, 2 * D), lambda i: (i, 0)),
            pl.BlockSpec((None, DA, D), lambda i: (layer, 0, 0)),
            pl.BlockSpec((None, DB, D), lambda i: (layer, 0, 0)),
            pl.BlockSpec((None, D, D), lambda i: (layer, 0, 0)),
        ],
        out_specs=pl.BlockSpec((tm, D), lambda i: (i, 0)),
        out_shape=jax.ShapeDtypeStruct((T, D), F32),
        compiler_params=_params("parallel"),
        name="merge",
    )(x, o_a, o_b, proj, wa, wb, wo)


def _rel_bucket(dist):
    n = jnp.maximum(dist, 0)
    exact = REL_BUCKETS // 2
    nf = jnp.maximum(n, 1).astype(jnp.float32)
    log_b = exact + (jnp.log(nf / exact) / math.log(REL_MAX_DIST / exact)
                     * (REL_BUCKETS - exact)).astype(jnp.int32)
    return jnp.where(n < exact, n, jnp.minimum(log_b, REL_BUCKETS - 1))


def _regroup_w_in(w_in):
    widths = (NSA_HEADS * NSA_DH, 6 * NSA_GROUPS * NSA_DH, 3 * NSA_HEADS, GLA_HEADS * GLA_DK,
              GLA_HEADS * GLA_DK, GLA_HEADS * GLA_DV, GLA_RANK, GLA_HEADS * GLA_DV, 2 * D_MODEL)
    offs = np.concatenate([[0], np.cumsum(widths)])
    q_a, kv_a, g_a, q_b, k_b, v_b, a_lr, r_b, g_m = (w_in[..., offs[n]:offs[n + 1]] for n in range(9))

    def pad(w):
        return jnp.pad(w, ((0, 0), (0, 0), (0, LANES - w.shape[-1])))

    out = jnp.concatenate([g_m, v_b, r_b, q_a, q_b, k_b, kv_a, pad(g_a), pad(a_lr)], axis=-1)
    assert out.shape[-1] == N_PROJ
    return out.astype(BF16)


def _overlap_t(ncp, nsp, nc, ns):
    c = np.arange(ncp)[None, :] * CMP_STRIDE
    s = np.arange(nsp)[:, None] * SEL_BLOCK
    ov = (c < s + SEL_BLOCK) & (c + CMP_BLOCK > s) & (np.arange(ncp)[None, :] < nc) & (np.arange(nsp)[:, None] < ns)
    return jnp.asarray(ov.astype(np.float32))


def kernel(x, rel_table, ffn1_norm, ffn1_w_gate, ffn1_w_up, ffn1_w_down, mix_norm, w_in, cmp_pos_k, cmp_pos_v, cmp_k_w1, cmp_k_w2, cmp_v_w1, cmp_v_w2, gla_a_w2, gla_a_b, gla_out_norm, w_branch_nsa, w_branch_gla, w_out, ffn2_norm, ffn2_w_gate, ffn2_w_up, ffn2_w_down, final_norm):
    B, S, D = x.shape
    L = w_in.shape[0]
    T = B * S
    G, HPG, DH = NSA_GROUPS, NSA_HPG, NSA_DH
    nch = S // CMP_STRIDE
    nc = (S - CMP_BLOCK) // CMP_STRIDE + 1
    ns = S // SEL_BLOCK
    assert D == D_MODEL and nc == nch - 1

    w1g, w1u, w1d = ffn1_w_gate.astype(BF16), ffn1_w_up.astype(BF16), ffn1_w_down.astype(BF16)
    w2g, w2u, w2d = ffn2_w_gate.astype(BF16), ffn2_w_up.astype(BF16), ffn2_w_down.astype(BF16)
    w_proj = _regroup_w_in(w_in)
    wa, wb, wo = w_branch_nsa.astype(BF16), w_branch_gla.astype(BF16), w_out.astype(BF16)
    cmp_pos = jnp.stack([cmp_pos_k, cmp_pos_v]).reshape(2, L, 1, CMP_BLOCK * DH)
    cmp_w1 = jnp.stack([cmp_k_w1, cmp_v_w1]).astype(BF16)
    cmp_w2 = jnp.stack([cmp_k_w2, cmp_v_w2]).astype(BF16)
    gla_w2 = jnp.pad(gla_a_w2, ((0, 0), (0, LANES - GLA_RANK), (0, 0))).astype(BF16)
    gla_b = gla_a_b.reshape(L, 1, -1)
    gla_gn = gla_out_norm.reshape(L, 1, -1)
    n1 = ffn1_norm.reshape(L, 1, D)
    n2 = ffn2_norm.reshape(L, 1, D)
    nm = mix_norm.reshape(L, 1, D)

    buckets = _rel_bucket(jnp.arange(REL_MAX_DIST + 1, dtype=jnp.int32))
    thr = jnp.searchsorted(buckets, jnp.arange(REL_BUCKETS, dtype=jnp.int32), side="left").astype(jnp.int32)
    bias_toep, bias_win, bias_cmp = _bias_tables(thr, rel_table, S, nch)
    ovl_t = _overlap_t(nch, DH, nc, ns)

    xf = x.reshape(T, D)
    for l in range(L):
        xf = _ffn(xf, n1, w1g, w1u, w1d, l)
        proj = _proj(xf, nm, w_proj, l)
        proj3 = proj.reshape(B, S, N_PROJ)

        xc, ksw, vsw = _kvprep(proj3)
        kvc = _compress(xc, cmp_pos, cmp_w1, cmp_w2, l)
        o_a = _nsa(proj3, kvc, ksw, vsw, bias_cmp, bias_toep, bias_win, ovl_t)
        o_a = o_a.reshape(T, NSA_HEADS * DH)

        o_b = _gla(proj3, gla_w2, gla_b, gla_gn, l).reshape(T, GLA_HEADS * GLA_DV)

        xf = _merge(xf, o_a, o_b, proj, wa, wb, wo, l)
        xf = _ffn(xf, n2, w2g, w2u, w2d, l,
                  final_g=final_norm.reshape(1, D) if l == L - 1 else None)
    return xf.reshape(B, S, D)
```

```python
import functools
import math

import numpy as np
import jax
import jax.numpy as jnp
from jax import lax
from jax.experimental import pallas as pl
from jax.experimental.pallas import tpu as pltpu

F32 = jnp.float32
BF16 = jnp.bfloat16

NSA_HEADS = 8
NSA_GROUPS = 2
NSA_HPG = NSA_HEADS // NSA_GROUPS
NSA_DH = 64
CMP_BLOCK = 32
CMP_STRIDE = 16
SEL_BLOCK = 64
SEL_SHIFT = 6
SEL_TOPN = 16
WINDOW = 512
GLA_HEADS = 4
GLA_DK = 128
GLA_DV = 256
GLA_RANK = 16
GLA_TAU = 16.0
GLA_CHUNK = 64
REL_BUCKETS = 32
REL_MAX_DIST = 1024
EPS = 1e-6
NEG = -1e30
LOG2E = math.log2(math.e)

LANES = 128
VMEM_LIMIT = 56 * 1024 * 1024

ATT_TILE = 256
N_BIAS_TILES = REL_MAX_DIST // ATT_TILE + 2
N_WIN_TILES = WINDOW // ATT_TILE + 1
NSA_SUB = 128
GLA_STEP = 512

D_MODEL = 1024
C_GM = 0
C_VB = 2048
C_RB = 3072
C_QA = 4096
C_QB = 4608
C_KB = 5120
C_KV = 5632
C_GA = 6400
C_AL = 6528
N_PROJ = 6656


def _dot(a, b, precision=None):
    return lax.dot_general(a, b, (((1,), (0,)), ((), ())), precision=precision,
                           preferred_element_type=F32)


def _dot_nt(a, b, precision=None):
    return lax.dot_general(a, b, (((1,), (1,)), ((), ())), precision=precision,
                           preferred_element_type=F32)


def _dot_tn(a, b, precision=None):
    return lax.dot_general(a, b, (((0,), (0,)), ((), ())), precision=precision,
                           preferred_element_type=F32)


def _rms(x, g):
    return x * lax.rsqrt(jnp.mean(x * x, axis=-1, keepdims=True) + EPS) * g


def _params(*sem):
    return pltpu.CompilerParams(dimension_semantics=sem, vmem_limit_bytes=VMEM_LIMIT)


def _ffn_body(x_ref, g_ref, wg_ref, wu_ref, wd_ref, *rest, final, fc):
    if final:
        fg_ref, o_ref = rest
    else:
        (o_ref,) = rest
    x = x_ref[...]
    h = _rms(x, g_ref[...]).astype(BF16)
    acc = None
    for f0 in range(0, wg_ref.shape[1], fc):
        gate = _dot(h, wg_ref[:, f0:f0 + fc])
        up = _dot(h, wu_ref[:, f0:f0 + fc])
        act = (gate * jax.nn.sigmoid(gate) * up).astype(BF16)
        down = _dot(act, wd_ref[f0:f0 + fc, :])
        acc = down if acc is None else acc + down
    y = x + 0.5 * acc
    if final:
        y = _rms(y, fg_ref[...])
    o_ref[...] = y


def _ffn(x, g, wg, wu, wd, layer, final_g=None, tm=1024, fc=704):
    T, D = x.shape
    F = wg.shape[-1]
    assert T % tm == 0 and F % fc == 0
    final = final_g is not None
    resident = pl.Buffered(1)
    in_specs = [
        pl.BlockSpec((tm, D), lambda i: (i, 0)),
        pl.BlockSpec((None, 1, D), lambda i: (layer, 0, 0)),
        pl.BlockSpec((None, D, F), lambda i: (layer, 0, 0), pipeline_mode=resident),
        pl.BlockSpec((None, D, F), lambda i: (layer, 0, 0), pipeline_mode=resident),
        pl.BlockSpec((None, F, D), lambda i: (layer, 0, 0), pipeline_mode=resident),
    ]
    args = [x, g, wg, wu, wd]
    if final:
        in_specs.append(pl.BlockSpec((1, D), lambda i: (0, 0)))
        args.append(final_g)
    return pl.pallas_call(
        functools.partial(_ffn_body, final=final, fc=fc),
        grid=(T // tm,),
        in_specs=in_specs,
        out_specs=pl.BlockSpec((tm, D), lambda i: (i, 0)),
        out_shape=jax.ShapeDtypeStruct((T, D), F32),
        compiler_params=_params("parallel"),
        name="ffn",
    )(*args)


def _proj_body(x_ref, g_ref, w_ref, o_ref, *, nc):
    h = _rms(x_ref[...], g_ref[...]).astype(BF16)
    for n0 in range(0, w_ref.shape[1], nc):
        o_ref[:, n0:n0 + nc] = _dot(h, w_ref[:, n0:n0 + nc])


def _proj(x, g, w, layer, tm=512, nc=1664):
    T, D = x.shape
    N = w.shape[-1]
    assert T % tm == 0 and N % nc == 0
    return pl.pallas_call(
        functools.partial(_proj_body, nc=nc),
        grid=(T // tm,),
        in_specs=[
            pl.BlockSpec((tm, D), lambda i: (i, 0)),
            pl.BlockSpec((None, 1, D), lambda i: (layer, 0, 0)),
            pl.BlockSpec((None, D, N), lambda i: (layer, 0, 0), pipeline_mode=pl.Buffered(1)),
        ],
        out_specs=pl.BlockSpec((tm, N), lambda i: (i, 0)),
        out_shape=jax.ShapeDtypeStruct((T, N), F32),
        compiler_params=_params("parallel"),
        name="proj",
    )(x, g, w)


def _kvprep_body(kc_ref, vc_ref, ks_ref, vs_ref, kw_ref, vw_ref, xc_ref, ksw_ref, vsw_ref):
    TS = ks_ref.shape[0]
    DH = NSA_DH
    t = pl.program_id(1) * TS + lax.broadcasted_iota(jnp.int32, (TS, DH), 0)
    onehot = ((t >> SEL_SHIFT) == lax.broadcasted_iota(jnp.int32, (TS, DH), 1)).astype(BF16)
    ones = jnp.ones((TS, DH), BF16)
    zeros = jnp.zeros((TS, DH), BF16)
    for g in range(NSA_GROUPS):
        cols = slice(g * DH, (g + 1) * DH)
        ksw_ref[0, g] = jnp.concatenate([ks_ref[:, cols].astype(BF16), onehot], axis=1)
        ksw_ref[1, g] = jnp.concatenate([kw_ref[:, cols].astype(BF16), zeros], axis=1)
        vsw_ref[0, g] = jnp.concatenate([vs_ref[:, cols].astype(BF16), ones], axis=1)
        vsw_ref[1, g] = jnp.concatenate([vw_ref[:, cols].astype(BF16), ones], axis=1)
    for s, src in enumerate((kc_ref, vc_ref)):
        for l in range(CMP_STRIDE):
            x = src[pl.ds(l, TS // CMP_STRIDE, stride=CMP_STRIDE), :]
            for g in range(NSA_GROUPS):
                xc_ref[s, g, :, l * DH:(l + 1) * DH] = x[:, g * DH:(g + 1) * DH]


def _kvprep(proj3, ts=512):
    B, S, _ = proj3.shape
    G, DH = NSA_GROUPS, NSA_DH
    GW = G * DH
    assert S % ts == 0 and ts % (8 * CMP_STRIDE) == 0 and C_KV % GW == 0

    def col(n):
        return pl.BlockSpec((None, ts, GW), lambda b, s: (b, s, C_KV // GW + n))

    def out(width, rows):
        return pl.BlockSpec((2, None, G, rows, width), lambda b, s: (0, b, 0, s, 0))

    nch = S // CMP_STRIDE
    return pl.pallas_call(
        _kvprep_body,
        grid=(B, S // ts),
        in_specs=[col(n) for n in range(6)],
        out_specs=[out(CMP_STRIDE * DH, ts // CMP_STRIDE), out(2 * DH, ts), out(2 * DH, ts)],
        out_shape=[jax.ShapeDtypeStruct((2, B, G, nch, CMP_STRIDE * DH), F32),
                   jax.ShapeDtypeStruct((2, B, G, S, 2 * DH), BF16),
                   jax.ShapeDtypeStruct((2, B, G, S, 2 * DH), BF16)],
        compiler_params=_params("parallel", "parallel"),
        name="kvprep",
    )(*([proj3] * 6))


def _compress_body(x_ref, pos_ref, w1_ref, w2_ref, o_ref):
    x = x_ref[...]
    half = x.shape[1]
    lo = (x + pos_ref[:, :half]).astype(BF16)
    hi = (x + pos_ref[:, half:]).astype(BF16)
    h_lo = _dot(lo, w1_ref[:half, :])
    h_hi = _dot(hi, w1_ref[half:, :])
    nch = x.shape[0]
    hid = h_lo + pltpu.roll(h_hi, nch - 1, 0)
    act = (hid * jax.nn.sigmoid(hid)).astype(BF16)
    o_ref[...] = _dot(act, w2_ref[...]).astype(o_ref.dtype)


def _compress(xc, pos, w1, w2, layer):
    _, B, G, NCH, CW = xc.shape
    HC = w1.shape[-1]
    dh = w2.shape[-1]
    return pl.pallas_call(
        _compress_body,
        grid=(2, B, G),
        in_specs=[
            pl.BlockSpec((None, None, None, NCH, CW), lambda s, b, g: (s, b, g, 0, 0)),
            pl.BlockSpec((None, None, 1, 2 * CW), lambda s, b, g: (s, layer, 0, 0)),
            pl.BlockSpec((None, None, 2 * CW, HC), lambda s, b, g: (s, layer, 0, 0)),
            pl.BlockSpec((None, None, HC, dh), lambda s, b, g: (s, layer, 0, 0)),
        ],
        out_specs=pl.BlockSpec((None, None, None, NCH, dh), lambda s, b, g: (s, b, g, 0, 0)),
        out_shape=jax.ShapeDtypeStruct((2, B, G, NCH, dh), BF16),
        compiler_params=_params("parallel", "parallel", "parallel"),
        name="compress",
    )(xc, pos, w1, w2)


def _bias_lookup(n, thr_ref, tab_ref):
    vals = [jnp.full(n.shape, tab_ref[0, h], F32) for h in range(NSA_HEADS)]
    for k in range(1, REL_BUCKETS):
        above = n >= thr_ref[k]
        vals = [jnp.where(above, tab_ref[k, h], v) for h, v in enumerate(vals)]
    return vals


def _toeplitz_body(thr_ref, tab_ref, o_ref, *, rows, window):
    dd = pl.program_id(0)
    T = o_ref.shape[-1]
    for r0 in range(0, T, rows):
        a = lax.broadcasted_iota(jnp.int32, (rows, T), 0) + r0
        b = lax.broadcasted_iota(jnp.int32, (rows, T), 1)
        dist = dd * T + a - b
        keep = (dist >= 0) & (dist < WINDOW) if window else dist >= 0
        for h, val in enumerate(_bias_lookup(jnp.clip(dist, 0, REL_MAX_DIST), thr_ref, tab_ref)):
            o_ref[h, r0:r0 + rows, :] = jnp.where(keep, val * LOG2E, NEG)


def _cmpbias_body(thr_ref, tab_ref, o_ref, *, rows):
    i = pl.program_id(0)
    _, TQ, NC = o_ref.shape
    for r0 in range(0, TQ, rows):
        t = lax.broadcasted_iota(jnp.int32, (rows, NC), 0) + (i * TQ + r0)
        c = lax.broadcasted_iota(jnp.int32, (rows, NC), 1)
        n = jnp.clip(t - (c * CMP_STRIDE + (CMP_BLOCK - 1)), 0, REL_MAX_DIST)
        for h, val in enumerate(_bias_lookup(n, thr_ref, tab_ref)):
            o_ref[h, r0:r0 + rows, :] = val


def _bias_tables(thr, rel_table, S, ncp):
    T = ATT_TILE
    H = NSA_HEADS
    rows = 16
    smem = pl.BlockSpec(memory_space=pltpu.SMEM)

    def toeplitz(n_tiles, window, name):
        return pl.pallas_call(
            functools.partial(_toeplitz_body, rows=rows, window=window),
            grid=(n_tiles,),
            in_specs=[smem, smem],
            out_specs=pl.BlockSpec((None, H, T, T), lambda d: (d, 0, 0, 0)),
            out_shape=jax.ShapeDtypeStruct((n_tiles, H, T, T), F32),
            compiler_params=_params("parallel"),
            name=name,
        )(thr, rel_table)

    toep = toeplitz(N_BIAS_TILES, False, "bias_toeplitz")
    toep_win = toeplitz(N_WIN_TILES + 1, True, "bias_window")
    cmpb = pl.pallas_call(
        functools.partial(_cmpbias_body, rows=rows),
        grid=(S // T,),
        in_specs=[smem, smem],
        out_specs=pl.BlockSpec((H, T, ncp), lambda i: (0, i, 0)),
        out_shape=jax.ShapeDtypeStruct((H, S, ncp), F32),
        compiler_params=_params("parallel"),
        name="bias_cmp",
    )(thr, rel_table)
    return toep, toep_win, cmpb


def _nsa_body(q_ref, gate_ref, kc_ref, vc_ref, ks_ref, vs_ref, kw_ref, vw_ref, bc_ref, bts_ref, btw_ref,
              ovl_ref, o_ref, qc_ref, qa_ref, m_ref, acc_ref, oacc_ref, sa_ref, sb_ref, sc_ref, alpha_ref,
              gs_ref, lc_ref, pcb_ref, ps_ref,
              *, n_sel_blocks, n_top):
    group = pl.program_id(1)
    i = pl.program_id(2)
    HPG, TQ, DH = qc_ref.shape
    TK = TQ
    SB = NSA_SUB
    NCP = kc_ref.shape[0]
    t0 = i * TQ
    R = HPG * TQ
    subs = [(h, a0) for a0 in range(0, TQ, SB) for h in range(HPG)]

    for h in range(HPG):
        q = q_ref[:, h * DH:(h + 1) * DH]
        qc_ref[h] = (q * (DH ** -0.5)).astype(BF16)
        qa_ref[h, :, :DH] = (q * (DH ** -0.5 * LOG2E)).astype(BF16)
        qa_ref[h, :, DH:] = jnp.zeros((TQ, DH), BF16)

    gates = jax.nn.sigmoid(gate_ref[...])
    gs_ref[...] = jnp.where(group == 0, gates, pltpu.roll(gates, LANES - 3 * HPG, 1))

    def gate_col(a0, col):
        return gs_ref[a0:a0 + SB, col:col + 1]

    lc_all = _dot_nt(qc_ref[...].reshape(R, DH), kc_ref[...])
    lc_ref[...] = (lc_all.reshape(HPG, TQ, NCP) + bc_ref[...]).reshape(R, NCP)
    c_end = lax.broadcasted_iota(jnp.int32, (SB, NCP), 1) * CMP_STRIDE + (CMP_BLOCK - 1)
    r_c = lax.broadcasted_iota(jnp.int32, (SB, NCP), 0)
    for a0 in range(0, TQ, SB):
        mc = c_end <= (t0 + a0 + r_c)
        p_heads = None
        for h in range(HPG):
            rows = slice(h * TQ + a0, h * TQ + a0 + SB)
            lc = jnp.where(mc, lc_ref[rows, :], NEG)
            pc = jnp.where(mc, jnp.exp(lc - jnp.max(lc, axis=-1, keepdims=True)), 0.0)
            den = jnp.sum(pc, axis=-1, keepdims=True)
            pc = pc * jnp.where(den > 0.0, 1.0 / den, 0.0)
            pcb_ref[rows, :] = pc.astype(BF16)
            p_heads = pc if p_heads is None else p_heads + pc
        ps_ref[a0:a0 + SB, :] = p_heads
    o_cmp = _dot(pcb_ref[...], vc_ref[...])
    for h in range(HPG):
        oacc_ref[h] = gs_ref[:, 3 * h:3 * h + 1] * o_cmp[h * TQ:(h + 1) * TQ]

    def reset():
        m_ref[...] = jnp.full(m_ref.shape, NEG, F32)
        acc_ref[...] = jnp.zeros(acc_ref.shape, F32)

    def logits(k_ref, j, bias_ref, bias_tile, s_ref):
        k = k_ref[pl.ds(pl.multiple_of(j * TK, TK), TK), :]
        qk = _dot_nt(qa_ref[...].reshape(R, 2 * DH), k)
        s_ref[...] = (qk.reshape(HPG, TQ, TK) + bias_ref[bias_tile]).reshape(R, TK)

    def update_steps(v_ref, j, s_ref):
        chunks = [slice(r0, r0 + SB) for r0 in range(0, R, SB)]

        def pass1(rows):
            m_old = m_ref[rows, :]
            m_new = jnp.maximum(m_old, jnp.max(s_ref[rows, :], axis=-1, keepdims=True))
            alpha_ref[rows, :] = jnp.exp2(m_old - m_new)
            m_ref[rows, :] = m_new

        def pass2(rows):
            v = v_ref[pl.ds(pl.multiple_of(j * TK, TK), TK), :]
            m_new = m_ref[rows, :]
            p = jnp.exp2(s_ref[rows, :] - jnp.concatenate([m_new] * (TK // LANES), axis=1))
            acc_ref[rows, :] = alpha_ref[rows, :] * acc_ref[rows, :] + _dot(p.astype(BF16), v)

        return ([functools.partial(pass1, rows) for rows in chunks]
                + [functools.partial(pass2, rows) for rows in chunks])

    def update(v_ref, j, s_ref):
        for piece in update_steps(v_ref, j, s_ref):
            piece()

    def finalize(gate_off):
        for h, a0 in subs:
            acc = acc_ref[h * TQ + a0:h * TQ + a0 + SB, :]
            o = (acc * (1.0 / pltpu.roll(acc, DH, 1)))[:, :DH]
            oacc_ref[h, a0:a0 + SB, :] += gate_col(a0, 3 * h + gate_off) * o

    reset()
    buf_a, buf_b, buf_c = sa_ref, sb_ref, sc_ref
    win_pieces = []
    for n, buf in enumerate((buf_a, buf_b, buf_c)):
        dd = N_WIN_TILES - 1 - n
        j = jnp.maximum(i - dd, 0)
        logits(kw_ref, j, btw_ref, jnp.where(i < dd, N_WIN_TILES, dd), buf)
        win_pieces += update_steps(vw_ref, j, buf)

    imp_t = _dot_nt(ovl_ref[...], ps_ref[...], precision=lax.Precision.HIGHEST)
    s_io = lax.broadcasted_iota(jnp.int32, (DH, TQ), 0)
    jcur = (t0 + lax.broadcasted_iota(jnp.int32, (DH, TQ), 1)) >> SEL_SHIFT
    forced = (s_io == 0) | (s_io == jcur) | (s_io == jcur - 1)
    score = jnp.where(forced, 1e6, jnp.where(s_io <= jcur, imp_t, -1e6))
    sub8 = lax.broadcasted_iota(jnp.int32, (8, TQ), 0)
    cnt = [jnp.zeros((8, TQ), jnp.int32) for _ in range(DH // 8)]
    emitted = 0
    for sp in range(n_sel_blocks):
        row = score[sp:sp + 1, :]
        for g in range(DH // 8):
            blk = score[8 * g:8 * g + 8, :]
            if 8 * g > sp:
                beats = row >= blk
            elif 8 * g + 7 <= sp:
                beats = row > blk
            else:
                beats = (row > blk) | ((row == blk) & (sub8 > sp - 8 * g))
            cnt[g] = cnt[g] + jnp.where(beats, 1, 0)
        due = (sp + 1) * len(win_pieces) // n_sel_blocks
        for piece in win_pieces[emitted:due]:
            piece()
        emitted = due
    finalize(2)
    rank = jnp.concatenate(cnt, axis=0)
    drop_t = jnp.where((rank < n_top) & (s_io < n_sel_blocks), 0.0, NEG)
    drop = jnp.concatenate([jnp.zeros((DH, TQ), F32), drop_t], axis=0).T.astype(BF16)
    for h in range(HPG):
        qa_ref[h, :, DH:] = drop[:, DH:]

    def sel_logits(j, buf):
        logits(ks_ref, j, bts_ref, jnp.minimum(i - j, N_BIAS_TILES - 1), buf)

    reset()
    sel_logits(0, buf_a)

    def pipeline(j, n):
        for t in range(0, n, 2):
            sel_logits(j + t + 1, buf_b)
            update(vs_ref, j + t, buf_a)
            sel_logits(j + t + 2, buf_a)
            update(vs_ref, j + t + 1, buf_b)

    unroll = 4
    n_long = i // unroll

    def long_step(n, carry):
        pipeline(unroll * n, unroll)
        return carry

    lax.fori_loop(0, n_long, long_step, 0)
    n_pairs = (i - unroll * n_long) // 2

    def pair_step(n, carry):
        pipeline(unroll * n_long + 2 * n, 2)
        return carry

    lax.fori_loop(0, n_pairs, pair_step, 0)
    j_tail = unroll * n_long + 2 * n_pairs

    @pl.when(j_tail == i)
    def _():
        update(vs_ref, i, buf_a)

    @pl.when(j_tail < i)
    def _():
        sel_logits(i, buf_b)
        update(vs_ref, j_tail, buf_a)
        update(vs_ref, i, buf_b)

    finalize(1)

    for h in range(HPG):
        o_ref[:, h * DH:(h + 1) * DH] = oacc_ref[h].astype(o_ref.dtype)


def _nsa(proj3, kvc, ksw, vsw, bias_cmp, bias_toep, bias_win, ovl_t):
    B, S, _ = proj3.shape
    G, HPG, DH = NSA_GROUPS, NSA_HPG, NSA_DH
    GW = HPG * DH
    TQ = ATT_TILE
    NCP = kvc.shape[3]
    ns = S // SEL_BLOCK
    assert S % TQ == 0 and (HPG * TQ) % NSA_SUB == 0 and WINDOW % TQ == 0
    assert 2 * DH == LANES and ns <= DH and ovl_t.shape == (DH, NCP)
    assert C_QA % GW == 0 and C_GA % LANES == 0 and 3 * NSA_HEADS <= LANES

    def seq_spec(idx):
        return pl.BlockSpec((None, None, None, S, 2 * DH), lambda b, g, i: (idx, b, g, 0, 0))

    def cmp_spec(idx):
        return pl.BlockSpec((None, None, None, NCP, DH), lambda b, g, i: (idx, b, g, 0, 0))

    return pl.pallas_call(
        functools.partial(_nsa_body, n_sel_blocks=ns, n_top=min(SEL_TOPN, ns)),
        grid=(B, G, S // TQ),
        in_specs=[
            pl.BlockSpec((None, TQ, GW), lambda b, g, i: (b, i, C_QA // GW + g)),
            pl.BlockSpec((None, TQ, LANES), lambda b, g, i: (b, i, C_GA // LANES)),
            cmp_spec(0), cmp_spec(1),
            seq_spec(0), seq_spec(0), seq_spec(1), seq_spec(1),
            pl.BlockSpec((HPG, TQ, NCP), lambda b, g, i: (g, i, 0)),
            pl.BlockSpec((N_BIAS_TILES, HPG, TQ, TQ), lambda b, g, i: (0, g, 0, 0)),
            pl.BlockSpec((N_WIN_TILES + 1, HPG, TQ, TQ), lambda b, g, i: (0, g, 0, 0)),
            pl.BlockSpec((DH, NCP), lambda b, g, i: (0, 0)),
        ],
        out_specs=pl.BlockSpec((None, TQ, GW), lambda b, g, i: (b, i, g)),
        out_shape=jax.ShapeDtypeStruct((B, S, G * GW), BF16),
        scratch_shapes=[
            pltpu.VMEM((HPG, TQ, DH), BF16),
            pltpu.VMEM((HPG, TQ, 2 * DH), BF16),
            pltpu.VMEM((HPG * TQ, LANES), F32),
            pltpu.VMEM((HPG * TQ, 2 * DH), F32),
            pltpu.VMEM((HPG, TQ, DH), F32),
            pltpu.VMEM((HPG * TQ, TQ), F32),
            pltpu.VMEM((HPG * TQ, TQ), F32),
            pltpu.VMEM((HPG * TQ, TQ), F32),
            pltpu.VMEM((HPG * TQ, LANES), F32),
            pltpu.VMEM((TQ, LANES), F32),
            pltpu.VMEM((HPG * TQ, NCP), F32),
            pltpu.VMEM((HPG * TQ, NCP), BF16),
            pltpu.VMEM((TQ, NCP), F32),
        ],
        compiler_params=_params("parallel", "parallel", "arbitrary"),
        name="nsa",
    )(proj3, proj3, kvc, kvc, ksw, vsw, ksw, vsw, bias_cmp, bias_toep, bias_win, ovl_t)


def _gla_body(q_ref, k_ref, v_ref, r_ref, al_ref, w2_ref, ab_ref, gn_ref, o_ref, st_ref, u_ref, sb_ref):
    @pl.when(pl.program_id(1) == 0)
    def _():
        st_ref[...] = jnp.zeros_like(st_ref)

    C = GLA_CHUNK
    RB = q_ref.shape[0]
    n_chunks = RB // C
    pre = _dot(al_ref[...].astype(BF16), w2_ref[...]) + ab_ref[...]
    la = (jnp.minimum(pre, 0.0) - jnp.log(1.0 + jnp.exp(-jnp.abs(pre)))) * (1.0 / GLA_TAU)

    la_hi = la.astype(BF16)
    rest = la - la_hi.astype(F32)
    la_mid = rest.astype(BF16)
    la_lo = (rest - la_mid.astype(F32)).astype(BF16)
    r_io = lax.broadcasted_iota(jnp.int32, (C, 3 * C), 0)
    c_io = lax.broadcasted_iota(jnp.int32, (C, 3 * C), 1)
    tri3 = ((c_io & (C - 1)) <= r_io).astype(BF16)
    b_parts, bl_parts = [], []
    for c in range(n_chunks):
        rows = slice(c * C, (c + 1) * C)
        b_c = _dot(tri3, jnp.concatenate([la_hi[rows], la_mid[rows], la_lo[rows]], axis=0))
        b_parts.append(b_c)
        bl_parts.append(jnp.broadcast_to(b_c[C - 1:C, :], b_c.shape))
    b = jnp.concatenate(b_parts, axis=0)
    b_last = jnp.concatenate(bl_parts, axis=0)

    k = k_ref[...]
    q_dec = (q_ref[...] * (jnp.exp(b) * (GLA_DK ** -0.5))).astype(BF16)
    k_intra = (k * jnp.exp(-b)).astype(BF16)
    k_state = (k * jnp.exp(b_last - b)).astype(BF16)

    rr = lax.broadcasted_iota(jnp.int32, (RB, RB), 0)
    cc = lax.broadcasted_iota(jnp.int32, (RB, RB), 1)
    same_chunk_causal = (cc <= rr) & ((rr & -C) == (cc & -C))
    heads = [(h, slice(h * GLA_DK, (h + 1) * GLA_DK), slice(h * GLA_DV, (h + 1) * GLA_DV))
             for h in range(GLA_HEADS)]
    chunks = [(c, slice(c * C, (c + 1) * C)) for c in range(n_chunks)]
    for h, kc, vc in heads:
        v = v_ref[:, vc].astype(BF16)
        for c, rows in chunks:
            u_ref[h, c] = _dot_tn(k_state[rows, kc], v[rows])
    for h, kc, vc in heads:
        st = st_ref[h]
        for c, rows in chunks:
            sb_ref[h, c] = st.astype(BF16)
            decay = jnp.exp(b[(c + 1) * C - 8:(c + 1) * C, kc].T[:, 7:8])
            st = st * decay + u_ref[h, c]
        st_ref[h] = st
    for h, kc, vc in heads:
        v = v_ref[:, vc].astype(BF16)
        a = jnp.where(same_chunk_causal, _dot_nt(q_dec[:, kc], k_intra[:, kc]), 0.0)
        o_intra = _dot(a.astype(BF16), v)
        o_inter = jnp.concatenate([_dot(q_dec[rows, kc], sb_ref[h, c]) for c, rows in chunks], axis=0)
        o = _rms(o_intra + o_inter, gn_ref[:, vc])
        r = r_ref[:, vc]
        o_ref[:, vc] = (o * (r * jax.nn.sigmoid(r))).astype(o_ref.dtype)


def _gla(proj3, w2, ab, gn, layer):
    B, S, _ = proj3.shape
    RB = GLA_STEP
    HK = GLA_HEADS * GLA_DK
    HV = GLA_HEADS * GLA_DV
    assert S % RB == 0 and RB % GLA_CHUNK == 0

    def col(width, offset):
        assert offset % width == 0
        return pl.BlockSpec((None, RB, width), lambda b, s: (b, s, offset // width))

    return pl.pallas_call(
        _gla_body,
        grid=(B, S // RB),
        in_specs=[
            col(HK, C_QB), col(HK, C_KB), col(HV, C_VB), col(HV, C_RB), col(LANES, C_AL),
            pl.BlockSpec((None, LANES, HK), lambda b, s: (layer, 0, 0)),
            pl.BlockSpec((None, 1, HK), lambda b, s: (layer, 0, 0)),
            pl.BlockSpec((None, 1, HV), lambda b, s: (layer, 0, 0)),
        ],
        out_specs=pl.BlockSpec((None, RB, HV), lambda b, s: (b, s, 0)),
        out_shape=jax.ShapeDtypeStruct((B, S, HV), BF16),
        scratch_shapes=[pltpu.VMEM((GLA_HEADS, GLA_DK, GLA_DV), F32),
                        pltpu.VMEM((GLA_HEADS, RB // GLA_CHUNK, GLA_DK, GLA_DV), F32),
                        pltpu.VMEM((GLA_HEADS, RB // GLA_CHUNK, GLA_DK, GLA_DV), BF16)],
        compiler_params=_params("parallel", "arbitrary"),
        name="gla",
    )(proj3, proj3, proj3, proj3, proj3, w2, ab, gn)


def _merge_body(x_ref, oa_ref, ob_ref, gm_ref, wa_ref, wb_ref, wo_ref, o_ref):
    D = x_ref.shape[1]
    gm = gm_ref[...]
    y = (jax.nn.sigmoid(gm[:, :D]) * _dot(oa_ref[...], wa_ref[...])
         + jax.nn.sigmoid(gm[:, D:]) * _dot(ob_ref[...], wb_ref[...]))
    o_ref[...] = x_ref[...] + _dot(y.astype(BF16), wo_ref[...])


def _merge(x, o_a, o_b, proj, wa, wb, wo, layer, tm=512):
    T, D = x.shape
    DA = o_a.shape[1]
    DB = o_b.shape[1]
    assert T % tm == 0 and C_GM == 0
    return pl.pallas_call(
        _merge_body,
        grid=(T // tm,),
        in_specs=[
            pl.BlockSpec((tm, D), lambda i: (i, 0)),
            pl.BlockSpec((tm, DA), lambda i: (i, 0)),
            pl.BlockSpec((tm, DB), lambda i: (i, 0)),
            pl.BlockSpec((tm, 2 * D), lambda i: (i, 0)),
            pl.BlockSpec((None, DA, D), lambda i: (layer, 0, 0)),
            pl.BlockSpec((None, DB, D), lambda i: (layer, 0, 0)),
            pl.BlockSpec((None, D, D), lambda i: (layer, 0, 0)),
        ],
        out_specs=pl.BlockSpec((tm, D), lambda i: (i, 0)),
        out_shape=jax.ShapeDtypeStruct((T, D), F32),
        compiler_params=_params("parallel"),
        name="merge",
    )(x, o_a, o_b, proj, wa, wb, wo)


def _rel_bucket(dist):
    n = jnp.maximum(dist, 0)
    exact = REL_BUCKETS // 2
    nf = jnp.maximum(n, 1).astype(jnp.float32)
    log_b = exact + (jnp.log(nf / exact) / math.log(REL_MAX_DIST / exact)
                     * (REL_BUCKETS - exact)).astype(jnp.int32)
    return jnp.where(n < exact, n, jnp.minimum(log_b, REL_BUCKETS - 1))


def _regroup_w_in(w_in):
    widths = (NSA_HEADS * NSA_DH, 6 * NSA_GROUPS * NSA_DH, 3 * NSA_HEADS, GLA_HEADS * GLA_DK,
              GLA_HEADS * GLA_DK, GLA_HEADS * GLA_DV, GLA_RANK, GLA_HEADS * GLA_DV, 2 * D_MODEL)
    offs = np.concatenate([[0], np.cumsum(widths)])
    q_a, kv_a, g_a, q_b, k_b, v_b, a_lr, r_b, g_m = (w_in[..., offs[n]:offs[n + 1]] for n in range(9))

    def pad(w):
        return jnp.pad(w, ((0, 0), (0, 0), (0, LANES - w.shape[-1])))

    out = jnp.concatenate([g_m, v_b, r_b, q_a, q_b, k_b, kv_a, pad(g_a), pad(a_lr)], axis=-1)
    assert out.shape[-1] == N_PROJ
    return out.astype(BF16)


def _overlap_t(ncp, nsp, nc, ns):
    c = np.arange(ncp)[None, :] * CMP_STRIDE
    s = np.arange(nsp)[:, None] * SEL_BLOCK
    ov = (c < s + SEL_BLOCK) & (c + CMP_BLOCK > s) & (np.arange(ncp)[None, :] < nc) & (np.arange(nsp)[:, None] < ns)
    return jnp.asarray(ov.astype(np.float32))


def kernel(x, rel_table, ffn1_norm, ffn1_w_gate, ffn1_w_up, ffn1_w_down, mix_norm, w_in, cmp_pos_k, cmp_pos_v, cmp_k_w1, cmp_k_w2, cmp_v_w1, cmp_v_w2, gla_a_w2, gla_a_b, gla_out_norm, w_branch_nsa, w_branch_gla, w_out, ffn2_norm, ffn2_w_gate, ffn2_w_up, ffn2_w_down, final_norm):
    B, S, D = x.shape
    L = w_in.shape[0]
    T = B * S
    G, HPG, DH = NSA_GROUPS, NSA_HPG, NSA_DH
    nch = S // CMP_STRIDE
    nc = (S - CMP_BLOCK) // CMP_STRIDE + 1
    ns = S // SEL_BLOCK
    assert D == D_MODEL and nc == nch - 1

    w1g, w1u, w1d = ffn1_w_gate.astype(BF16), ffn1_w_up.astype(BF16), ffn1_w_down.astype(BF16)
    w2g, w2u, w2d = ffn2_w_gate.astype(BF16), ffn2_w_up.astype(BF16), ffn2_w_down.astype(BF16)
    w_proj = _regroup_w_in(w_in)
    wa, wb, wo = w_branch_nsa.astype(BF16), w_branch_gla.astype(BF16), w_out.astype(BF16)
    cmp_pos = jnp.stack([cmp_pos_k, cmp_pos_v]).reshape(2, L, 1, CMP_BLOCK * DH)
    cmp_w1 = jnp.stack([cmp_k_w1, cmp_v_w1]).astype(BF16)
    cmp_w2 = jnp.stack([cmp_k_w2, cmp_v_w2]).astype(BF16)
    gla_w2 = jnp.pad(gla_a_w2, ((0, 0), (0, LANES - GLA_RANK), (0, 0))).astype(BF16)
    gla_b = gla_a_b.reshape(L, 1, -1)
    gla_gn = gla_out_norm.reshape(L, 1, -1)
    n1 = ffn1_norm.reshape(L, 1, D)
    n2 = ffn2_norm.reshape(L, 1, D)
    nm = mix_norm.reshape(L, 1, D)

    buckets = _rel_bucket(jnp.arange(REL_MAX_DIST + 1, dtype=jnp.int32))
    thr = jnp.searchsorted(buckets, jnp.arange(REL_BUCKETS, dtype=jnp.int32), side="left").astype(jnp.int32)
    bias_toep, bias_win, bias_cmp = _bias_tables(thr, rel_table, S, nch)
    ovl_t = _overlap_t(nch, DH, nc, ns)

    xf = x.reshape(T, D)
    for l in range(L):
        xf = _ffn(xf, n1, w1g, w1u, w1d, l)
        proj = _proj(xf, nm, w_proj, l)
        proj3 = proj.reshape(B, S, N_PROJ)

        xc, ksw, vsw = _kvprep(proj3)
        kvc = _compress(xc, cmp_pos, cmp_w1, cmp_w2, l)
        o_a = _nsa(proj3, kvc, ksw, vsw, bias_cmp, bias_toep, bias_win, ovl_t)
        o_a = o_a.reshape(T, NSA_HEADS * DH)

        o_b = _gla(proj3, gla_w2, gla_b, gla_gn, l).reshape(T, GLA_HEADS * GLA_DV)

        xf = _merge(xf, o_a, o_b, proj, wa, wb, wo, l)
        xf = _ffn(xf, n2, w2g, w2u, w2d, l,
                  final_g=final_norm.reshape(1, D) if l == L - 1 else None)
    return xf.reshape(B, S, D)
```

```python
import functools
import math

import numpy as np
import jax
import jax.numpy as jnp
from jax import lax
from jax.experimental import pallas as pl
from jax.experimental.pallas import tpu as pltpu

F32 = jnp.float32
BF16 = jnp.bfloat16

NSA_HEADS = 8
NSA_GROUPS = 2
NSA_HPG = NSA_HEADS // NSA_GROUPS
NSA_DH = 64
CMP_BLOCK = 32
CMP_STRIDE = 16
SEL_BLOCK = 64
SEL_SHIFT = 6
SEL_TOPN = 16
WINDOW = 512
GLA_HEADS = 4
GLA_DK = 128
GLA_DV = 256
GLA_RANK = 16
GLA_TAU = 16.0
GLA_CHUNK = 64
REL_BUCKETS = 32
REL_MAX_DIST = 1024
EPS = 1e-6
NEG = -1e30
LOG2E = math.log2(math.e)

LANES = 128
VMEM_LIMIT = 56 * 1024 * 1024

ATT_TILE = 256
N_BIAS_TILES = REL_MAX_DIST // ATT_TILE + 2
N_WIN_TILES = WINDOW // ATT_TILE + 1
NSA_SUB = 128
GLA_STEP = 512

D_MODEL = 1024
C_GM = 0
C_VB = 2048
C_RB = 3072
C_QA = 4096
C_QB = 4608
C_KB = 5120
C_KV = 5632
C_GA = 6400
C_AL = 6528
N_PROJ = 6656


def _dot(a, b, precision=None):
    return lax.dot_general(a, b, (((1,), (0,)), ((), ())), precision=precision,
                           preferred_element_type=F32)


def _dot_nt(a, b, precision=None):
    return lax.dot_general(a, b, (((1,), (1,)), ((), ())), precision=precision,
                           preferred_element_type=F32)


def _dot_tn(a, b, precision=None):
    return lax.dot_general(a, b, (((0,), (0,)), ((), ())), precision=precision,
                           preferred_element_type=F32)


def _rms(x, g):
    return x * lax.rsqrt(jnp.mean(x * x, axis=-1, keepdims=True) + EPS) * g


def _params(*sem):
    return pltpu.CompilerParams(dimension_semantics=sem, vmem_limit_bytes=VMEM_LIMIT)


def _ffn_body(x_ref, g_ref, wg_ref, wu_ref, wd_ref, *rest, final, fc):
    if final:
        fg_ref, o_ref = rest
    else:
        (o_ref,) = rest
    x = x_ref[...]
    h = _rms(x, g_ref[...]).astype(BF16)
    acc = None
    for f0 in range(0, wg_ref.shape[1], fc):
        gate = _dot(h, wg_ref[:, f0:f0 + fc])
        up = _dot(h, wu_ref[:, f0:f0 + fc])
        act = (gate * jax.nn.sigmoid(gate) * up).astype(BF16)
        down = _dot(act, wd_ref[f0:f0 + fc, :])
        acc = down if acc is None else acc + down
    y = x + 0.5 * acc
    if final:
        y = _rms(y, fg_ref[...])
    o_ref[...] = y


def _ffn(x, g, wg, wu, wd, layer, final_g=None, tm=1024, fc=704):
    T, D = x.shape
    F = wg.shape[-1]
    assert T % tm == 0 and F % fc == 0
    final = final_g is not None
    resident = pl.Buffered(1)
    in_specs = [
        pl.BlockSpec((tm, D), lambda i: (i, 0)),
        pl.BlockSpec((None, 1, D), lambda i: (layer, 0, 0)),
        pl.BlockSpec((None, D, F), lambda i: (layer, 0, 0), pipeline_mode=resident),
        pl.BlockSpec((None, D, F), lambda i: (layer, 0, 0), pipeline_mode=resident),
        pl.BlockSpec((None, F, D), lambda i: (layer, 0, 0), pipeline_mode=resident),
    ]
    args = [x, g, wg, wu, wd]
    if final:
        in_specs.append(pl.BlockSpec((1, D), lambda i: (0, 0)))
        args.append(final_g)
    return pl.pallas_call(
        functools.partial(_ffn_body, final=final, fc=fc),
        grid=(T // tm,),
        in_specs=in_specs,
        out_specs=pl.BlockSpec((tm, D), lambda i: (i, 0)),
        out_shape=jax.ShapeDtypeStruct((T, D), F32),
        compiler_params=_params("parallel"),
        name="ffn",
    )(*args)


def _proj_body(x_ref, g_ref, w_ref, o_ref, xc_ref, ksw_ref, vsw_ref, stage_ref, *, nc, tiles_per_seq):
    h = _rms(x_ref[...], g_ref[...]).astype(BF16)
    for n0 in range(0, w_ref.shape[1], nc):
        o_ref[:, n0:n0 + nc] = _dot(h, w_ref[:, n0:n0 + nc])

    TM = o_ref.shape[0]
    DH = NSA_DH
    GW = NSA_GROUPS * DH

    def kv_cols(n, g):
        return slice(C_KV + n * GW + g * DH, C_KV + n * GW + (g + 1) * DH)

    t = (pl.program_id(0) % tiles_per_seq) * TM + lax.broadcasted_iota(jnp.int32, (TM, DH), 0)
    onehot = ((t >> SEL_SHIFT) == lax.broadcasted_iota(jnp.int32, (TM, DH), 1)).astype(BF16)
    ones = jnp.ones((TM, DH), BF16)
    zeros = jnp.zeros((TM, DH), BF16)
    for g in range(NSA_GROUPS):
        ksw_ref[0, g] = jnp.concatenate([o_ref[:, kv_cols(2, g)].astype(BF16), onehot], axis=1)
        vsw_ref[0, g] = jnp.concatenate([o_ref[:, kv_cols(3, g)].astype(BF16), ones], axis=1)
        ksw_ref[1, g] = jnp.concatenate([o_ref[:, kv_cols(4, g)].astype(BF16), zeros], axis=1)
        vsw_ref[1, g] = jnp.concatenate([o_ref[:, kv_cols(5, g)].astype(BF16), ones], axis=1)
    for s in range(2):
        stage_ref[s] = o_ref[:, C_KV + s * GW:C_KV + (s + 1) * GW]
        for l in range(CMP_STRIDE):
            x = stage_ref[s, pl.ds(l, TM // CMP_STRIDE, stride=CMP_STRIDE), :]
            for g in range(NSA_GROUPS):
                xc_ref[s, g, :, l * DH:(l + 1) * DH] = x[:, g * DH:(g + 1) * DH]


def _proj(x, g, w, layer, seq_len, tm=512, nc=1664):
    T, D = x.shape
    N = w.shape[-1]
    G, DH = NSA_GROUPS, NSA_DH
    B = T // seq_len
    nb = seq_len // tm
    assert T % seq_len == 0 and seq_len % tm == 0 and N % nc == 0 and tm % (8 * CMP_STRIDE) == 0

    def kv_out(width, rows):
        return pl.BlockSpec((2, None, G, rows, width), lambda i: (0, i // nb, 0, i % nb, 0))

    return pl.pallas_call(
        functools.partial(_proj_body, nc=nc, tiles_per_seq=nb),
        grid=(T // tm,),
        in_specs=[
            pl.BlockSpec((tm, D), lambda i: (i, 0)),
            pl.BlockSpec((None, 1, D), lambda i: (layer, 0, 0)),
            pl.BlockSpec((None, D, N), lambda i: (layer, 0, 0), pipeline_mode=pl.Buffered(1)),
        ],
        out_specs=[pl.BlockSpec((tm, N), lambda i: (i, 0)),
                   kv_out(CMP_STRIDE * DH, tm // CMP_STRIDE), kv_out(2 * DH, tm), kv_out(2 * DH, tm)],
        out_shape=[jax.ShapeDtypeStruct((T, N), F32),
                   jax.ShapeDtypeStruct((2, B, G, seq_len // CMP_STRIDE, CMP_STRIDE * DH), F32),
                   jax.ShapeDtypeStruct((2, B, G, seq_len, 2 * DH), BF16),
                   jax.ShapeDtypeStruct((2, B, G, seq_len, 2 * DH), BF16)],
        scratch_shapes=[pltpu.VMEM((2, tm, G * DH), F32)],
        compiler_params=_params("parallel"),
        name="proj",
    )(x, g, w)


def _compress_body(x_ref, pos_ref, w1_ref, w2_ref, o_ref):
    x = x_ref[...]
    half = x.shape[1]
    lo = (x + pos_ref[:, :half]).astype(BF16)
    hi = (x + pos_ref[:, half:]).astype(BF16)
    h_lo = _dot(lo, w1_ref[:half, :])
    h_hi = _dot(hi, w1_ref[half:, :])
    nch = x.shape[0]
    hid = h_lo + pltpu.roll(h_hi, nch - 1, 0)
    act = (hid * jax.nn.sigmoid(hid)).astype(BF16)
    o_ref[...] = _dot(act, w2_ref[...]).astype(o_ref.dtype)


def _compress(xc, pos, w1, w2, layer):
    _, B, G, NCH, CW = xc.shape
    HC = w1.shape[-1]
    dh = w2.shape[-1]
    return pl.pallas_call(
        _compress_body,
        grid=(2, B, G),
        in_specs=[
            pl.BlockSpec((None, None, None, NCH, CW), lambda s, b, g: (s, b, g, 0, 0)),
            pl.BlockSpec((None, None, 1, 2 * CW), lambda s, b, g: (s, layer, 0, 0)),
            pl.BlockSpec((None, None, 2 * CW, HC), lambda s, b, g: (s, layer, 0, 0)),
            pl.BlockSpec((None, None, HC, dh), lambda s, b, g: (s, layer, 0, 0)),
        ],
        out_specs=pl.BlockSpec((None, None, None, NCH, dh), lambda s, b, g: (s, b, g, 0, 0)),
        out_shape=jax.ShapeDtypeStruct((2, B, G, NCH, dh), BF16),
        compiler_params=_params("parallel", "parallel", "parallel"),
        name="compress",
    )(xc, pos, w1, w2)


def _bias_lookup(n, thr_ref, tab_ref):
    vals = [jnp.full(n.shape, tab_ref[0, h], F32) for h in range(NSA_HEADS)]
    for k in range(1, REL_BUCKETS):
        above = n >= thr_ref[k]
        vals = [jnp.where(above, tab_ref[k, h], v) for h, v in enumerate(vals)]
    return vals


def _toeplitz_body(thr_ref, tab_ref, o_ref, *, rows, window):
    dd = pl.program_id(0)
    T = o_ref.shape[-1]
    for r0 in range(0, T, rows):
        a = lax.broadcasted_iota(jnp.int32, (rows, T), 0) + r0
        b = lax.broadcasted_iota(jnp.int32, (rows, T), 1)
        dist = dd * T + a - b
        keep = (dist >= 0) & (dist < WINDOW) if window else dist >= 0
        for h, val in enumerate(_bias_lookup(jnp.clip(dist, 0, REL_MAX_DIST), thr_ref, tab_ref)):
            o_ref[h, r0:r0 + rows, :] = jnp.where(keep, val * LOG2E, NEG)


def _cmpbias_body(thr_ref, tab_ref, o_ref, *, rows):
    i = pl.program_id(0)
    _, TQ, NC = o_ref.shape
    for r0 in range(0, TQ, rows):
        t = lax.broadcasted_iota(jnp.int32, (rows, NC), 0) + (i * TQ + r0)
        c = lax.broadcasted_iota(jnp.int32, (rows, NC), 1)
        n = jnp.clip(t - (c * CMP_STRIDE + (CMP_BLOCK - 1)), 0, REL_MAX_DIST)
        for h, val in enumerate(_bias_lookup(n, thr_ref, tab_ref)):
            o_ref[h, r0:r0 + rows, :] = val


def _bias_tables(thr, rel_table, S, ncp):
    T = ATT_TILE
    H = NSA_HEADS
    rows = 16
    smem = pl.BlockSpec(memory_space=pltpu.SMEM)

    def toeplitz(n_tiles, window, name):
        return pl.pallas_call(
            functools.partial(_toeplitz_body, rows=rows, window=window),
            grid=(n_tiles,),
            in_specs=[smem, smem],
            out_specs=pl.BlockSpec((None, H, T, T), lambda d: (d, 0, 0, 0)),
            out_shape=jax.ShapeDtypeStruct((n_tiles, H, T, T), F32),
            compiler_params=_params("parallel"),
            name=name,
        )(thr, rel_table)

    toep = toeplitz(N_BIAS_TILES, False, "bias_toeplitz")
    toep_win = toeplitz(N_WIN_TILES + 1, True, "bias_window")
    cmpb = pl.pallas_call(
        functools.partial(_cmpbias_body, rows=rows),
        grid=(S // T,),
        in_specs=[smem, smem],
        out_specs=pl.BlockSpec((H, T, ncp), lambda i: (0, i, 0)),
        out_shape=jax.ShapeDtypeStruct((H, S, ncp), F32),
        compiler_params=_params("parallel"),
        name="bias_cmp",
    )(thr, rel_table)
    return toep, toep_win, cmpb


def _nsa_body(q_ref, gate_ref, kc_ref, vc_ref, ks_ref, vs_ref, kw_ref, vw_ref, bc_ref, bts_ref, btw_ref,
              ovl_ref, o_ref, qc_ref, qa_ref, m_ref, acc_ref, oacc_ref, sa_ref, sb_ref, sc_ref, alpha_ref,
              gs_ref, lc_ref, pcb_ref, ps_ref,
              *, n_sel_blocks, n_top):
    group = pl.program_id(1)
    i = pl.program_id(2)
    HPG, TQ, DH = qc_ref.shape
    TK = TQ
    SB = NSA_SUB
    NCP = kc_ref.shape[0]
    t0 = i * TQ
    R = HPG * TQ
    subs = [(h, a0) for a0 in range(0, TQ, SB) for h in range(HPG)]

    for h in range(HPG):
        q = q_ref[:, h * DH:(h + 1) * DH]
        qc_ref[h] = (q * (DH ** -0.5)).astype(BF16)
        qa_ref[h, :, :DH] = (q * (DH ** -0.5 * LOG2E)).astype(BF16)
        qa_ref[h, :, DH:] = jnp.zeros((TQ, DH), BF16)

    gates = jax.nn.sigmoid(gate_ref[...])
    gs_ref[...] = jnp.where(group == 0, gates, pltpu.roll(gates, LANES - 3 * HPG, 1))

    def gate_col(a0, col):
        return gs_ref[a0:a0 + SB, col:col + 1]

    lc_all = _dot_nt(qc_ref[...].reshape(R, DH), kc_ref[...])
    lc_ref[...] = (lc_all.reshape(HPG, TQ, NCP) + bc_ref[...]).reshape(R, NCP)
    c_end = lax.broadcasted_iota(jnp.int32, (SB, NCP), 1) * CMP_STRIDE + (CMP_BLOCK - 1)
    r_c = lax.broadcasted_iota(jnp.int32, (SB, NCP), 0)
    for a0 in range(0, TQ, SB):
        mc = c_end <= (t0 + a0 + r_c)
        p_heads = None
        for h in range(HPG):
            rows = slice(h * TQ + a0, h * TQ + a0 + SB)
            lc = jnp.where(mc, lc_ref[rows, :], NEG)
            pc = jnp.where(mc, jnp.exp(lc - jnp.max(lc, axis=-1, keepdims=True)), 0.0)
            den = jnp.sum(pc, axis=-1, keepdims=True)
            pc = pc * jnp.where(den > 0.0, 1.0 / den, 0.0)
            pcb_ref[rows, :] = pc.astype(BF16)
            p_heads = pc if p_heads is None else p_heads + pc
        ps_ref[a0:a0 + SB, :] = p_heads
    o_cmp = _dot(pcb_ref[...], vc_ref[...])
    for h in range(HPG):
        oacc_ref[h] = gs_ref[:, 3 * h:3 * h + 1] * o_cmp[h * TQ:(h + 1) * TQ]

    def reset():
        m_ref[...] = jnp.full(m_ref.shape, NEG, F32)
        acc_ref[...] = jnp.zeros(acc_ref.shape, F32)

    def logits(k_ref, j, bias_ref, bias_tile, s_ref):
        k = k_ref[pl.ds(pl.multiple_of(j * TK, TK), TK), :]
        qk = _dot_nt(qa_ref[...].reshape(R, 2 * DH), k)
        s_ref[...] = (qk.reshape(HPG, TQ, TK) + bias_ref[bias_tile]).reshape(R, TK)

    def update_steps(v_ref, j, s_ref):
        chunks = [slice(r0, r0 + SB) for r0 in range(0, R, SB)]

        def pass1(rows):
            m_old = m_ref[rows, :]
            m_new = jnp.maximum(m_old, jnp.max(s_ref[rows, :], axis=-1, keepdims=True))
            alpha_ref[rows, :] = jnp.exp2(m_old - m_new)
            m_ref[rows, :] = m_new

        def pass2(rows):
            v = v_ref[pl.ds(pl.multiple_of(j * TK, TK), TK), :]
            m_new = m_ref[rows, :]
            p = jnp.exp2(s_ref[rows, :] - jnp.concatenate([m_new] * (TK // LANES), axis=1))
            acc_ref[rows, :] = alpha_ref[rows, :] * acc_ref[rows, :] + _dot(p.astype(BF16), v)

        return ([functools.partial(pass1, rows) for rows in chunks]
                + [functools.partial(pass2, rows) for rows in chunks])

    def update(v_ref, j, s_ref):
        for piece in update_steps(v_ref, j, s_ref):
            piece()

    def finalize(gate_off):
        for h, a0 in subs:
            acc = acc_ref[h * TQ + a0:h * TQ + a0 + SB, :]
            o = (acc * (1.0 / pltpu.roll(acc, DH, 1)))[:, :DH]
            oacc_ref[h, a0:a0 + SB, :] += gate_col(a0, 3 * h + gate_off) * o

    reset()
    buf_a, buf_b, buf_c = sa_ref, sb_ref, sc_ref
    win_pieces = []
    for n, buf in enumerate((buf_a, buf_b, buf_c)):
        dd = N_WIN_TILES - 1 - n
        j = jnp.maximum(i - dd, 0)
        logits(kw_ref, j, btw_ref, jnp.where(i < dd, N_WIN_TILES, dd), buf)
        win_pieces += update_steps(vw_ref, j, buf)

    imp_t = _dot_nt(ovl_ref[...], ps_ref[...], precision=lax.Precision.HIGHEST)
    s_io = lax.broadcasted_iota(jnp.int32, (DH, TQ), 0)
    jcur = (t0 + lax.broadcasted_iota(jnp.int32, (DH, TQ), 1)) >> SEL_SHIFT
    forced = (s_io == 0) | (s_io == jcur) | (s_io == jcur - 1)
    score = jnp.where(forced, 1e6, jnp.where(s_io <= jcur, imp_t, -1e6))
    sub8 = lax.broadcasted_iota(jnp.int32, (8, TQ), 0)
    cnt = [jnp.zeros((8, TQ), jnp.int32) for _ in range(DH // 8)]
    emitted = 0
    for sp in range(n_sel_blocks):
        row = score[sp:sp + 1, :]
        for g in range(DH // 8):
            blk = score[8 * g:8 * g + 8, :]
            if 8 * g > sp:
                beats = row >= blk
            elif 8 * g + 7 <= sp:
                beats = row > blk
            else:
                beats = (row > blk) | ((row == blk) & (sub8 > sp - 8 * g))
            cnt[g] = cnt[g] + jnp.where(beats, 1, 0)
        due = (sp + 1) * len(win_pieces) // n_sel_blocks
        for piece in win_pieces[emitted:due]:
            piece()
        emitted = due
    finalize(2)
    rank = jnp.concatenate(cnt, axis=0)
    drop_t = jnp.where((rank < n_top) & (s_io < n_sel_blocks), 0.0, NEG)
    drop = jnp.concatenate([jnp.zeros((DH, TQ), F32), drop_t], axis=0).T.astype(BF16)
    for h in range(HPG):
        qa_ref[h, :, DH:] = drop[:, DH:]

    def sel_logits(j, buf):
        logits(ks_ref, j, bts_ref, jnp.minimum(i - j, N_BIAS_TILES - 1), buf)

    reset()
    sel_logits(0, buf_a)

    def pipeline(j, n):
        for t in range(0, n, 2):
            sel_logits(j + t + 1, buf_b)
            update(vs_ref, j + t, buf_a)
            sel_logits(j + t + 2, buf_a)
            update(vs_ref, j + t + 1, buf_b)

    unroll = 4
    n_long = i // unroll

    def long_step(n, carry):
        pipeline(unroll * n, unroll)
        return carry

    lax.fori_loop(0, n_long, long_step, 0)
    n_pairs = (i - unroll * n_long) // 2

    def pair_step(n, carry):
        pipeline(unroll * n_long + 2 * n, 2)
        return carry

    lax.fori_loop(0, n_pairs, pair_step, 0)
    j_tail = unroll * n_long + 2 * n_pairs

    @pl.when(j_tail == i)
    def _():
        update(vs_ref, i, buf_a)

    @pl.when(j_tail < i)
    def _():
        sel_logits(i, buf_b)
        update(vs_ref, j_tail, buf_a)
        update(vs_ref, i, buf_b)

    finalize(1)

    for h in range(HPG):
        o_ref[:, h * DH:(h + 1) * DH] = oacc_ref[h].astype(o_ref.dtype)


def _nsa(proj3, kvc, ksw, vsw, bias_cmp, bias_toep, bias_win, ovl_t):
    B, S, _ = proj3.shape
    G, HPG, DH = NSA_GROUPS, NSA_HPG, NSA_DH
    GW = HPG * DH
    TQ = ATT_TILE
    NCP = kvc.shape[3]
    ns = S // SEL_BLOCK
    assert S % TQ == 0 and (HPG * TQ) % NSA_SUB == 0 and WINDOW % TQ == 0
    assert 2 * DH == LANES and ns <= DH and ovl_t.shape == (DH, NCP)
    assert C_QA % GW == 0 and C_GA % LANES == 0 and 3 * NSA_HEADS <= LANES

    def seq_spec(idx):
        return pl.BlockSpec((None, None, None, S, 2 * DH), lambda b, g, i: (idx, b, g, 0, 0))

    def cmp_spec(idx):
        return pl.BlockSpec((None, None, None, NCP, DH), lambda b, g, i: (idx, b, g, 0, 0))

    return pl.pallas_call(
        functools.partial(_nsa_body, n_sel_blocks=ns, n_top=min(SEL_TOPN, ns)),
        grid=(B, G, S // TQ),
        in_specs=[
            pl.BlockSpec((None, TQ, GW), lambda b, g, i: (b, i, C_QA // GW + g)),
            pl.BlockSpec((None, TQ, LANES), lambda b, g, i: (b, i, C_GA // LANES)),
            cmp_spec(0), cmp_spec(1),
            seq_spec(0), seq_spec(0), seq_spec(1), seq_spec(1),
            pl.BlockSpec((HPG, TQ, NCP), lambda b, g, i: (g, i, 0)),
            pl.BlockSpec((N_BIAS_TILES, HPG, TQ, TQ), lambda b, g, i: (0, g, 0, 0)),
            pl.BlockSpec((N_WIN_TILES + 1, HPG, TQ, TQ), lambda b, g, i: (0, g, 0, 0)),
            pl.BlockSpec((DH, NCP), lambda b, g, i: (0, 0)),
        ],
        out_specs=pl.BlockSpec((None, TQ, GW), lambda b, g, i: (b, i, g)),
        out_shape=jax.ShapeDtypeStruct((B, S, G * GW), BF16),
        scratch_shapes=[
            pltpu.VMEM((HPG, TQ, DH), BF16),
            pltpu.VMEM((HPG, TQ, 2 * DH), BF16),
            pltpu.VMEM((HPG * TQ, LANES), F32),
            pltpu.VMEM((HPG * TQ, 2 * DH), F32),
            pltpu.VMEM((HPG, TQ, DH), F32),
            pltpu.VMEM((HPG * TQ, TQ), F32),
            pltpu.VMEM((HPG * TQ, TQ), F32),
            pltpu.VMEM((HPG * TQ, TQ), F32),
            pltpu.VMEM((HPG * TQ, LANES), F32),
            pltpu.VMEM((TQ, LANES), F32),
            pltpu.VMEM((HPG * TQ, NCP), F32),
            pltpu.VMEM((HPG * TQ, NCP), BF16),
            pltpu.VMEM((TQ, NCP), F32),
        ],
        compiler_params=_params("parallel", "parallel", "arbitrary"),
        name="nsa",
    )(proj3, proj3, kvc, kvc, ksw, vsw, ksw, vsw, bias_cmp, bias_toep, bias_win, ovl_t)


def _gla_body(q_ref, k_ref, v_ref, r_ref, al_ref, w2_ref, ab_ref, gn_ref, o_ref, st_ref, u_ref, sb_ref):
    @pl.when(pl.program_id(1) == 0)
    def _():
        st_ref[...] = jnp.zeros_like(st_ref)

    C = GLA_CHUNK
    RB = q_ref.shape[0]
    n_chunks = RB // C
    pre = _dot(al_ref[...].astype(BF16), w2_ref[...]) + ab_ref[...]
    la = (jnp.minimum(pre, 0.0) - jnp.log(1.0 + jnp.exp(-jnp.abs(pre)))) * (1.0 / GLA_TAU)

    la_hi = la.astype(BF16)
    rest = la - la_hi.astype(F32)
    la_mid = rest.astype(BF16)
    la_lo = (rest - la_mid.astype(F32)).astype(BF16)
    r_io = lax.broadcasted_iota(jnp.int32, (C, 3 * C), 0)
    c_io = lax.broadcasted_iota(jnp.int32, (C, 3 * C), 1)
    tri3 = ((c_io & (C - 1)) <= r_io).astype(BF16)
    b_parts, bl_parts = [], []
    for c in range(n_chunks):
        rows = slice(c * C, (c + 1) * C)
        b_c = _dot(tri3, jnp.concatenate([la_hi[rows], la_mid[rows], la_lo[rows]], axis=0))
        b_parts.append(b_c)
        bl_parts.append(jnp.broadcast_to(b_c[C - 1:C, :], b_c.shape))
    b = jnp.concatenate(b_parts, axis=0)
    b_last = jnp.concatenate(bl_parts, axis=0)

    k = k_ref[...]
    q_dec = (q_ref[...] * (jnp.exp(b) * (GLA_DK ** -0.5))).astype(BF16)
    k_intra = (k * jnp.exp(-b)).astype(BF16)
    k_state = (k * jnp.exp(b_last - b)).astype(BF16)

    rr = lax.broadcasted_iota(jnp.int32, (RB, RB), 0)
    cc = lax.broadcasted_iota(jnp.int32, (RB, RB), 1)
    same_chunk_causal = (cc <= rr) & ((rr & -C) == (cc & -C))
    heads = [(h, slice(h * GLA_DK, (h + 1) * GLA_DK), slice(h * GLA_DV, (h + 1) * GLA_DV))
             for h in range(GLA_HEADS)]
    chunks = [(c, slice(c * C, (c + 1) * C)) for c in range(n_chunks)]
    for h, kc, vc in heads:
        v = v_ref[:, vc].astype(BF16)
        for c, rows in chunks:
            u_ref[h, c] = _dot_tn(k_state[rows, kc], v[rows])
    for h, kc, vc in heads:
        st = st_ref[h]
        for c, rows in chunks:
            sb_ref[h, c] = st.astype(BF16)
            decay = jnp.exp(b[(c + 1) * C - 8:(c + 1) * C, kc].T[:, 7:8])
            st = st * decay + u_ref[h, c]
        st_ref[h] = st
    for h, kc, vc in heads:
        v = v_ref[:, vc].astype(BF16)
        a = jnp.where(same_chunk_causal, _dot_nt(q_dec[:, kc], k_intra[:, kc]), 0.0)
        o_intra = _dot(a.astype(BF16), v)
        o_inter = jnp.concatenate([_dot(q_dec[rows, kc], sb_ref[h, c]) for c, rows in chunks], axis=0)
        o = _rms(o_intra + o_inter, gn_ref[:, vc])
        r = r_ref[:, vc]
        o_ref[:, vc] = (o * (r * jax.nn.sigmoid(r))).astype(o_ref.dtype)


def _gla(proj3, w2, ab, gn, layer):
    B, S, _ = proj3.shape
    RB = GLA_STEP
    HK = GLA_HEADS * GLA_DK
    HV = GLA_HEADS * GLA_DV
    assert S % RB == 0 and RB % GLA_CHUNK == 0

    def col(width, offset):
        assert offset % width == 0
        return pl.BlockSpec((None, RB, width), lambda b, s: (b, s, offset // width))

    return pl.pallas_call(
        _gla_body,
        grid=(B, S // RB),
        in_specs=[
            col(HK, C_QB), col(HK, C_KB), col(HV, C_VB), col(HV, C_RB), col(LANES, C_AL),
            pl.BlockSpec((None, LANES, HK), lambda b, s: (layer, 0, 0)),
            pl.BlockSpec((None, 1, HK), lambda b, s: (layer, 0, 0)),
            pl.BlockSpec((None, 1, HV), lambda b, s: (layer, 0, 0)),
        ],
        out_specs=pl.BlockSpec((None, RB, HV), lambda b, s: (b, s, 0)),
        out_shape=jax.ShapeDtypeStruct((B, S, HV), BF16),
        scratch_shapes=[pltpu.VMEM((GLA_HEADS, GLA_DK, GLA_DV), F32),
                        pltpu.VMEM((GLA_HEADS, RB // GLA_CHUNK, GLA_DK, GLA_DV), F32),
                        pltpu.VMEM((GLA_HEADS, RB // GLA_CHUNK, GLA_DK, GLA_DV), BF16)],
        compiler_params=_params("parallel", "arbitrary"),
        name="gla",
    )(proj3, proj3, proj3, proj3, proj3, w2, ab, gn)


def _merge_body(x_ref, oa_ref, ob_ref, gm_ref, wa_ref, wb_ref, wo_ref, o_ref):
    D = x_ref.shape[1]
    gm = gm_ref[...]
    y = (jax.nn.sigmoid(gm[:, :D]) * _dot(oa_ref[...], wa_ref[...])
         + jax.nn.sigmoid(gm[:, D:]) * _dot(ob_ref[...], wb_ref[...]))
    o_ref[...] = x_ref[...] + _dot(y.astype(BF16), wo_ref[...])


def _merge(x, o_a, o_b, proj, wa, wb, wo, layer, tm=512):
    T, D = x.shape
    DA = o_a.shape[1]
    DB = o_b.shape[1]
    assert T % tm == 0 and C_GM == 0
    return pl.pallas_call(
        _merge_body,
        grid=(T // tm,),
        in_specs=[
            pl.BlockSpec((tm, D), lambda i: (i, 0)),
            pl.BlockSpec((tm, DA), lambda i: (i, 0)),
            pl.BlockSpec((tm, DB), lambda i: (i, 0)),
            pl.BlockSpec((tm, 2 * D), lambda i: (i, 0)),
            pl.BlockSpec((None, DA, D), lambda i: (layer, 0, 0)),
            pl.BlockSpec((None, DB, D), lambda i: (layer, 0, 0)),
            pl.BlockSpec((None, D, D), lambda i: (layer, 0, 0)),
        ],
        out_specs=pl.BlockSpec((tm, D), lambda i: (i, 0)),
        out_shape=jax.ShapeDtypeStruct((T, D), F32),
        compiler_params=_params("parallel"),
        name="merge",
    )(x, o_a, o_b, proj, wa, wb, wo)


def _rel_bucket(dist):
    n = jnp.maximum(dist, 0)
    exact = REL_BUCKETS // 2
    nf = jnp.maximum(n, 1).astype(jnp.float32)
    log_b = exact + (jnp.log(nf / exact) / math.log(REL_MAX_DIST / exact)
                     * (REL_BUCKETS - exact)).astype(jnp.int32)
    return jnp.where(n < exact, n, jnp.minimum(log_b, REL_BUCKETS - 1))


def _regroup_w_in(w_in):
    widths = (NSA_HEADS * NSA_DH, 6 * NSA_GROUPS * NSA_DH, 3 * NSA_HEADS, GLA_HEADS * GLA_DK,
              GLA_HEADS * GLA_DK, GLA_HEADS * GLA_DV, GLA_RANK, GLA_HEADS * GLA_DV, 2 * D_MODEL)
    offs = np.concatenate([[0], np.cumsum(widths)])
    q_a, kv_a, g_a, q_b, k_b, v_b, a_lr, r_b, g_m = (w_in[..., offs[n]:offs[n + 1]] for n in range(9))

    def pad(w):
        return jnp.pad(w, ((0, 0), (0, 0), (0, LANES - w.shape[-1])))

    out = jnp.concatenate([g_m, v_b, r_b, q_a, q_b, k_b, kv_a, pad(g_a), pad(a_lr)], axis=-1)
    assert out.shape[-1] == N_PROJ
    return out.astype(BF16)


def _overlap_t(ncp, nsp, nc, ns):
    c = np.arange(ncp)[None, :] * CMP_STRIDE
    s = np.arange(nsp)[:, None] * SEL_BLOCK
    ov = (c < s + SEL_BLOCK) & (c + CMP_BLOCK > s) & (np.arange(ncp)[None, :] < nc) & (np.arange(nsp)[:, None] < ns)
    return jnp.asarray(ov.astype(np.float32))


def kernel(x, rel_table, ffn1_norm, ffn1_w_gate, ffn1_w_up, ffn1_w_down, mix_norm, w_in, cmp_pos_k, cmp_pos_v, cmp_k_w1, cmp_k_w2, cmp_v_w1, cmp_v_w2, gla_a_w2, gla_a_b, gla_out_norm, w_branch_nsa, w_branch_gla, w_out, ffn2_norm, ffn2_w_gate, ffn2_w_up, ffn2_w_down, final_norm):
    B, S, D = x.shape
    L = w_in.shape[0]
    T = B * S
    G, HPG, DH = NSA_GROUPS, NSA_HPG, NSA_DH
    nch = S // CMP_STRIDE
    nc = (S - CMP_BLOCK) // CMP_STRIDE + 1
    ns = S // SEL_BLOCK
    assert D == D_MODEL and nc == nch - 1

    w1g, w1u, w1d = ffn1_w_gate.astype(BF16), ffn1_w_up.astype(BF16), ffn1_w_down.astype(BF16)
    w2g, w2u, w2d = ffn2_w_gate.astype(BF16), ffn2_w_up.astype(BF16), ffn2_w_down.astype(BF16)
    w_proj = _regroup_w_in(w_in)
    wa, wb, wo = w_branch_nsa.astype(BF16), w_branch_gla.astype(BF16), w_out.astype(BF16)
    cmp_pos = jnp.stack([cmp_pos_k, cmp_pos_v]).reshape(2, L, 1, CMP_BLOCK * DH)
    cmp_w1 = jnp.stack([cmp_k_w1, cmp_v_w1]).astype(BF16)
    cmp_w2 = jnp.stack([cmp_k_w2, cmp_v_w2]).astype(BF16)
    gla_w2 = jnp.pad(gla_a_w2, ((0, 0), (0, LANES - GLA_RANK), (0, 0))).astype(BF16)
    gla_b = gla_a_b.reshape(L, 1, -1)
    gla_gn = gla_out_norm.reshape(L, 1, -1)
    n1 = ffn1_norm.reshape(L, 1, D)
    n2 = ffn2_norm.reshape(L, 1, D)
    nm = mix_norm.reshape(L, 1, D)

    buckets = _rel_bucket(jnp.arange(REL_MAX_DIST + 1, dtype=jnp.int32))
    thr = jnp.searchsorted(buckets, jnp.arange(REL_BUCKETS, dtype=jnp.int32), side="left").astype(jnp.int32)
    bias_toep, bias_win, bias_cmp = _bias_tables(thr, rel_table, S, nch)
    ovl_t = _overlap_t(nch, DH, nc, ns)

    xf = x.reshape(T, D)
    for l in range(L):
        xf = _ffn(xf, n1, w1g, w1u, w1d, l)
        proj, xc, ksw, vsw = _proj(xf, nm, w_proj, l, S)
        proj3 = proj.reshape(B, S, N_PROJ)

        kvc = _compress(xc, cmp_pos, cmp_w1, cmp_w2, l)
        o_a = _nsa(proj3, kvc, ksw, vsw, bias_cmp, bias_toep, bias_win, ovl_t)
        o_a = o_a.reshape(T, NSA_HEADS * DH)

        o_b = _gla(proj3, gla_w2, gla_b, gla_gn, l).reshape(T, GLA_HEADS * GLA_DV)

        xf = _merge(xf, o_a, o_b, proj, wa, wb, wo, l)
        xf = _ffn(xf, n2, w2g, w2u, w2d, l,
                  final_g=final_norm.reshape(1, D) if l == L - 1 else None)
    return xf.reshape(B, S, D)
```

```python
import functools
import math

import numpy as np
import jax
import jax.numpy as jnp
from jax import lax
from jax.experimental import pallas as pl
from jax.experimental.pallas import tpu as pltpu

F32 = jnp.float32
BF16 = jnp.bfloat16

NSA_HEADS = 8
NSA_GROUPS = 2
NSA_HPG = NSA_HEADS // NSA_GROUPS
NSA_DH = 64
CMP_BLOCK = 32
CMP_STRIDE = 16
SEL_BLOCK = 64
SEL_SHIFT = 6
SEL_TOPN = 16
WINDOW = 512
GLA_HEADS = 4
GLA_DK = 128
GLA_DV = 256
GLA_RANK = 16
GLA_TAU = 16.0
GLA_CHUNK = 64
REL_BUCKETS = 32
REL_MAX_DIST = 1024
EPS = 1e-6
NEG = -1e30
LOG2E = math.log2(math.e)

LANES = 128
VMEM_LIMIT = 56 * 1024 * 1024

ATT_TILE = 256
N_BIAS_TILES = REL_MAX_DIST // ATT_TILE + 2
N_WIN_TILES = WINDOW // ATT_TILE + 1
NSA_SUB = 128
GLA_STEP = 512

D_MODEL = 1024
C_GM = 0
C_VB = 2048
C_RB = 3072
C_QA = 4096
C_QB = 4608
C_KB = 5120
C_KV = 5632
C_GA = 6400
C_AL = 6528
N_PROJ = 6656


def _dot(a, b, precision=None):
    return lax.dot_general(a, b, (((1,), (0,)), ((), ())), precision=precision,
                           preferred_element_type=F32)


def _dot_nt(a, b, precision=None):
    return lax.dot_general(a, b, (((1,), (1,)), ((), ())), precision=precision,
                           preferred_element_type=F32)


def _dot_tn(a, b, precision=None):
    return lax.dot_general(a, b, (((0,), (0,)), ((), ())), precision=precision,
                           preferred_element_type=F32)


def _rms(x, g):
    return x * lax.rsqrt(jnp.mean(x * x, axis=-1, keepdims=True) + EPS) * g


def _params(*sem):
    return pltpu.CompilerParams(dimension_semantics=sem, vmem_limit_bytes=VMEM_LIMIT)


def _ffn_body(x_ref, g_ref, wg_ref, wu_ref, wd_ref, *rest, final, fc):
    if final:
        fg_ref, o_ref = rest
    else:
        (o_ref,) = rest
    x = x_ref[...]
    h = _rms(x, g_ref[...]).astype(BF16)
    acc = None
    for f0 in range(0, wg_ref.shape[1], fc):
        gate = _dot(h, wg_ref[:, f0:f0 + fc])
        up = _dot(h, wu_ref[:, f0:f0 + fc])
        act = (gate * jax.nn.sigmoid(gate) * up).astype(BF16)
        down = _dot(act, wd_ref[f0:f0 + fc, :])
        acc = down if acc is None else acc + down
    y = x + 0.5 * acc
    if final:
        y = _rms(y, fg_ref[...])
    o_ref[...] = y


def _ffn(x, g, wg, wu, wd, layer, final_g=None, tm=1024, fc=704):
    T, D = x.shape
    F = wg.shape[-1]
    assert T % tm == 0 and F % fc == 0
    final = final_g is not None
    resident = pl.Buffered(1)
    in_specs = [
        pl.BlockSpec((tm, D), lambda i: (i, 0)),
        pl.BlockSpec((None, 1, D), lambda i: (layer, 0, 0)),
        pl.BlockSpec((None, D, F), lambda i: (layer, 0, 0), pipeline_mode=resident),
        pl.BlockSpec((None, D, F), lambda i: (layer, 0, 0), pipeline_mode=resident),
        pl.BlockSpec((None, F, D), lambda i: (layer, 0, 0), pipeline_mode=resident),
    ]
    args = [x, g, wg, wu, wd]
    if final:
        in_specs.append(pl.BlockSpec((1, D), lambda i: (0, 0)))
        args.append(final_g)
    return pl.pallas_call(
        functools.partial(_ffn_body, final=final, fc=fc),
        grid=(T // tm,),
        in_specs=in_specs,
        out_specs=pl.BlockSpec((tm, D), lambda i: (i, 0)),
        out_shape=jax.ShapeDtypeStruct((T, D), F32),
        compiler_params=_params("parallel"),
        name="ffn",
    )(*args)


def _proj_body(x_ref, g_ref, w_ref, o_ref, xc_ref, ksw_ref, vsw_ref, stage_ref, *, nc, tiles_per_seq):
    h = _rms(x_ref[...], g_ref[...]).astype(BF16)
    for n0 in range(0, w_ref.shape[1], nc):
        o_ref[:, n0:n0 + nc] = _dot(h, w_ref[:, n0:n0 + nc])

    TM = o_ref.shape[0]
    DH = NSA_DH
    GW = NSA_GROUPS * DH

    def kv_cols(n, g):
        return slice(C_KV + n * GW + g * DH, C_KV + n * GW + (g + 1) * DH)

    t = (pl.program_id(0) % tiles_per_seq) * TM + lax.broadcasted_iota(jnp.int32, (TM, DH), 0)
    onehot = ((t >> SEL_SHIFT) == lax.broadcasted_iota(jnp.int32, (TM, DH), 1)).astype(BF16)
    ones = jnp.ones((TM, DH), BF16)
    zeros = jnp.zeros((TM, DH), BF16)
    for g in range(NSA_GROUPS):
        ksw_ref[0, g] = jnp.concatenate([o_ref[:, kv_cols(2, g)].astype(BF16), onehot], axis=1)
        vsw_ref[0, g] = jnp.concatenate([o_ref[:, kv_cols(3, g)].astype(BF16), ones], axis=1)
        ksw_ref[1, g] = jnp.concatenate([o_ref[:, kv_cols(4, g)].astype(BF16), zeros], axis=1)
        vsw_ref[1, g] = jnp.concatenate([o_ref[:, kv_cols(5, g)].astype(BF16), ones], axis=1)
    for s in range(2):
        stage_ref[s] = o_ref[:, C_KV + s * GW:C_KV + (s + 1) * GW]
        for l in range(CMP_STRIDE):
            x = stage_ref[s, pl.ds(l, TM // CMP_STRIDE, stride=CMP_STRIDE), :]
            for g in range(NSA_GROUPS):
                xc_ref[s, g, :, l * DH:(l + 1) * DH] = x[:, g * DH:(g + 1) * DH]


def _proj(x, g, w, layer, seq_len, tm=512, nc=1664):
    T, D = x.shape
    N = w.shape[-1]
    G, DH = NSA_GROUPS, NSA_DH
    B = T // seq_len
    nb = seq_len // tm
    assert T % seq_len == 0 and seq_len % tm == 0 and N % nc == 0 and tm % (8 * CMP_STRIDE) == 0

    def kv_out(width, rows):
        return pl.BlockSpec((2, None, G, rows, width), lambda i: (0, i // nb, 0, i % nb, 0))

    return pl.pallas_call(
        functools.partial(_proj_body, nc=nc, tiles_per_seq=nb),
        grid=(T // tm,),
        in_specs=[
            pl.BlockSpec((tm, D), lambda i: (i, 0)),
            pl.BlockSpec((None, 1, D), lambda i: (layer, 0, 0)),
            pl.BlockSpec((None, D, N), lambda i: (layer, 0, 0), pipeline_mode=pl.Buffered(1)),
        ],
        out_specs=[pl.BlockSpec((tm, N), lambda i: (i, 0)),
                   kv_out(CMP_STRIDE * DH, tm // CMP_STRIDE), kv_out(2 * DH, tm), kv_out(2 * DH, tm)],
        out_shape=[jax.ShapeDtypeStruct((T, N), F32),
                   jax.ShapeDtypeStruct((2, B, G, seq_len // CMP_STRIDE, CMP_STRIDE * DH), F32),
                   jax.ShapeDtypeStruct((2, B, G, seq_len, 2 * DH), BF16),
                   jax.ShapeDtypeStruct((2, B, G, seq_len, 2 * DH), BF16)],
        scratch_shapes=[pltpu.VMEM((2, tm, G * DH), F32)],
        compiler_params=_params("parallel"),
        name="proj",
    )(x, g, w)


def _compress_body(x_ref, pos_ref, w1_ref, w2_ref, o_ref):
    x = x_ref[...]
    half = x.shape[1]
    lo = (x + pos_ref[:, :half]).astype(BF16)
    hi = (x + pos_ref[:, half:]).astype(BF16)
    h_lo = _dot(lo, w1_ref[:half, :])
    h_hi = _dot(hi, w1_ref[half:, :])
    nch = x.shape[0]
    hid = h_lo + pltpu.roll(h_hi, nch - 1, 0)
    act = (hid * jax.nn.sigmoid(hid)).astype(BF16)
    o_ref[...] = _dot(act, w2_ref[...]).astype(o_ref.dtype)


def _compress(xc, pos, w1, w2, layer):
    _, B, G, NCH, CW = xc.shape
    HC = w1.shape[-1]
    dh = w2.shape[-1]
    return pl.pallas_call(
        _compress_body,
        grid=(2, B, G),
        in_specs=[
            pl.BlockSpec((None, None, None, NCH, CW), lambda s, b, g: (s, b, g, 0, 0)),
            pl.BlockSpec((None, None, 1, 2 * CW), lambda s, b, g: (s, layer, 0, 0)),
            pl.BlockSpec((None, None, 2 * CW, HC), lambda s, b, g: (s, layer, 0, 0)),
            pl.BlockSpec((None, None, HC, dh), lambda s, b, g: (s, layer, 0, 0)),
        ],
        out_specs=pl.BlockSpec((None, None, None, NCH, dh), lambda s, b, g: (s, b, g, 0, 0)),
        out_shape=jax.ShapeDtypeStruct((2, B, G, NCH, dh), BF16),
        compiler_params=_params("parallel", "parallel", "parallel"),
        name="compress",
    )(xc, pos, w1, w2)


def _bias_lookup(n, thr_ref, tab_ref):
    vals = [jnp.full(n.shape, tab_ref[0, h], F32) for h in range(NSA_HEADS)]
    for k in range(1, REL_BUCKETS):
        above = n >= thr_ref[k]
        vals = [jnp.where(above, tab_ref[k, h], v) for h, v in enumerate(vals)]
    return vals


def _toeplitz_body(thr_ref, tab_ref, o_ref, *, rows, window):
    dd = pl.program_id(0)
    T = o_ref.shape[-1]
    for r0 in range(0, T, rows):
        a = lax.broadcasted_iota(jnp.int32, (rows, T), 0) + r0
        b = lax.broadcasted_iota(jnp.int32, (rows, T), 1)
        dist = dd * T + a - b
        keep = (dist >= 0) & (dist < WINDOW) if window else dist >= 0
        for h, val in enumerate(_bias_lookup(jnp.clip(dist, 0, REL_MAX_DIST), thr_ref, tab_ref)):
            o_ref[h, r0:r0 + rows, :] = jnp.where(keep, val * LOG2E, NEG)


def _cmpbias_body(thr_ref, tab_ref, o_ref, *, rows):
    i = pl.program_id(0)
    _, TQ, NC = o_ref.shape
    for r0 in range(0, TQ, rows):
        t = lax.broadcasted_iota(jnp.int32, (rows, NC), 0) + (i * TQ + r0)
        c = lax.broadcasted_iota(jnp.int32, (rows, NC), 1)
        n = jnp.clip(t - (c * CMP_STRIDE + (CMP_BLOCK - 1)), 0, REL_MAX_DIST)
        for h, val in enumerate(_bias_lookup(n, thr_ref, tab_ref)):
            o_ref[h, r0:r0 + rows, :] = val


def _bias_tables(thr, rel_table, S, ncp):
    T = ATT_TILE
    H = NSA_HEADS
    rows = 16
    smem = pl.BlockSpec(memory_space=pltpu.SMEM)

    def toeplitz(n_tiles, window, name):
        return pl.pallas_call(
            functools.partial(_toeplitz_body, rows=rows, window=window),
            grid=(n_tiles,),
            in_specs=[smem, smem],
            out_specs=pl.BlockSpec((None, H, T, T), lambda d: (d, 0, 0, 0)),
            out_shape=jax.ShapeDtypeStruct((n_tiles, H, T, T), F32),
            compiler_params=_params("parallel"),
            name=name,
        )(thr, rel_table)

    toep = toeplitz(N_BIAS_TILES, False, "bias_toeplitz")
    toep_win = toeplitz(N_WIN_TILES + 1, True, "bias_window")
    cmpb = pl.pallas_call(
        functools.partial(_cmpbias_body, rows=rows),
        grid=(S // T,),
        in_specs=[smem, smem],
        out_specs=pl.BlockSpec((H, T, ncp), lambda i: (0, i, 0)),
        out_shape=jax.ShapeDtypeStruct((H, S, ncp), F32),
        compiler_params=_params("parallel"),
        name="bias_cmp",
    )(thr, rel_table)
    return toep, toep_win, cmpb


def _nsa_body(q_ref, gate_ref, kc_ref, vc_ref, ks_ref, vs_ref, kw_ref, vw_ref, bc_ref, bts_ref, btw_ref,
              ovl_ref, o_ref, qc_ref, qa_ref, m_ref, acc_ref, oacc_ref, sa_ref, sb_ref, sc_ref, alpha_ref,
              gs_ref, lc_ref, pcb_ref, ps_ref,
              *, n_sel_blocks, n_top):
    group = pl.program_id(1)
    i = pl.program_id(2)
    HPG, TQ, DH = qc_ref.shape
    TK = TQ
    SB = NSA_SUB
    NCP = kc_ref.shape[0]
    t0 = i * TQ
    R = HPG * TQ
    subs = [(h, a0) for a0 in range(0, TQ, SB) for h in range(HPG)]

    for h in range(HPG):
        q = q_ref[:, h * DH:(h + 1) * DH]
        qc_ref[h] = (q * (DH ** -0.5)).astype(BF16)
        qa_ref[h, :, :DH] = (q * (DH ** -0.5 * LOG2E)).astype(BF16)
        qa_ref[h, :, DH:] = jnp.zeros((TQ, DH), BF16)

    gates = jax.nn.sigmoid(gate_ref[...])
    gs_ref[...] = jnp.where(group == 0, gates, pltpu.roll(gates, LANES - 3 * HPG, 1))

    def gate_col(a0, col):
        return gs_ref[a0:a0 + SB, col:col + 1]

    def cmp_branch():
        lc_all = _dot_nt(qc_ref[...].reshape(R, DH), kc_ref[...])
        lc_ref[...] = (lc_all.reshape(HPG, TQ, NCP) + bc_ref[...]).reshape(R, NCP)
        c_end = lax.broadcasted_iota(jnp.int32, (SB, NCP), 1) * CMP_STRIDE + (CMP_BLOCK - 1)
        r_c = lax.broadcasted_iota(jnp.int32, (SB, NCP), 0)
        for a0 in range(0, TQ, SB):
            mc = c_end <= (t0 + a0 + r_c)
            p_heads = None
            for h in range(HPG):
                rows = slice(h * TQ + a0, h * TQ + a0 + SB)
                lc = jnp.where(mc, lc_ref[rows, :], NEG)
                pc = jnp.where(mc, jnp.exp(lc - jnp.max(lc, axis=-1, keepdims=True)), 0.0)
                den = jnp.sum(pc, axis=-1, keepdims=True)
                pc = pc * jnp.where(den > 0.0, 1.0 / den, 0.0)
                pcb_ref[rows, :] = pc.astype(BF16)
                p_heads = pc if p_heads is None else p_heads + pc
            ps_ref[a0:a0 + SB, :] = p_heads
        o_cmp = _dot(pcb_ref[...], vc_ref[...])
        for h in range(HPG):
            oacc_ref[h] = gs_ref[:, 3 * h:3 * h + 1] * o_cmp[h * TQ:(h + 1) * TQ]

    def reset():
        m_ref[...] = jnp.full(m_ref.shape, NEG, F32)
        acc_ref[...] = jnp.zeros(acc_ref.shape, F32)

    def logits(k_ref, j, bias_ref, bias_tile, s_ref):
        k = k_ref[pl.ds(pl.multiple_of(j * TK, TK), TK), :]
        qk = _dot_nt(qa_ref[...].reshape(R, 2 * DH), k)
        s_ref[...] = (qk.reshape(HPG, TQ, TK) + bias_ref[bias_tile]).reshape(R, TK)

    def update_steps(v_ref, j, s_ref):
        chunks = [slice(r0, r0 + SB) for r0 in range(0, R, SB)]

        def pass1(rows):
            m_old = m_ref[rows, :]
            m_new = jnp.maximum(m_old, jnp.max(s_ref[rows, :], axis=-1, keepdims=True))
            alpha_ref[rows, :] = jnp.exp2(m_old - m_new)
            m_ref[rows, :] = m_new

        def pass2(rows):
            v = v_ref[pl.ds(pl.multiple_of(j * TK, TK), TK), :]
            m_new = m_ref[rows, :]
            p = jnp.exp2(s_ref[rows, :] - jnp.concatenate([m_new] * (TK // LANES), axis=1))
            acc_ref[rows, :] = alpha_ref[rows, :] * acc_ref[rows, :] + _dot(p.astype(BF16), v)

        return ([functools.partial(pass1, rows) for rows in chunks]
                + [functools.partial(pass2, rows) for rows in chunks])

    def update(v_ref, j, s_ref):
        for piece in update_steps(v_ref, j, s_ref):
            piece()

    def finalize(gate_off):
        for h, a0 in subs:
            acc = acc_ref[h * TQ + a0:h * TQ + a0 + SB, :]
            o = (acc * (1.0 / pltpu.roll(acc, DH, 1)))[:, :DH]
            oacc_ref[h, a0:a0 + SB, :] += gate_col(a0, 3 * h + gate_off) * o

    reset()
    buf_a, buf_b, buf_c = sa_ref, sb_ref, sc_ref
    win_pieces = []
    for n, buf in enumerate((buf_a, buf_b, buf_c)):
        dd = N_WIN_TILES - 1 - n
        j = jnp.maximum(i - dd, 0)
        logits(kw_ref, j, btw_ref, jnp.where(i < dd, N_WIN_TILES, dd), buf)
        win_pieces += update_steps(vw_ref, j, buf)

    cmp_branch()

    imp_t = _dot_nt(ovl_ref[...], ps_ref[...], precision=lax.Precision.HIGHEST)
    s_io = lax.broadcasted_iota(jnp.int32, (DH, TQ), 0)
    jcur = (t0 + lax.broadcasted_iota(jnp.int32, (DH, TQ), 1)) >> SEL_SHIFT
    forced = (s_io == 0) | (s_io == jcur) | (s_io == jcur - 1)
    score = jnp.where(forced, 1e6, jnp.where(s_io <= jcur, imp_t, -1e6))
    sub8 = lax.broadcasted_iota(jnp.int32, (8, TQ), 0)
    cnt = [jnp.zeros((8, TQ), jnp.int32) for _ in range(DH // 8)]
    emitted = 0
    for sp in range(n_sel_blocks):
        row = score[sp:sp + 1, :]
        for g in range(DH // 8):
            blk = score[8 * g:8 * g + 8, :]
            if 8 * g > sp:
                beats = row >= blk
            elif 8 * g + 7 <= sp:
                beats = row > blk
            else:
                beats = (row > blk) | ((row == blk) & (sub8 > sp - 8 * g))
            cnt[g] = cnt[g] + jnp.where(beats, 1, 0)
        due = (sp + 1) * len(win_pieces) // n_sel_blocks
        for piece in win_pieces[emitted:due]:
            piece()
        emitted = due
    finalize(2)
    rank = jnp.concatenate(cnt, axis=0)
    drop_t = jnp.where((rank < n_top) & (s_io < n_sel_blocks), 0.0, NEG)
    drop = jnp.concatenate([jnp.zeros((DH, TQ), F32), drop_t], axis=0).T.astype(BF16)
    for h in range(HPG):
        qa_ref[h, :, DH:] = drop[:, DH:]

    def sel_logits(j, buf):
        logits(ks_ref, j, bts_ref, jnp.minimum(i - j, N_BIAS_TILES - 1), buf)

    reset()
    sel_logits(0, buf_a)

    def pipeline(j, n):
        for t in range(0, n, 2):
            sel_logits(j + t + 1, buf_b)
            update(vs_ref, j + t, buf_a)
            sel_logits(j + t + 2, buf_a)
            update(vs_ref, j + t + 1, buf_b)

    unroll = 4
    n_long = i // unroll

    def long_step(n, carry):
        pipeline(unroll * n, unroll)
        return carry

    lax.fori_loop(0, n_long, long_step, 0)
    n_pairs = (i - unroll * n_long) // 2

    def pair_step(n, carry):
        pipeline(unroll * n_long + 2 * n, 2)
        return carry

    lax.fori_loop(0, n_pairs, pair_step, 0)
    j_tail = unroll * n_long + 2 * n_pairs

    @pl.when(j_tail == i)
    def _():
        update(vs_ref, i, buf_a)

    @pl.when(j_tail < i)
    def _():
        sel_logits(i, buf_b)
        update(vs_ref, j_tail, buf_a)
        update(vs_ref, i, buf_b)

    finalize(1)

    for h in range(HPG):
        o_ref[:, h * DH:(h + 1) * DH] = oacc_ref[h].astype(o_ref.dtype)


def _nsa(proj3, kvc, ksw, vsw, bias_cmp, bias_toep, bias_win, ovl_t):
    B, S, _ = proj3.shape
    G, HPG, DH = NSA_GROUPS, NSA_HPG, NSA_DH
    GW = HPG * DH
    TQ = ATT_TILE
    NCP = kvc.shape[3]
    ns = S // SEL_BLOCK
    assert S % TQ == 0 and (HPG * TQ) % NSA_SUB == 0 and WINDOW % TQ == 0
    assert 2 * DH == LANES and ns <= DH and ovl_t.shape == (DH, NCP)
    assert C_QA % GW == 0 and C_GA % LANES == 0 and 3 * NSA_HEADS <= LANES

    def seq_spec(idx):
        return pl.BlockSpec((None, None, None, S, 2 * DH), lambda b, g, i: (idx, b, g, 0, 0))

    def cmp_spec(idx):
        return pl.BlockSpec((None, None, None, NCP, DH), lambda b, g, i: (idx, b, g, 0, 0))

    return pl.pallas_call(
        functools.partial(_nsa_body, n_sel_blocks=ns, n_top=min(SEL_TOPN, ns)),
        grid=(B, G, S // TQ),
        in_specs=[
            pl.BlockSpec((None, TQ, GW), lambda b, g, i: (b, i, C_QA // GW + g)),
            pl.BlockSpec((None, TQ, LANES), lambda b, g, i: (b, i, C_GA // LANES)),
            cmp_spec(0), cmp_spec(1),
            seq_spec(0), seq_spec(0), seq_spec(1), seq_spec(1),
            pl.BlockSpec((HPG, TQ, NCP), lambda b, g, i: (g, i, 0)),
            pl.BlockSpec((N_BIAS_TILES, HPG, TQ, TQ), lambda b, g, i: (0, g, 0, 0)),
            pl.BlockSpec((N_WIN_TILES + 1, HPG, TQ, TQ), lambda b, g, i: (0, g, 0, 0)),
            pl.BlockSpec((DH, NCP), lambda b, g, i: (0, 0)),
        ],
        out_specs=pl.BlockSpec((None, TQ, GW), lambda b, g, i: (b, i, g)),
        out_shape=jax.ShapeDtypeStruct((B, S, G * GW), BF16),
        scratch_shapes=[
            pltpu.VMEM((HPG, TQ, DH), BF16),
            pltpu.VMEM((HPG, TQ, 2 * DH), BF16),
            pltpu.VMEM((HPG * TQ, LANES), F32),
            pltpu.VMEM((HPG * TQ, 2 * DH), F32),
            pltpu.VMEM((HPG, TQ, DH), F32),
            pltpu.VMEM((HPG * TQ, TQ), F32),
            pltpu.VMEM((HPG * TQ, TQ), F32),
            pltpu.VMEM((HPG * TQ, TQ), F32),
            pltpu.VMEM((HPG * TQ, LANES), F32),
            pltpu.VMEM((TQ, LANES), F32),
            pltpu.VMEM((HPG * TQ, NCP), F32),
            pltpu.VMEM((HPG * TQ, NCP), BF16),
            pltpu.VMEM((TQ, NCP), F32),
        ],
        compiler_params=_params("parallel", "parallel", "arbitrary"),
        name="nsa",
    )(proj3, proj3, kvc, kvc, ksw, vsw, ksw, vsw, bias_cmp, bias_toep, bias_win, ovl_t)


def _gla_body(q_ref, k_ref, v_ref, r_ref, al_ref, w2_ref, ab_ref, gn_ref, o_ref, st_ref, u_ref, sb_ref):
    @pl.when(pl.program_id(1) == 0)
    def _():
        st_ref[...] = jnp.zeros_like(st_ref)

    C = GLA_CHUNK
    RB = q_ref.shape[0]
    n_chunks = RB // C
    pre = _dot(al_ref[...].astype(BF16), w2_ref[...]) + ab_ref[...]
    la = (jnp.minimum(pre, 0.0) - jnp.log(1.0 + jnp.exp(-jnp.abs(pre)))) * (1.0 / GLA_TAU)

    la_hi = la.astype(BF16)
    rest = la - la_hi.astype(F32)
    la_mid = rest.astype(BF16)
    la_lo = (rest - la_mid.astype(F32)).astype(BF16)
    r_io = lax.broadcasted_iota(jnp.int32, (C, 3 * C), 0)
    c_io = lax.broadcasted_iota(jnp.int32, (C, 3 * C), 1)
    tri3 = ((c_io & (C - 1)) <= r_io).astype(BF16)
    b_parts, bl_parts = [], []
    for c in range(n_chunks):
        rows = slice(c * C, (c + 1) * C)
        b_c = _dot(tri3, jnp.concatenate([la_hi[rows], la_mid[rows], la_lo[rows]], axis=0))
        b_parts.append(b_c)
        bl_parts.append(jnp.broadcast_to(b_c[C - 1:C, :], b_c.shape))
    b = jnp.concatenate(b_parts, axis=0)
    b_last = jnp.concatenate(bl_parts, axis=0)

    k = k_ref[...]
    q_dec = (q_ref[...] * (jnp.exp(b) * (GLA_DK ** -0.5))).astype(BF16)
    k_intra = (k * jnp.exp(-b)).astype(BF16)
    k_state = (k * jnp.exp(b_last - b)).astype(BF16)

    rr = lax.broadcasted_iota(jnp.int32, (RB, RB), 0)
    cc = lax.broadcasted_iota(jnp.int32, (RB, RB), 1)
    same_chunk_causal = (cc <= rr) & ((rr & -C) == (cc & -C))
    heads = [(h, slice(h * GLA_DK, (h + 1) * GLA_DK), slice(h * GLA_DV, (h + 1) * GLA_DV))
             for h in range(GLA_HEADS)]
    chunks = [(c, slice(c * C, (c + 1) * C)) for c in range(n_chunks)]
    for h, kc, vc in heads:
        v = v_ref[:, vc].astype(BF16)
        for c, rows in chunks:
            u_ref[h, c] = _dot_tn(k_state[rows, kc], v[rows])
    for h, kc, vc in heads:
        st = st_ref[h]
        for c, rows in chunks:
            sb_ref[h, c] = st.astype(BF16)
            decay = jnp.exp(b[(c + 1) * C - 8:(c + 1) * C, kc].T[:, 7:8])
            st = st * decay + u_ref[h, c]
        st_ref[h] = st
    for h, kc, vc in heads:
        v = v_ref[:, vc].astype(BF16)
        a = jnp.where(same_chunk_causal, _dot_nt(q_dec[:, kc], k_intra[:, kc]), 0.0)
        o_intra = _dot(a.astype(BF16), v)
        o_inter = jnp.concatenate([_dot(q_dec[rows, kc], sb_ref[h, c]) for c, rows in chunks], axis=0)
        o = _rms(o_intra + o_inter, gn_ref[:, vc])
        r = r_ref[:, vc]
        o_ref[:, vc] = (o * (r * jax.nn.sigmoid(r))).astype(o_ref.dtype)


def _gla(proj3, w2, ab, gn, layer):
    B, S, _ = proj3.shape
    RB = GLA_STEP
    HK = GLA_HEADS * GLA_DK
    HV = GLA_HEADS * GLA_DV
    assert S % RB == 0 and RB % GLA_CHUNK == 0

    def col(width, offset):
        assert offset % width == 0
        return pl.BlockSpec((None, RB, width), lambda b, s: (b, s, offset // width))

    return pl.pallas_call(
        _gla_body,
        grid=(B, S // RB),
        in_specs=[
            col(HK, C_QB), col(HK, C_KB), col(HV, C_VB), col(HV, C_RB), col(LANES, C_AL),
            pl.BlockSpec((None, LANES, HK), lambda b, s: (layer, 0, 0)),
            pl.BlockSpec((None, 1, HK), lambda b, s: (layer, 0, 0)),
            pl.BlockSpec((None, 1, HV), lambda b, s: (layer, 0, 0)),
        ],
        out_specs=pl.BlockSpec((None, RB, HV), lambda b, s: (b, s, 0)),
        out_shape=jax.ShapeDtypeStruct((B, S, HV), BF16),
        scratch_shapes=[pltpu.VMEM((GLA_HEADS, GLA_DK, GLA_DV), F32),
                        pltpu.VMEM((GLA_HEADS, RB // GLA_CHUNK, GLA_DK, GLA_DV), F32),
                        pltpu.VMEM((GLA_HEADS, RB // GLA_CHUNK, GLA_DK, GLA_DV), BF16)],
        compiler_params=_params("parallel", "arbitrary"),
        name="gla",
    )(proj3, proj3, proj3, proj3, proj3, w2, ab, gn)


def _merge_body(x_ref, oa_ref, ob_ref, gm_ref, wa_ref, wb_ref, wo_ref, o_ref):
    D = x_ref.shape[1]
    gm = gm_ref[...]
    y = (jax.nn.sigmoid(gm[:, :D]) * _dot(oa_ref[...], wa_ref[...])
         + jax.nn.sigmoid(gm[:, D:]) * _dot(ob_ref[...], wb_ref[...]))
    o_ref[...] = x_ref[...] + _dot(y.astype(BF16), wo_ref[...])


def _merge(x, o_a, o_b, proj, wa, wb, wo, layer, tm=512):
    T, D = x.shape
    DA = o_a.shape[1]
    DB = o_b.shape[1]
    assert T % tm == 0 and C_GM == 0
    return pl.pallas_call(
        _merge_body,
        grid=(T // tm,),
        in_specs=[
            pl.BlockSpec((tm, D), lambda i: (i, 0)),
            pl.BlockSpec((tm, DA), lambda i: (i, 0)),
            pl.BlockSpec((tm, DB), lambda i: (i, 0)),
            pl.BlockSpec((tm, 2 * D), lambda i: (i, 0)),
            pl.BlockSpec((None, DA, D), lambda i: (layer, 0, 0)),
            pl.BlockSpec((None, DB, D), lambda i: (layer, 0, 0)),
            pl.BlockSpec((None, D, D), lambda i: (layer, 0, 0)),
        ],
        out_specs=pl.BlockSpec((tm, D), lambda i: (i, 0)),
        out_shape=jax.ShapeDtypeStruct((T, D), F32),
        compiler_params=_params("parallel"),
        name="merge",
    )(x, o_a, o_b, proj, wa, wb, wo)


def _rel_bucket(dist):
    n = jnp.maximum(dist, 0)
    exact = REL_BUCKETS // 2
    nf = jnp.maximum(n, 1).astype(jnp.float32)
    log_b = exact + (jnp.log(nf / exact) / math.log(REL_MAX_DIST / exact)
                     * (REL_BUCKETS - exact)).astype(jnp.int32)
    return jnp.where(n < exact, n, jnp.minimum(log_b, REL_BUCKETS - 1))


def _regroup_w_in(w_in):
    widths = (NSA_HEADS * NSA_DH, 6 * NSA_GROUPS * NSA_DH, 3 * NSA_HEADS, GLA_HEADS * GLA_DK,
              GLA_HEADS * GLA_DK, GLA_HEADS * GLA_DV, GLA_RANK, GLA_HEADS * GLA_DV, 2 * D_MODEL)
    offs = np.concatenate([[0], np.cumsum(widths)])
    w = w_in.astype(BF16)
    q_a, kv_a, g_a, q_b, k_b, v_b, a_lr, r_b, g_m = (w[..., offs[n]:offs[n + 1]] for n in range(9))

    def pad(p):
        return jnp.pad(p, ((0, 0), (0, 0), (0, LANES - p.shape[-1])))

    out = jnp.concatenate([g_m, v_b, r_b, q_a, q_b, k_b, kv_a, pad(g_a), pad(a_lr)], axis=-1)
    assert out.shape[-1] == N_PROJ
    return out


def _overlap_t(ncp, nsp, nc, ns):
    c = np.arange(ncp)[None, :] * CMP_STRIDE
    s = np.arange(nsp)[:, None] * SEL_BLOCK
    ov = (c < s + SEL_BLOCK) & (c + CMP_BLOCK > s) & (np.arange(ncp)[None, :] < nc) & (np.arange(nsp)[:, None] < ns)
    return jnp.asarray(ov.astype(np.float32))


def kernel(x, rel_table, ffn1_norm, ffn1_w_gate, ffn1_w_up, ffn1_w_down, mix_norm, w_in, cmp_pos_k, cmp_pos_v, cmp_k_w1, cmp_k_w2, cmp_v_w1, cmp_v_w2, gla_a_w2, gla_a_b, gla_out_norm, w_branch_nsa, w_branch_gla, w_out, ffn2_norm, ffn2_w_gate, ffn2_w_up, ffn2_w_down, final_norm):
    B, S, D = x.shape
    L = w_in.shape[0]
    T = B * S
    G, HPG, DH = NSA_GROUPS, NSA_HPG, NSA_DH
    nch = S // CMP_STRIDE
    nc = (S - CMP_BLOCK) // CMP_STRIDE + 1
    ns = S // SEL_BLOCK
    assert D == D_MODEL and nc == nch - 1

    w1g, w1u, w1d = ffn1_w_gate.astype(BF16), ffn1_w_up.astype(BF16), ffn1_w_down.astype(BF16)
    w2g, w2u, w2d = ffn2_w_gate.astype(BF16), ffn2_w_up.astype(BF16), ffn2_w_down.astype(BF16)
    w_proj = _regroup_w_in(w_in)
    wa, wb, wo = w_branch_nsa.astype(BF16), w_branch_gla.astype(BF16), w_out.astype(BF16)
    cmp_pos = jnp.stack([cmp_pos_k, cmp_pos_v]).reshape(2, L, 1, CMP_BLOCK * DH)
    cmp_w1 = jnp.stack([cmp_k_w1, cmp_v_w1]).astype(BF16)
    cmp_w2 = jnp.stack([cmp_k_w2, cmp_v_w2]).astype(BF16)
    gla_w2 = jnp.pad(gla_a_w2, ((0, 0), (0, LANES - GLA_RANK), (0, 0))).astype(BF16)
    gla_b = gla_a_b.reshape(L, 1, -1)
    gla_gn = gla_out_norm.reshape(L, 1, -1)
    n1 = ffn1_norm.reshape(L, 1, D)
    n2 = ffn2_norm.reshape(L, 1, D)
    nm = mix_norm.reshape(L, 1, D)

    buckets = _rel_bucket(jnp.arange(REL_MAX_DIST + 1, dtype=jnp.int32))
    thr = jnp.searchsorted(buckets, jnp.arange(REL_BUCKETS, dtype=jnp.int32), side="left").astype(jnp.int32)
    bias_toep, bias_win, bias_cmp = _bias_tables(thr, rel_table, S, nch)
    ovl_t = _overlap_t(nch, DH, nc, ns)

    xf = x.reshape(T, D)
    for l in range(L):
        xf = _ffn(xf, n1, w1g, w1u, w1d, l)
        proj, xc, ksw, vsw = _proj(xf, nm, w_proj, l, S)
        proj3 = proj.reshape(B, S, N_PROJ)

        kvc = _compress(xc, cmp_pos, cmp_w1, cmp_w2, l)
        o_a = _nsa(proj3, kvc, ksw, vsw, bias_cmp, bias_toep, bias_win, ovl_t)
        o_a = o_a.reshape(T, NSA_HEADS * DH)

        o_b = _gla(proj3, gla_w2, gla_b, gla_gn, l).reshape(T, GLA_HEADS * GLA_DV)

        xf = _merge(xf, o_a, o_b, proj, wa, wb, wo, l)
        xf = _ffn(xf, n2, w2g, w2u, w2d, l,
                  final_g=final_norm.reshape(1, D) if l == L - 1 else None)
    return xf.reshape(B, S, D)
```

```python
import functools
import math

import numpy as np
import jax
import jax.numpy as jnp
from jax import lax
from jax.experimental import pallas as pl
from jax.experimental.pallas import tpu as pltpu

F32 = jnp.float32
BF16 = jnp.bfloat16

NSA_HEADS = 8
NSA_GROUPS = 2
NSA_HPG = NSA_HEADS // NSA_GROUPS
NSA_DH = 64
CMP_BLOCK = 32
CMP_STRIDE = 16
SEL_BLOCK = 64
SEL_SHIFT = 6
SEL_TOPN = 16
WINDOW = 512
GLA_HEADS = 4
GLA_DK = 128
GLA_DV = 256
GLA_RANK = 16
GLA_TAU = 16.0
GLA_CHUNK = 64
REL_BUCKETS = 32
REL_MAX_DIST = 1024
EPS = 1e-6
NEG = -1e30
LOG2E = math.log2(math.e)

LANES = 128
VMEM_LIMIT = 56 * 1024 * 1024

ATT_TILE = 256
N_BIAS_TILES = REL_MAX_DIST // ATT_TILE + 2
N_WIN_TILES = WINDOW // ATT_TILE + 1
NSA_SUB = 128
SEL_UNROLLS = (8, 4, 2)
GLA_STEP = 512

D_MODEL = 1024
C_GM = 0
C_VB = 2048
C_RB = 3072
C_QA = 4096
C_QB = 4608
C_KB = 5120
C_KV = 5632
C_GA = 6400
C_AL = 6528
N_PROJ = 6656


def _dot(a, b, precision=None):
    return lax.dot_general(a, b, (((1,), (0,)), ((), ())), precision=precision,
                           preferred_element_type=F32)


def _dot_nt(a, b, precision=None):
    return lax.dot_general(a, b, (((1,), (1,)), ((), ())), precision=precision,
                           preferred_element_type=F32)


def _dot_tn(a, b, precision=None):
    return lax.dot_general(a, b, (((0,), (0,)), ((), ())), precision=precision,
                           preferred_element_type=F32)


def _rms(x, g):
    return x * lax.rsqrt(jnp.mean(x * x, axis=-1, keepdims=True) + EPS) * g


def _params(*sem):
    return pltpu.CompilerParams(dimension_semantics=sem, vmem_limit_bytes=VMEM_LIMIT)


def _ffn_body(x_ref, g_ref, wg_ref, wu_ref, wd_ref, *rest, final, fc):
    if final:
        fg_ref, o_ref = rest
    else:
        (o_ref,) = rest
    x = x_ref[...]
    h = _rms(x, g_ref[...]).astype(BF16)
    acc = None
    for f0 in range(0, wg_ref.shape[1], fc):
        gate = _dot(h, wg_ref[:, f0:f0 + fc])
        up = _dot(h, wu_ref[:, f0:f0 + fc])
        act = (gate * jax.nn.sigmoid(gate) * up).astype(BF16)
        down = _dot(act, wd_ref[f0:f0 + fc, :])
        acc = down if acc is None else acc + down
    y = x + 0.5 * acc
    if final:
        y = _rms(y, fg_ref[...])
    o_ref[...] = y


def _ffn(x, g, wg, wu, wd, layer, final_g=None, tm=1024, fc=704):
    T, D = x.shape
    F = wg.shape[-1]
    assert T % tm == 0 and F % fc == 0
    final = final_g is not None
    resident = pl.Buffered(1)
    in_specs = [
        pl.BlockSpec((tm, D), lambda i: (i, 0)),
        pl.BlockSpec((None, 1, D), lambda i: (layer, 0, 0)),
        pl.BlockSpec((None, D, F), lambda i: (layer, 0, 0), pipeline_mode=resident),
        pl.BlockSpec((None, D, F), lambda i: (layer, 0, 0), pipeline_mode=resident),
        pl.BlockSpec((None, F, D), lambda i: (layer, 0, 0), pipeline_mode=resident),
    ]
    args = [x, g, wg, wu, wd]
    if final:
        in_specs.append(pl.BlockSpec((1, D), lambda i: (0, 0)))
        args.append(final_g)
    return pl.pallas_call(
        functools.partial(_ffn_body, final=final, fc=fc),
        grid=(T // tm,),
        in_specs=in_specs,
        out_specs=pl.BlockSpec((tm, D), lambda i: (i, 0)),
        out_shape=jax.ShapeDtypeStruct((T, D), F32),
        compiler_params=_params("parallel"),
        name="ffn",
    )(*args)


def _proj_body(x_ref, g_ref, w_ref, o_ref, xc_ref, ksw_ref, vsw_ref, stage_ref, *, nc, tiles_per_seq):
    h = _rms(x_ref[...], g_ref[...]).astype(BF16)
    for n0 in range(0, w_ref.shape[1], nc):
        o_ref[:, n0:n0 + nc] = _dot(h, w_ref[:, n0:n0 + nc])

    TM = o_ref.shape[0]
    DH = NSA_DH
    GW = NSA_GROUPS * DH

    def kv_cols(n, g):
        return slice(C_KV + n * GW + g * DH, C_KV + n * GW + (g + 1) * DH)

    t = (pl.program_id(0) % tiles_per_seq) * TM + lax.broadcasted_iota(jnp.int32, (TM, DH), 0)
    onehot = ((t >> SEL_SHIFT) == lax.broadcasted_iota(jnp.int32, (TM, DH), 1)).astype(BF16)
    ones = jnp.ones((TM, DH), BF16)
    zeros = jnp.zeros((TM, DH), BF16)
    for g in range(NSA_GROUPS):
        ksw_ref[0, g] = jnp.concatenate([o_ref[:, kv_cols(2, g)].astype(BF16), onehot], axis=1)
        vsw_ref[0, g] = jnp.concatenate([o_ref[:, kv_cols(3, g)].astype(BF16), ones], axis=1)
        ksw_ref[1, g] = jnp.concatenate([o_ref[:, kv_cols(4, g)].astype(BF16), zeros], axis=1)
        vsw_ref[1, g] = jnp.concatenate([o_ref[:, kv_cols(5, g)].astype(BF16), ones], axis=1)
    for s in range(2):
        stage_ref[s] = o_ref[:, C_KV + s * GW:C_KV + (s + 1) * GW]
        for l in range(CMP_STRIDE):
            x = stage_ref[s, pl.ds(l, TM // CMP_STRIDE, stride=CMP_STRIDE), :]
            for g in range(NSA_GROUPS):
                xc_ref[s, g, :, l * DH:(l + 1) * DH] = x[:, g * DH:(g + 1) * DH]


def _proj(x, g, w, layer, seq_len, tm=512, nc=1664):
    T, D = x.shape
    N = w.shape[-1]
    G, DH = NSA_GROUPS, NSA_DH
    B = T // seq_len
    nb = seq_len // tm
    assert T % seq_len == 0 and seq_len % tm == 0 and N % nc == 0 and tm % (8 * CMP_STRIDE) == 0

    def kv_out(width, rows):
        return pl.BlockSpec((2, None, G, rows, width), lambda i: (0, i // nb, 0, i % nb, 0))

    return pl.pallas_call(
        functools.partial(_proj_body, nc=nc, tiles_per_seq=nb),
        grid=(T // tm,),
        in_specs=[
            pl.BlockSpec((tm, D), lambda i: (i, 0)),
            pl.BlockSpec((None, 1, D), lambda i: (layer, 0, 0)),
            pl.BlockSpec((None, D, N), lambda i: (layer, 0, 0), pipeline_mode=pl.Buffered(1)),
        ],
        out_specs=[pl.BlockSpec((tm, N), lambda i: (i, 0)),
                   kv_out(CMP_STRIDE * DH, tm // CMP_STRIDE), kv_out(2 * DH, tm), kv_out(2 * DH, tm)],
        out_shape=[jax.ShapeDtypeStruct((T, N), F32),
                   jax.ShapeDtypeStruct((2, B, G, seq_len // CMP_STRIDE, CMP_STRIDE * DH), F32),
                   jax.ShapeDtypeStruct((2, B, G, seq_len, 2 * DH), BF16),
                   jax.ShapeDtypeStruct((2, B, G, seq_len, 2 * DH), BF16)],
        scratch_shapes=[pltpu.VMEM((2, tm, G * DH), F32)],
        compiler_params=_params("parallel"),
        name="proj",
    )(x, g, w)


def _compress_body(x_ref, pos_ref, w1_ref, w2_ref, o_ref):
    x = x_ref[...]
    half = x.shape[1]
    lo = (x + pos_ref[:, :half]).astype(BF16)
    hi = (x + pos_ref[:, half:]).astype(BF16)
    h_lo = _dot(lo, w1_ref[:half, :])
    h_hi = _dot(hi, w1_ref[half:, :])
    nch = x.shape[0]
    hid = h_lo + pltpu.roll(h_hi, nch - 1, 0)
    act = (hid * jax.nn.sigmoid(hid)).astype(BF16)
    o_ref[...] = _dot(act, w2_ref[...]).astype(o_ref.dtype)


def _compress(xc, pos, w1, w2, layer):
    _, B, G, NCH, CW = xc.shape
    HC = w1.shape[-1]
    dh = w2.shape[-1]
    return pl.pallas_call(
        _compress_body,
        grid=(2, B, G),
        in_specs=[
            pl.BlockSpec((None, None, None, NCH, CW), lambda s, b, g: (s, b, g, 0, 0)),
            pl.BlockSpec((None, None, 1, 2 * CW), lambda s, b, g: (s, layer, 0, 0)),
            pl.BlockSpec((None, None, 2 * CW, HC), lambda s, b, g: (s, layer, 0, 0)),
            pl.BlockSpec((None, None, HC, dh), lambda s, b, g: (s, layer, 0, 0)),
        ],
        out_specs=pl.BlockSpec((None, None, None, NCH, dh), lambda s, b, g: (s, b, g, 0, 0)),
        out_shape=jax.ShapeDtypeStruct((2, B, G, NCH, dh), BF16),
        compiler_params=_params("parallel", "parallel", "parallel"),
        name="compress",
    )(xc, pos, w1, w2)


def _bias_lookup(n, thr_ref, tab_ref):
    vals = [jnp.full(n.shape, tab_ref[0, h], F32) for h in range(NSA_HEADS)]
    for k in range(1, REL_BUCKETS):
        above = n >= thr_ref[k]
        vals = [jnp.where(above, tab_ref[k, h], v) for h, v in enumerate(vals)]
    return vals


def _toeplitz_body(thr_ref, tab_ref, o_ref, *, rows, window):
    dd = pl.program_id(0)
    T = o_ref.shape[-1]
    for r0 in range(0, T, rows):
        a = lax.broadcasted_iota(jnp.int32, (rows, T), 0) + r0
        b = lax.broadcasted_iota(jnp.int32, (rows, T), 1)
        dist = dd * T + a - b
        keep = (dist >= 0) & (dist < WINDOW) if window else dist >= 0
        for h, val in enumerate(_bias_lookup(jnp.clip(dist, 0, REL_MAX_DIST), thr_ref, tab_ref)):
            o_ref[h, r0:r0 + rows, :] = jnp.where(keep, val * LOG2E, NEG)


def _cmpbias_body(thr_ref, tab_ref, o_ref, *, rows):
    i = pl.program_id(0)
    _, TQ, NC = o_ref.shape
    for r0 in range(0, TQ, rows):
        t = lax.broadcasted_iota(jnp.int32, (rows, NC), 0) + (i * TQ + r0)
        c = lax.broadcasted_iota(jnp.int32, (rows, NC), 1)
        n = jnp.clip(t - (c * CMP_STRIDE + (CMP_BLOCK - 1)), 0, REL_MAX_DIST)
        for h, val in enumerate(_bias_lookup(n, thr_ref, tab_ref)):
            o_ref[h, r0:r0 + rows, :] = val


def _bias_tables(thr, rel_table, S, ncp):
    T = ATT_TILE
    H = NSA_HEADS
    rows = 16
    smem = pl.BlockSpec(memory_space=pltpu.SMEM)

    def toeplitz(n_tiles, window, name):
        return pl.pallas_call(
            functools.partial(_toeplitz_body, rows=rows, window=window),
            grid=(n_tiles,),
            in_specs=[smem, smem],
            out_specs=pl.BlockSpec((None, H, T, T), lambda d: (d, 0, 0, 0)),
            out_shape=jax.ShapeDtypeStruct((n_tiles, H, T, T), F32),
            compiler_params=_params("parallel"),
            name=name,
        )(thr, rel_table)

    toep = toeplitz(N_BIAS_TILES, False, "bias_toeplitz")
    toep_win = toeplitz(N_WIN_TILES + 1, True, "bias_window")
    cmpb = pl.pallas_call(
        functools.partial(_cmpbias_body, rows=rows),
        grid=(S // T,),
        in_specs=[smem, smem],
        out_specs=pl.BlockSpec((H, T, ncp), lambda i: (0, i, 0)),
        out_shape=jax.ShapeDtypeStruct((H, S, ncp), F32),
        compiler_params=_params("parallel"),
        name="bias_cmp",
    )(thr, rel_table)
    return toep, toep_win, cmpb


def _nsa_body(q_ref, gate_ref, kc_ref, vc_ref, ks_ref, vs_ref, kw_ref, vw_ref, bc_ref, bts_ref, btw_ref,
              ovl_ref, o_ref, qc_ref, qa_ref, m_ref, acc_ref, oacc_ref, sa_ref, sb_ref, sc_ref, alpha_ref,
              gs_ref, lc_ref, pcb_ref, ps_ref,
              *, n_sel_blocks, n_top):
    group = pl.program_id(1)
    i = pl.program_id(2)
    HPG, TQ, DH = qc_ref.shape
    TK = TQ
    SB = NSA_SUB
    NCP = kc_ref.shape[0]
    t0 = i * TQ
    R = HPG * TQ
    subs = [(h, a0) for a0 in range(0, TQ, SB) for h in range(HPG)]

    for h in range(HPG):
        q = q_ref[:, h * DH:(h + 1) * DH]
        qc_ref[h] = (q * (DH ** -0.5)).astype(BF16)
        qa_ref[h, :, :DH] = (q * (DH ** -0.5 * LOG2E)).astype(BF16)
        qa_ref[h, :, DH:] = jnp.zeros((TQ, DH), BF16)

    gates = jax.nn.sigmoid(gate_ref[...])
    gs_ref[...] = jnp.where(group == 0, gates, pltpu.roll(gates, LANES - 3 * HPG, 1))

    def gate_col(a0, col):
        return gs_ref[a0:a0 + SB, col:col + 1]

    def cmp_branch():
        lc_all = _dot_nt(qc_ref[...].reshape(R, DH), kc_ref[...])
        lc_ref[...] = (lc_all.reshape(HPG, TQ, NCP) + bc_ref[...]).reshape(R, NCP)
        c_end = lax.broadcasted_iota(jnp.int32, (SB, NCP), 1) * CMP_STRIDE + (CMP_BLOCK - 1)
        r_c = lax.broadcasted_iota(jnp.int32, (SB, NCP), 0)
        for a0 in range(0, TQ, SB):
            mc = c_end <= (t0 + a0 + r_c)
            p_heads = None
            for h in range(HPG):
                rows = slice(h * TQ + a0, h * TQ + a0 + SB)
                lc = jnp.where(mc, lc_ref[rows, :], NEG)
                pc = jnp.where(mc, jnp.exp(lc - jnp.max(lc, axis=-1, keepdims=True)), 0.0)
                den = jnp.sum(pc, axis=-1, keepdims=True)
                pc = pc * jnp.where(den > 0.0, 1.0 / den, 0.0)
                pcb_ref[rows, :] = pc.astype(BF16)
                p_heads = pc if p_heads is None else p_heads + pc
            ps_ref[a0:a0 + SB, :] = p_heads
        o_cmp = _dot(pcb_ref[...], vc_ref[...])
        for h in range(HPG):
            oacc_ref[h] = gs_ref[:, 3 * h:3 * h + 1] * o_cmp[h * TQ:(h + 1) * TQ]

    def reset():
        m_ref[...] = jnp.full(m_ref.shape, NEG, F32)
        acc_ref[...] = jnp.zeros(acc_ref.shape, F32)

    def logits(k_ref, j, bias_ref, bias_tile, s_ref):
        k = k_ref[pl.ds(pl.multiple_of(j * TK, TK), TK), :]
        qk = _dot_nt(qa_ref[...].reshape(R, 2 * DH), k)
        s_ref[...] = (qk.reshape(HPG, TQ, TK) + bias_ref[bias_tile]).reshape(R, TK)

    def update_steps(v_ref, j, s_ref):
        chunks = [slice(r0, r0 + SB) for r0 in range(0, R, SB)]

        def pass1(rows):
            m_old = m_ref[rows, :]
            m_new = jnp.maximum(m_old, jnp.max(s_ref[rows, :], axis=-1, keepdims=True))
            alpha_ref[rows, :] = jnp.exp2(m_old - m_new)
            m_ref[rows, :] = m_new

        def pass2(rows):
            v = v_ref[pl.ds(pl.multiple_of(j * TK, TK), TK), :]
            m_new = m_ref[rows, :]
            p = jnp.exp2(s_ref[rows, :] - jnp.concatenate([m_new] * (TK // LANES), axis=1))
            acc_ref[rows, :] = alpha_ref[rows, :] * acc_ref[rows, :] + _dot(p.astype(BF16), v)

        return ([functools.partial(pass1, rows) for rows in chunks]
                + [functools.partial(pass2, rows) for rows in chunks])

    def update(v_ref, j, s_ref):
        for piece in update_steps(v_ref, j, s_ref):
            piece()

    def finalize(gate_off):
        for h, a0 in subs:
            acc = acc_ref[h * TQ + a0:h * TQ + a0 + SB, :]
            o = (acc * (1.0 / pltpu.roll(acc, DH, 1)))[:, :DH]
            oacc_ref[h, a0:a0 + SB, :] += gate_col(a0, 3 * h + gate_off) * o

    reset()
    buf_a, buf_b, buf_c = sa_ref, sb_ref, sc_ref
    win_pieces = []
    for n, buf in enumerate((buf_a, buf_b, buf_c)):
        dd = N_WIN_TILES - 1 - n
        j = jnp.maximum(i - dd, 0)
        logits(kw_ref, j, btw_ref, jnp.where(i < dd, N_WIN_TILES, dd), buf)
        win_pieces += update_steps(vw_ref, j, buf)

    cmp_branch()

    imp_t = _dot_nt(ovl_ref[...], ps_ref[...], precision=lax.Precision.HIGHEST)
    s_io = lax.broadcasted_iota(jnp.int32, (DH, TQ), 0)
    jcur = (t0 + lax.broadcasted_iota(jnp.int32, (DH, TQ), 1)) >> SEL_SHIFT
    forced = (s_io == 0) | (s_io == jcur) | (s_io == jcur - 1)
    score = jnp.where(forced, 1e6, jnp.where(s_io <= jcur, imp_t, -1e6))
    sub8 = lax.broadcasted_iota(jnp.int32, (8, TQ), 0)
    cnt = [jnp.zeros((8, TQ), jnp.int32) for _ in range(DH // 8)]
    emitted = 0
    for sp in range(n_sel_blocks):
        row = score[sp:sp + 1, :]
        for g in range(DH // 8):
            blk = score[8 * g:8 * g + 8, :]
            if 8 * g > sp:
                beats = row >= blk
            elif 8 * g + 7 <= sp:
                beats = row > blk
            else:
                beats = (row > blk) | ((row == blk) & (sub8 > sp - 8 * g))
            cnt[g] = cnt[g] + jnp.where(beats, 1, 0)
        due = (sp + 1) * len(win_pieces) // n_sel_blocks
        for piece in win_pieces[emitted:due]:
            piece()
        emitted = due
    finalize(2)
    rank = jnp.concatenate(cnt, axis=0)
    drop_t = jnp.where((rank < n_top) & (s_io < n_sel_blocks), 0.0, NEG)
    drop = jnp.concatenate([jnp.zeros((DH, TQ), F32), drop_t], axis=0).T.astype(BF16)
    for h in range(HPG):
        qa_ref[h, :, DH:] = drop[:, DH:]

    def sel_logits(j, buf):
        logits(ks_ref, j, bts_ref, jnp.minimum(i - j, N_BIAS_TILES - 1), buf)

    reset()
    sel_logits(0, buf_a)

    def pipeline(j, n):
        for t in range(0, n, 2):
            sel_logits(j + t + 1, buf_b)
            update(vs_ref, j + t, buf_a)
            sel_logits(j + t + 2, buf_a)
            update(vs_ref, j + t + 1, buf_b)

    j_tail = 0
    for unroll in SEL_UNROLLS:
        trips = (i - j_tail) // unroll

        def body(n, carry, start=j_tail, unroll=unroll):
            pipeline(start + unroll * n, unroll)
            return carry

        lax.fori_loop(0, trips, body, 0)
        j_tail = j_tail + unroll * trips

    @pl.when(j_tail == i)
    def _():
        update(vs_ref, i, buf_a)

    @pl.when(j_tail < i)
    def _():
        sel_logits(i, buf_b)
        update(vs_ref, j_tail, buf_a)
        update(vs_ref, i, buf_b)

    finalize(1)

    for h in range(HPG):
        o_ref[:, h * DH:(h + 1) * DH] = oacc_ref[h].astype(o_ref.dtype)


def _nsa(proj3, kvc, ksw, vsw, bias_cmp, bias_toep, bias_win, ovl_t):
    B, S, _ = proj3.shape
    G, HPG, DH = NSA_GROUPS, NSA_HPG, NSA_DH
    GW = HPG * DH
    TQ = ATT_TILE
    NCP = kvc.shape[3]
    ns = S // SEL_BLOCK
    assert S % TQ == 0 and (HPG * TQ) % NSA_SUB == 0 and WINDOW % TQ == 0
    assert 2 * DH == LANES and ns <= DH and ovl_t.shape == (DH, NCP)
    assert C_QA % GW == 0 and C_GA % LANES == 0 and 3 * NSA_HEADS <= LANES

    def seq_spec(idx):
        return pl.BlockSpec((None, None, None, S, 2 * DH), lambda b, g, i: (idx, b, g, 0, 0))

    def cmp_spec(idx):
        return pl.BlockSpec((None, None, None, NCP, DH), lambda b, g, i: (idx, b, g, 0, 0))

    return pl.pallas_call(
        functools.partial(_nsa_body, n_sel_blocks=ns, n_top=min(SEL_TOPN, ns)),
        grid=(B, G, S // TQ),
        in_specs=[
            pl.BlockSpec((None, TQ, GW), lambda b, g, i: (b, i, C_QA // GW + g)),
            pl.BlockSpec((None, TQ, LANES), lambda b, g, i: (b, i, C_GA // LANES)),
            cmp_spec(0), cmp_spec(1),
            seq_spec(0), seq_spec(0), seq_spec(1), seq_spec(1),
            pl.BlockSpec((HPG, TQ, NCP), lambda b, g, i: (g, i, 0)),
            pl.BlockSpec((N_BIAS_TILES, HPG, TQ, TQ), lambda b, g, i: (0, g, 0, 0)),
            pl.BlockSpec((N_WIN_TILES + 1, HPG, TQ, TQ), lambda b, g, i: (0, g, 0, 0)),
            pl.BlockSpec((DH, NCP), lambda b, g, i: (0, 0)),
        ],
        out_specs=pl.BlockSpec((None, TQ, GW), lambda b, g, i: (b, i, g)),
        out_shape=jax.ShapeDtypeStruct((B, S, G * GW), BF16),
        scratch_shapes=[
            pltpu.VMEM((HPG, TQ, DH), BF16),
            pltpu.VMEM((HPG, TQ, 2 * DH), BF16),
            pltpu.VMEM((HPG * TQ, LANES), F32),
            pltpu.VMEM((HPG * TQ, 2 * DH), F32),
            pltpu.VMEM((HPG, TQ, DH), F32),
            pltpu.VMEM((HPG * TQ, TQ), F32),
            pltpu.VMEM((HPG * TQ, TQ), F32),
            pltpu.VMEM((HPG * TQ, TQ), F32),
            pltpu.VMEM((HPG * TQ, LANES), F32),
            pltpu.VMEM((TQ, LANES), F32),
            pltpu.VMEM((HPG * TQ, NCP), F32),
            pltpu.VMEM((HPG * TQ, NCP), BF16),
            pltpu.VMEM((TQ, NCP), F32),
        ],
        compiler_params=_params("parallel", "parallel", "arbitrary"),
        name="nsa",
    )(proj3, proj3, kvc, kvc, ksw, vsw, ksw, vsw, bias_cmp, bias_toep, bias_win, ovl_t)


def _gla_body(q_ref, k_ref, v_ref, r_ref, al_ref, w2_ref, ab_ref, gn_ref, o_ref, st_ref, u_ref, sb_ref):
    @pl.when(pl.program_id(1) == 0)
    def _():
        st_ref[...] = jnp.zeros_like(st_ref)

    C = GLA_CHUNK
    RB = q_ref.shape[0]
    n_chunks = RB // C
    pre = _dot(al_ref[...].astype(BF16), w2_ref[...]) + ab_ref[...]
    la = (jnp.minimum(pre, 0.0) - jnp.log(1.0 + jnp.exp(-jnp.abs(pre)))) * (1.0 / GLA_TAU)

    la_hi = la.astype(BF16)
    rest = la - la_hi.astype(F32)
    la_mid = rest.astype(BF16)
    la_lo = (rest - la_mid.astype(F32)).astype(BF16)
    r_io = lax.broadcasted_iota(jnp.int32, (C, 3 * C), 0)
    c_io = lax.broadcasted_iota(jnp.int32, (C, 3 * C), 1)
    tri3 = ((c_io & (C - 1)) <= r_io).astype(BF16)
    b_parts, bl_parts = [], []
    for c in range(n_chunks):
        rows = slice(c * C, (c + 1) * C)
        b_c = _dot(tri3, jnp.concatenate([la_hi[rows], la_mid[rows], la_lo[rows]], axis=0))
        b_parts.append(b_c)
        bl_parts.append(jnp.broadcast_to(b_c[C - 1:C, :], b_c.shape))
    b = jnp.concatenate(b_parts, axis=0)
    b_last = jnp.concatenate(bl_parts, axis=0)

    k = k_ref[...]
    q_dec = (q_ref[...] * (jnp.exp(b) * (GLA_DK ** -0.5))).astype(BF16)
    k_intra = (k * jnp.exp(-b)).astype(BF16)
    k_state = (k * jnp.exp(b_last - b)).astype(BF16)

    rr = lax.broadcasted_iota(jnp.int32, (RB, RB), 0)
    cc = lax.broadcasted_iota(jnp.int32, (RB, RB), 1)
    same_chunk_causal = (cc <= rr) & ((rr & -C) == (cc & -C))
    heads = [(h, slice(h * GLA_DK, (h + 1) * GLA_DK), slice(h * GLA_DV, (h + 1) * GLA_DV))
             for h in range(GLA_HEADS)]
    chunks = [(c, slice(c * C, (c + 1) * C)) for c in range(n_chunks)]
    for h, kc, vc in heads:
        v = v_ref[:, vc].astype(BF16)
        for c, rows in chunks:
            u_ref[h, c] = _dot_tn(k_state[rows, kc], v[rows])
    for h, kc, vc in heads:
        st = st_ref[h]
        for c, rows in chunks:
            sb_ref[h, c] = st.astype(BF16)
            decay = jnp.exp(b[(c + 1) * C - 8:(c + 1) * C, kc].T[:, 7:8])
            st = st * decay + u_ref[h, c]
        st_ref[h] = st
    for h, kc, vc in heads:
        v = v_ref[:, vc].astype(BF16)
        a = jnp.where(same_chunk_causal, _dot_nt(q_dec[:, kc], k_intra[:, kc]), 0.0)
        o_intra = _dot(a.astype(BF16), v)
        o_inter = jnp.concatenate([_dot(q_dec[rows, kc], sb_ref[h, c]) for c, rows in chunks], axis=0)
        o = _rms(o_intra + o_inter, gn_ref[:, vc])
        r = r_ref[:, vc]
        o_ref[:, vc] = (o * (r * jax.nn.sigmoid(r))).astype(o_ref.dtype)


def _gla(proj3, w2, ab, gn, layer):
    B, S, _ = proj3.shape
    RB = GLA_STEP
    HK = GLA_HEADS * GLA_DK
    HV = GLA_HEADS * GLA_DV
    assert S % RB == 0 and RB % GLA_CHUNK == 0

    def col(width, offset):
        assert offset % width == 0
        return pl.BlockSpec((None, RB, width), lambda b, s: (b, s, offset // width))

    return pl.pallas_call(
        _gla_body,
        grid=(B, S // RB),
        in_specs=[
            col(HK, C_QB), col(HK, C_KB), col(HV, C_VB), col(HV, C_RB), col(LANES, C_AL),
            pl.BlockSpec((None, LANES, HK), lambda b, s: (layer, 0, 0)),
            pl.BlockSpec((None, 1, HK), lambda b, s: (layer, 0, 0)),
            pl.BlockSpec((None, 1, HV), lambda b, s: (layer, 0, 0)),
        ],
        out_specs=pl.BlockSpec((None, RB, HV), lambda b, s: (b, s, 0)),
        out_shape=jax.ShapeDtypeStruct((B, S, HV), BF16),
        scratch_shapes=[pltpu.VMEM((GLA_HEADS, GLA_DK, GLA_DV), F32),
                        pltpu.VMEM((GLA_HEADS, RB // GLA_CHUNK, GLA_DK, GLA_DV), F32),
                        pltpu.VMEM((GLA_HEADS, RB // GLA_CHUNK, GLA_DK, GLA_DV), BF16)],
        compiler_params=_params("parallel", "arbitrary"),
        name="gla",
    )(proj3, proj3, proj3, proj3, proj3, w2, ab, gn)


def _merge_body(x_ref, oa_ref, ob_ref, gm_ref, wa_ref, wb_ref, wo_ref, o_ref):
    D = x_ref.shape[1]
    gm = gm_ref[...]
    y = (jax.nn.sigmoid(gm[:, :D]) * _dot(oa_ref[...], wa_ref[...])
         + jax.nn.sigmoid(gm[:, D:]) * _dot(ob_ref[...], wb_ref[...]))
    o_ref[...] = x_ref[...] + _dot(y.astype(BF16), wo_ref[...])


def _merge(x, o_a, o_b, proj, wa, wb, wo, layer, tm=512):
    T, D = x.shape
    DA = o_a.shape[1]
    DB = o_b.shape[1]
    assert T % tm == 0 and C_GM == 0
    return pl.pallas_call(
        _merge_body,
        grid=(T // tm,),
        in_specs=[
            pl.BlockSpec((tm, D), lambda i: (i, 0)),
            pl.BlockSpec((tm, DA), lambda i: (i, 0)),
            pl.BlockSpec((tm, DB), lambda i: (i, 0)),
            pl.BlockSpec((tm, 2 * D), lambda i: (i, 0)),
            pl.BlockSpec((None, DA, D), lambda i: (layer, 0, 0)),
            pl.BlockSpec((None, DB, D), lambda i: (layer, 0, 0)),
            pl.BlockSpec((None, D, D), lambda i: (layer, 0, 0)),
        ],
        out_specs=pl.BlockSpec((tm, D), lambda i: (i, 0)),
        out_shape=jax.ShapeDtypeStruct((T, D), F32),
        compiler_params=_params("parallel"),
        name="merge",
    )(x, o_a, o_b, proj, wa, wb, wo)


def _rel_bucket(dist):
    n = jnp.maximum(dist, 0)
    exact = REL_BUCKETS // 2
    nf = jnp.maximum(n, 1).astype(jnp.float32)
    log_b = exact + (jnp.log(nf / exact) / math.log(REL_MAX_DIST / exact)
                     * (REL_BUCKETS - exact)).astype(jnp.int32)
    return jnp.where(n < exact, n, jnp.minimum(log_b, REL_BUCKETS - 1))


def _regroup_w_in(w_in):
    widths = (NSA_HEADS * NSA_DH, 6 * NSA_GROUPS * NSA_DH, 3 * NSA_HEADS, GLA_HEADS * GLA_DK,
              GLA_HEADS * GLA_DK, GLA_HEADS * GLA_DV, GLA_RANK, GLA_HEADS * GLA_DV, 2 * D_MODEL)
    offs = np.concatenate([[0], np.cumsum(widths)])
    w = w_in.astype(BF16)
    q_a, kv_a, g_a, q_b, k_b, v_b, a_lr, r_b, g_m = (w[..., offs[n]:offs[n + 1]] for n in range(9))

    def pad(p):
        return jnp.pad(p, ((0, 0), (0, 0), (0, LANES - p.shape[-1])))

    out = jnp.concatenate([g_m, v_b, r_b, q_a, q_b, k_b, kv_a, pad(g_a), pad(a_lr)], axis=-1)
    assert out.shape[-1] == N_PROJ
    return out


def _overlap_t(ncp, nsp, nc, ns):
    c = np.arange(ncp)[None, :] * CMP_STRIDE
    s = np.arange(nsp)[:, None] * SEL_BLOCK
    ov = (c < s + SEL_BLOCK) & (c + CMP_BLOCK > s) & (np.arange(ncp)[None, :] < nc) & (np.arange(nsp)[:, None] < ns)
    return jnp.asarray(ov.astype(np.float32))


def kernel(x, rel_table, ffn1_norm, ffn1_w_gate, ffn1_w_up, ffn1_w_down, mix_norm, w_in, cmp_pos_k, cmp_pos_v, cmp_k_w1, cmp_k_w2, cmp_v_w1, cmp_v_w2, gla_a_w2, gla_a_b, gla_out_norm, w_branch_nsa, w_branch_gla, w_out, ffn2_norm, ffn2_w_gate, ffn2_w_up, ffn2_w_down, final_norm):
    B, S, D = x.shape
    L = w_in.shape[0]
    T = B * S
    G, HPG, DH = NSA_GROUPS, NSA_HPG, NSA_DH
    nch = S // CMP_STRIDE
    nc = (S - CMP_BLOCK) // CMP_STRIDE + 1
    ns = S // SEL_BLOCK
    assert D == D_MODEL and nc == nch - 1

    w1g, w1u, w1d = ffn1_w_gate.astype(BF16), ffn1_w_up.astype(BF16), ffn1_w_down.astype(BF16)
    w2g, w2u, w2d = ffn2_w_gate.astype(BF16), ffn2_w_up.astype(BF16), ffn2_w_down.astype(BF16)
    w_proj = _regroup_w_in(w_in)
    wa, wb, wo = w_branch_nsa.astype(BF16), w_branch_gla.astype(BF16), w_out.astype(BF16)
    cmp_pos = jnp.stack([cmp_pos_k, cmp_pos_v]).reshape(2, L, 1, CMP_BLOCK * DH)
    cmp_w1 = jnp.stack([cmp_k_w1, cmp_v_w1]).astype(BF16)
    cmp_w2 = jnp.stack([cmp_k_w2, cmp_v_w2]).astype(BF16)
    gla_w2 = jnp.pad(gla_a_w2, ((0, 0), (0, LANES - GLA_RANK), (0, 0))).astype(BF16)
    gla_b = gla_a_b.reshape(L, 1, -1)
    gla_gn = gla_out_norm.reshape(L, 1, -1)
    n1 = ffn1_norm.reshape(L, 1, D)
    n2 = ffn2_norm.reshape(L, 1, D)
    nm = mix_norm.reshape(L, 1, D)

    buckets = _rel_bucket(jnp.arange(REL_MAX_DIST + 1, dtype=jnp.int32))
    thr = jnp.searchsorted(buckets, jnp.arange(REL_BUCKETS, dtype=jnp.int32), side="left").astype(jnp.int32)
    bias_toep, bias_win, bias_cmp = _bias_tables(thr, rel_table, S, nch)
    ovl_t = _overlap_t(nch, DH, nc, ns)

    xf = x.reshape(T, D)
    for l in range(L):
        xf = _ffn(xf, n1, w1g, w1u, w1d, l)
        proj, xc, ksw, vsw = _proj(xf, nm, w_proj, l, S)
        proj3 = proj.reshape(B, S, N_PROJ)

        kvc = _compress(xc, cmp_pos, cmp_w1, cmp_w2, l)
        o_a = _nsa(proj3, kvc, ksw, vsw, bias_cmp, bias_toep, bias_win, ovl_t)
        o_a = o_a.reshape(T, NSA_HEADS * DH)

        o_b = _gla(proj3, gla_w2, gla_b, gla_gn, l).reshape(T, GLA_HEADS * GLA_DV)

        xf = _merge(xf, o_a, o_b, proj, wa, wb, wo, l)
        xf = _ffn(xf, n2, w2g, w2u, w2d, l,
                  final_g=final_norm.reshape(1, D) if l == L - 1 else None)
    return xf.reshape(B, S, D)
```

```python
import functools
import math

import numpy as np
import jax
import jax.numpy as jnp
from jax import lax
from jax.experimental import pallas as pl
from jax.experimental.pallas import tpu as pltpu

F32 = jnp.float32
BF16 = jnp.bfloat16

NSA_HEADS = 8
NSA_GROUPS = 2
NSA_HPG = NSA_HEADS // NSA_GROUPS
NSA_DH = 64
CMP_BLOCK = 32
CMP_STRIDE = 16
SEL_BLOCK = 64
SEL_SHIFT = 6
SEL_TOPN = 16
WINDOW = 512
GLA_HEADS = 4
GLA_DK = 128
GLA_DV = 256
GLA_RANK = 16
GLA_TAU = 16.0
GLA_CHUNK = 64
REL_BUCKETS = 32
REL_MAX_DIST = 1024
EPS = 1e-6
NEG = -1e30
LOG2E = math.log2(math.e)

LANES = 128
VMEM_LIMIT = 56 * 1024 * 1024

ATT_TILE = 256
N_BIAS_TILES = REL_MAX_DIST // ATT_TILE + 2
N_WIN_TILES = WINDOW // ATT_TILE + 1
NSA_SUB = 128
SEL_UNROLLS = (8, 4, 2)
GLA_STEP = 512

D_MODEL = 1024
C_GM = 0
C_VB = 2048
C_RB = 3072
C_QA = 4096
C_QB = 4608
C_KB = 5120
C_KV = 5632
C_GA = 6400
C_AL = 6528
N_PROJ = 6656


def _dot(a, b, precision=None):
    return lax.dot_general(a, b, (((1,), (0,)), ((), ())), precision=precision,
                           preferred_element_type=F32)


def _dot_nt(a, b, precision=None):
    return lax.dot_general(a, b, (((1,), (1,)), ((), ())), precision=precision,
                           preferred_element_type=F32)


def _dot_tn(a, b, precision=None):
    return lax.dot_general(a, b, (((0,), (0,)), ((), ())), precision=precision,
                           preferred_element_type=F32)


def _rms(x, g):
    return x * lax.rsqrt(jnp.mean(x * x, axis=-1, keepdims=True) + EPS) * g


def _params(*sem):
    return pltpu.CompilerParams(dimension_semantics=sem, vmem_limit_bytes=VMEM_LIMIT)


def _ffn_body(x_ref, g_ref, wg_ref, wu_ref, wd_ref, *rest, final, fc):
    if final:
        fg_ref, o_ref = rest
    else:
        (o_ref,) = rest
    x = x_ref[...]
    h = _rms(x, g_ref[...]).astype(BF16)
    acc = None
    for f0 in range(0, wg_ref.shape[1], fc):
        gate = _dot(h, wg_ref[:, f0:f0 + fc])
        up = _dot(h, wu_ref[:, f0:f0 + fc])
        act = (gate * jax.nn.sigmoid(gate) * up).astype(BF16)
        down = _dot(act, wd_ref[f0:f0 + fc, :])
        acc = down if acc is None else acc + down
    y = x + 0.5 * acc
    if final:
        y = _rms(y, fg_ref[...])
    o_ref[...] = y


def _ffn(x, g, wg, wu, wd, layer, final_g=None, tm=1024, fc=704):
    T, D = x.shape
    F = wg.shape[-1]
    assert T % tm == 0 and F % fc == 0
    final = final_g is not None
    resident = pl.Buffered(1)
    in_specs = [
        pl.BlockSpec((tm, D), lambda i: (i, 0)),
        pl.BlockSpec((None, 1, D), lambda i: (layer, 0, 0)),
        pl.BlockSpec((None, D, F), lambda i: (layer, 0, 0), pipeline_mode=resident),
        pl.BlockSpec((None, D, F), lambda i: (layer, 0, 0), pipeline_mode=resident),
        pl.BlockSpec((None, F, D), lambda i: (layer, 0, 0), pipeline_mode=resident),
    ]
    args = [x, g, wg, wu, wd]
    if final:
        in_specs.append(pl.BlockSpec((1, D), lambda i: (0, 0)))
        args.append(final_g)
    return pl.pallas_call(
        functools.partial(_ffn_body, final=final, fc=fc),
        grid=(T // tm,),
        in_specs=in_specs,
        out_specs=pl.BlockSpec((tm, D), lambda i: (i, 0)),
        out_shape=jax.ShapeDtypeStruct((T, D), F32),
        compiler_params=_params("parallel"),
        name="ffn",
    )(*args)


def _proj_body(x_ref, g_ref, w_ref, o_ref, xc_ref, ksw_ref, vsw_ref, stage_ref, *, nc, tiles_per_seq):
    h = _rms(x_ref[...], g_ref[...]).astype(BF16)
    for n0 in range(0, w_ref.shape[1], nc):
        o_ref[:, n0:n0 + nc] = _dot(h, w_ref[:, n0:n0 + nc])

    TM = o_ref.shape[0]
    DH = NSA_DH
    GW = NSA_GROUPS * DH

    def kv_cols(n, g):
        return slice(C_KV + n * GW + g * DH, C_KV + n * GW + (g + 1) * DH)

    t = (pl.program_id(0) % tiles_per_seq) * TM + lax.broadcasted_iota(jnp.int32, (TM, DH), 0)
    onehot = ((t >> SEL_SHIFT) == lax.broadcasted_iota(jnp.int32, (TM, DH), 1)).astype(BF16)
    ones = jnp.ones((TM, DH), BF16)
    zeros = jnp.zeros((TM, DH), BF16)
    for g in range(NSA_GROUPS):
        ksw_ref[0, g] = jnp.concatenate([o_ref[:, kv_cols(2, g)].astype(BF16), onehot], axis=1)
        vsw_ref[0, g] = jnp.concatenate([o_ref[:, kv_cols(3, g)].astype(BF16), ones], axis=1)
        ksw_ref[1, g] = jnp.concatenate([o_ref[:, kv_cols(4, g)].astype(BF16), zeros], axis=1)
        vsw_ref[1, g] = jnp.concatenate([o_ref[:, kv_cols(5, g)].astype(BF16), ones], axis=1)
    for s in range(2):
        stage_ref[s] = o_ref[:, C_KV + s * GW:C_KV + (s + 1) * GW]
        for l in range(CMP_STRIDE):
            x = stage_ref[s, pl.ds(l, TM // CMP_STRIDE, stride=CMP_STRIDE), :]
            for g in range(NSA_GROUPS):
                xc_ref[s, g, :, l * DH:(l + 1) * DH] = x[:, g * DH:(g + 1) * DH]


def _proj(x, g, w, layer, seq_len, tm=512, nc=1664):
    T, D = x.shape
    N = w.shape[-1]
    G, DH = NSA_GROUPS, NSA_DH
    B = T // seq_len
    nb = seq_len // tm
    assert T % seq_len == 0 and seq_len % tm == 0 and N % nc == 0 and tm % (8 * CMP_STRIDE) == 0

    def kv_out(width, rows):
        return pl.BlockSpec((2, None, G, rows, width), lambda i: (0, i // nb, 0, i % nb, 0))

    return pl.pallas_call(
        functools.partial(_proj_body, nc=nc, tiles_per_seq=nb),
        grid=(T // tm,),
        in_specs=[
            pl.BlockSpec((tm, D), lambda i: (i, 0)),
            pl.BlockSpec((None, 1, D), lambda i: (layer, 0, 0)),
            pl.BlockSpec((None, D, N), lambda i: (layer, 0, 0), pipeline_mode=pl.Buffered(1)),
        ],
        out_specs=[pl.BlockSpec((tm, N), lambda i: (i, 0)),
                   kv_out(CMP_STRIDE * DH, tm // CMP_STRIDE), kv_out(2 * DH, tm), kv_out(2 * DH, tm)],
        out_shape=[jax.ShapeDtypeStruct((T, N), F32),
                   jax.ShapeDtypeStruct((2, B, G, seq_len // CMP_STRIDE, CMP_STRIDE * DH), F32),
                   jax.ShapeDtypeStruct((2, B, G, seq_len, 2 * DH), BF16),
                   jax.ShapeDtypeStruct((2, B, G, seq_len, 2 * DH), BF16)],
        scratch_shapes=[pltpu.VMEM((2, tm, G * DH), F32)],
        compiler_params=_params("parallel"),
        name="proj",
    )(x, g, w)


def _compress_body(x_ref, pos_ref, w1_ref, w2_ref, o_ref):
    x = x_ref[...]
    half = x.shape[1]
    lo = (x + pos_ref[:, :half]).astype(BF16)
    hi = (x + pos_ref[:, half:]).astype(BF16)
    h_lo = _dot(lo, w1_ref[:half, :])
    h_hi = _dot(hi, w1_ref[half:, :])
    nch = x.shape[0]
    hid = h_lo + pltpu.roll(h_hi, nch - 1, 0)
    act = (hid * jax.nn.sigmoid(hid)).astype(BF16)
    o_ref[...] = _dot(act, w2_ref[...]).astype(o_ref.dtype)


def _compress(xc, pos, w1, w2, layer):
    _, B, G, NCH, CW = xc.shape
    HC = w1.shape[-1]
    dh = w2.shape[-1]
    return pl.pallas_call(
        _compress_body,
        grid=(2, B, G),
        in_specs=[
            pl.BlockSpec((None, None, None, NCH, CW), lambda s, b, g: (s, b, g, 0, 0)),
            pl.BlockSpec((None, None, 1, 2 * CW), lambda s, b, g: (s, layer, 0, 0)),
            pl.BlockSpec((None, None, 2 * CW, HC), lambda s, b, g: (s, layer, 0, 0)),
            pl.BlockSpec((None, None, HC, dh), lambda s, b, g: (s, layer, 0, 0)),
        ],
        out_specs=pl.BlockSpec((None, None, None, NCH, dh), lambda s, b, g: (s, b, g, 0, 0)),
        out_shape=jax.ShapeDtypeStruct((2, B, G, NCH, dh), BF16),
        compiler_params=_params("parallel", "parallel", "parallel"),
        name="compress",
    )(xc, pos, w1, w2)


def _bias_lookup(n, thr_ref, tab_ref):
    vals = [jnp.full(n.shape, tab_ref[0, h], F32) for h in range(NSA_HEADS)]
    for k in range(1, REL_BUCKETS):
        above = n >= thr_ref[k]
        vals = [jnp.where(above, tab_ref[k, h], v) for h, v in enumerate(vals)]
    return vals


def _toeplitz_body(thr_ref, tab_ref, o_ref, *, rows, window):
    dd = pl.program_id(0)
    T = o_ref.shape[-1]
    k = lax.broadcasted_iota(jnp.int32, (8, 2 * T), 1)
    dist = dd * T + (T - 1) - k
    keep = (dist >= 0) & (dist < WINDOW) if window else dist >= 0
    for h, val in enumerate(_bias_lookup(jnp.clip(dist, 0, REL_MAX_DIST), thr_ref, tab_ref)):
        row = jnp.where(keep, val * LOG2E, NEG)[0:1, :]
        for r0 in range(0, T, rows):
            rot = pltpu.roll(jnp.broadcast_to(row, (rows, 2 * T)), T + 1 + r0, 1, stride=1, stride_axis=0)
            o_ref[h, r0:r0 + rows, :] = rot[:, :T]


def _cmpbias_body(thr_ref, tab_ref, o_ref, stage_ref):
    i = pl.program_id(0)
    H, TQ, NC = o_ref.shape
    RES = CMP_STRIDE
    MR = TQ // RES
    W = NC + LANES
    res = lax.broadcasted_iota(jnp.int32, (RES, W), 0)
    k = lax.broadcasted_iota(jnp.int32, (RES, W), 1)
    n = CMP_STRIDE * (i * MR + MR - 1 - k) + res - (CMP_BLOCK - 1)
    for h, val in enumerate(_bias_lookup(jnp.clip(n, 0, REL_MAX_DIST), thr_ref, tab_ref)):
        for r in range(RES):
            rot = pltpu.roll(jnp.broadcast_to(val[r:r + 1, :], (MR, W)), W - (MR - 1), 1,
                             stride=1, stride_axis=0)
            for s in range(NC // LANES):
                stage_ref[s, pl.ds(r, MR, stride=RES), :] = rot[:, s * LANES:(s + 1) * LANES]
        for s in range(NC // LANES):
            o_ref[h, :, s * LANES:(s + 1) * LANES] = stage_ref[s]


def _bias_tables(thr, rel_table, S, ncp):
    T = ATT_TILE
    H = NSA_HEADS
    rows = 64
    assert ncp % LANES == 0 and T % CMP_STRIDE == 0 and T // CMP_STRIDE - 1 <= LANES
    smem = pl.BlockSpec(memory_space=pltpu.SMEM)

    def toeplitz(n_tiles, window, name):
        return pl.pallas_call(
            functools.partial(_toeplitz_body, rows=rows, window=window),
            grid=(n_tiles,),
            in_specs=[smem, smem],
            out_specs=pl.BlockSpec((None, H, T, T), lambda d: (d, 0, 0, 0)),
            out_shape=jax.ShapeDtypeStruct((n_tiles, H, T, T), F32),
            compiler_params=_params("parallel"),
            name=name,
        )(thr, rel_table)

    toep = toeplitz(N_BIAS_TILES, False, "bias_toeplitz")
    toep_win = toeplitz(N_WIN_TILES + 1, True, "bias_window")
    cmpb = pl.pallas_call(
        _cmpbias_body,
        grid=(S // T,),
        in_specs=[smem, smem],
        out_specs=pl.BlockSpec((H, T, ncp), lambda i: (0, i, 0)),
        out_shape=jax.ShapeDtypeStruct((H, S, ncp), F32),
        scratch_shapes=[pltpu.VMEM((ncp // LANES, T, LANES), F32)],
        compiler_params=_params("parallel"),
        name="bias_cmp",
    )(thr, rel_table)
    return toep, toep_win, cmpb


def _nsa_body(q_ref, gate_ref, kc_ref, vc_ref, ks_ref, vs_ref, kw_ref, vw_ref, bc_ref, bts_ref, btw_ref,
              ovl_ref, o_ref, qc_ref, qa_ref, m_ref, acc_ref, oacc_ref, sa_ref, sb_ref, sc_ref, alpha_ref,
              gs_ref, lc_ref, pcb_ref, ps_ref,
              *, n_sel_blocks, n_top):
    group = pl.program_id(1)
    i = pl.program_id(2)
    HPG, TQ, DH = qc_ref.shape
    TK = TQ
    SB = NSA_SUB
    NCP = kc_ref.shape[0]
    t0 = i * TQ
    R = HPG * TQ
    subs = [(h, a0) for a0 in range(0, TQ, SB) for h in range(HPG)]

    for h in range(HPG):
        q = q_ref[:, h * DH:(h + 1) * DH]
        qc_ref[h] = (q * (DH ** -0.5)).astype(BF16)
        qa_ref[h, :, :DH] = (q * (DH ** -0.5 * LOG2E)).astype(BF16)
        qa_ref[h, :, DH:] = jnp.zeros((TQ, DH), BF16)

    gates = jax.nn.sigmoid(gate_ref[...])
    gs_ref[...] = jnp.where(group == 0, gates, pltpu.roll(gates, LANES - 3 * HPG, 1))

    def gate_col(a0, col):
        return gs_ref[a0:a0 + SB, col:col + 1]

    def cmp_branch():
        lc_all = _dot_nt(qc_ref[...].reshape(R, DH), kc_ref[...])
        lc_ref[...] = (lc_all.reshape(HPG, TQ, NCP) + bc_ref[...]).reshape(R, NCP)
        c_end = lax.broadcasted_iota(jnp.int32, (SB, NCP), 1) * CMP_STRIDE + (CMP_BLOCK - 1)
        r_c = lax.broadcasted_iota(jnp.int32, (SB, NCP), 0)
        for a0 in range(0, TQ, SB):
            mc = c_end <= (t0 + a0 + r_c)
            p_heads = None
            for h in range(HPG):
                rows = slice(h * TQ + a0, h * TQ + a0 + SB)
                lc = jnp.where(mc, lc_ref[rows, :], NEG)
                pc = jnp.where(mc, jnp.exp(lc - jnp.max(lc, axis=-1, keepdims=True)), 0.0)
                den = jnp.sum(pc, axis=-1, keepdims=True)
                pc = pc * jnp.where(den > 0.0, 1.0 / den, 0.0)
                pcb_ref[rows, :] = pc.astype(BF16)
                p_heads = pc if p_heads is None else p_heads + pc
            ps_ref[a0:a0 + SB, :] = p_heads
        o_cmp = _dot(pcb_ref[...], vc_ref[...])
        for h in range(HPG):
            oacc_ref[h] = gs_ref[:, 3 * h:3 * h + 1] * o_cmp[h * TQ:(h + 1) * TQ]

    def reset():
        m_ref[...] = jnp.full(m_ref.shape, NEG, F32)
        acc_ref[...] = jnp.zeros(acc_ref.shape, F32)

    def logits(k_ref, j, bias_ref, bias_tile, s_ref):
        k = k_ref[pl.ds(pl.multiple_of(j * TK, TK), TK), :]
        qk = _dot_nt(qa_ref[...].reshape(R, 2 * DH), k)
        s_ref[...] = (qk.reshape(HPG, TQ, TK) + bias_ref[bias_tile]).reshape(R, TK)

    def update_steps(v_ref, j, s_ref):
        chunks = [slice(r0, r0 + SB) for r0 in range(0, R, SB)]

        def pass1(rows):
            m_old = m_ref[rows, :]
            m_new = jnp.maximum(m_old, jnp.max(s_ref[rows, :], axis=-1, keepdims=True))
            alpha_ref[rows, :] = jnp.exp2(m_old - m_new)
            m_ref[rows, :] = m_new

        def pass2(rows):
            v = v_ref[pl.ds(pl.multiple_of(j * TK, TK), TK), :]
            m_new = m_ref[rows, :]
            p = jnp.exp2(s_ref[rows, :] - jnp.concatenate([m_new] * (TK // LANES), axis=1))
            acc_ref[rows, :] = alpha_ref[rows, :] * acc_ref[rows, :] + _dot(p.astype(BF16), v)

        return ([functools.partial(pass1, rows) for rows in chunks]
                + [functools.partial(pass2, rows) for rows in chunks])

    def update(v_ref, j, s_ref):
        for piece in update_steps(v_ref, j, s_ref):
            piece()

    def finalize(gate_off):
        for h, a0 in subs:
            acc = acc_ref[h * TQ + a0:h * TQ + a0 + SB, :]
            o = (acc * (1.0 / pltpu.roll(acc, DH, 1)))[:, :DH]
            oacc_ref[h, a0:a0 + SB, :] += gate_col(a0, 3 * h + gate_off) * o

    reset()
    buf_a, buf_b, buf_c = sa_ref, sb_ref, sc_ref
    win_pieces = []
    for n, buf in enumerate((buf_a, buf_b, buf_c)):
        dd = N_WIN_TILES - 1 - n
        j = jnp.maximum(i - dd, 0)
        logits(kw_ref, j, btw_ref, jnp.where(i < dd, N_WIN_TILES, dd), buf)
        win_pieces += update_steps(vw_ref, j, buf)

    cmp_branch()

    imp_t = _dot_nt(ovl_ref[...], ps_ref[...], precision=lax.Precision.HIGHEST)
    s_io = lax.broadcasted_iota(jnp.int32, (DH, TQ), 0)
    jcur = (t0 + lax.broadcasted_iota(jnp.int32, (DH, TQ), 1)) >> SEL_SHIFT
    forced = (s_io == 0) | (s_io == jcur) | (s_io == jcur - 1)
    score = jnp.where(forced, 1e6, jnp.where(s_io <= jcur, imp_t, -1e6))
    sub8 = lax.broadcasted_iota(jnp.int32, (8, TQ), 0)
    cnt = [jnp.zeros((8, TQ), jnp.int32) for _ in range(DH // 8)]
    emitted = 0
    for sp in range(n_sel_blocks):
        row = score[sp:sp + 1, :]
        for g in range(DH // 8):
            blk = score[8 * g:8 * g + 8, :]
            if 8 * g > sp:
                beats = row >= blk
            elif 8 * g + 7 <= sp:
                beats = row > blk
            else:
                beats = (row > blk) | ((row == blk) & (sub8 > sp - 8 * g))
            cnt[g] = cnt[g] + jnp.where(beats, 1, 0)
        due = (sp + 1) * len(win_pieces) // n_sel_blocks
        for piece in win_pieces[emitted:due]:
            piece()
        emitted = due
    finalize(2)
    rank = jnp.concatenate(cnt, axis=0)
    drop_t = jnp.where((rank < n_top) & (s_io < n_sel_blocks), 0.0, NEG)
    drop = jnp.concatenate([jnp.zeros((DH, TQ), F32), drop_t], axis=0).T.astype(BF16)
    for h in range(HPG):
        qa_ref[h, :, DH:] = drop[:, DH:]

    def sel_logits(j, buf):
        logits(ks_ref, j, bts_ref, jnp.minimum(i - j, N_BIAS_TILES - 1), buf)

    reset()
    sel_logits(0, buf_a)

    def pipeline(j, n):
        for t in range(0, n, 2):
            sel_logits(j + t + 1, buf_b)
            update(vs_ref, j + t, buf_a)
            sel_logits(j + t + 2, buf_a)
            update(vs_ref, j + t + 1, buf_b)

    j_tail = 0
    for unroll in SEL_UNROLLS:
        trips = (i - j_tail) // unroll

        def body(n, carry, start=j_tail, unroll=unroll):
            pipeline(start + unroll * n, unroll)
            return carry

        lax.fori_loop(0, trips, body, 0)
        j_tail = j_tail + unroll * trips

    @pl.when(j_tail == i)
    def _():
        update(vs_ref, i, buf_a)

    @pl.when(j_tail < i)
    def _():
        sel_logits(i, buf_b)
        update(vs_ref, j_tail, buf_a)
        update(vs_ref, i, buf_b)

    finalize(1)

    for h in range(HPG):
        o_ref[:, h * DH:(h + 1) * DH] = oacc_ref[h].astype(o_ref.dtype)


def _nsa(proj3, kvc, ksw, vsw, bias_cmp, bias_toep, bias_win, ovl_t):
    B, S, _ = proj3.shape
    G, HPG, DH = NSA_GROUPS, NSA_HPG, NSA_DH
    GW = HPG * DH
    TQ = ATT_TILE
    NCP = kvc.shape[3]
    ns = S // SEL_BLOCK
    assert S % TQ == 0 and (HPG * TQ) % NSA_SUB == 0 and WINDOW % TQ == 0
    assert 2 * DH == LANES and ns <= DH and ovl_t.shape == (DH, NCP)
    assert C_QA % GW == 0 and C_GA % LANES == 0 and 3 * NSA_HEADS <= LANES

    def seq_spec(idx):
        return pl.BlockSpec((None, None, None, S, 2 * DH), lambda b, g, i: (idx, b, g, 0, 0))

    def cmp_spec(idx):
        return pl.BlockSpec((None, None, None, NCP, DH), lambda b, g, i: (idx, b, g, 0, 0))

    return pl.pallas_call(
        functools.partial(_nsa_body, n_sel_blocks=ns, n_top=min(SEL_TOPN, ns)),
        grid=(B, G, S // TQ),
        in_specs=[
            pl.BlockSpec((None, TQ, GW), lambda b, g, i: (b, i, C_QA // GW + g)),
            pl.BlockSpec((None, TQ, LANES), lambda b, g, i: (b, i, C_GA // LANES)),
            cmp_spec(0), cmp_spec(1),
            seq_spec(0), seq_spec(0), seq_spec(1), seq_spec(1),
            pl.BlockSpec((HPG, TQ, NCP), lambda b, g, i: (g, i, 0)),
            pl.BlockSpec((N_BIAS_TILES, HPG, TQ, TQ), lambda b, g, i: (0, g, 0, 0)),
            pl.BlockSpec((N_WIN_TILES + 1, HPG, TQ, TQ), lambda b, g, i: (0, g, 0, 0)),
            pl.BlockSpec((DH, NCP), lambda b, g, i: (0, 0)),
        ],
        out_specs=pl.BlockSpec((None, TQ, GW), lambda b, g, i: (b, i, g)),
        out_shape=jax.ShapeDtypeStruct((B, S, G * GW), BF16),
        scratch_shapes=[
            pltpu.VMEM((HPG, TQ, DH), BF16),
            pltpu.VMEM((HPG, TQ, 2 * DH), BF16),
            pltpu.VMEM((HPG * TQ, LANES), F32),
            pltpu.VMEM((HPG * TQ, 2 * DH), F32),
            pltpu.VMEM((HPG, TQ, DH), F32),
            pltpu.VMEM((HPG * TQ, TQ), F32),
            pltpu.VMEM((HPG * TQ, TQ), F32),
            pltpu.VMEM((HPG * TQ, TQ), F32),
            pltpu.VMEM((HPG * TQ, LANES), F32),
            pltpu.VMEM((TQ, LANES), F32),
            pltpu.VMEM((HPG * TQ, NCP), F32),
            pltpu.VMEM((HPG * TQ, NCP), BF16),
            pltpu.VMEM((TQ, NCP), F32),
        ],
        compiler_params=_params("parallel", "parallel", "arbitrary"),
        name="nsa",
    )(proj3, proj3, kvc, kvc, ksw, vsw, ksw, vsw, bias_cmp, bias_toep, bias_win, ovl_t)


def _gla_body(q_ref, k_ref, v_ref, r_ref, al_ref, w2_ref, ab_ref, gn_ref, o_ref, st_ref, u_ref, sb_ref):
    @pl.when(pl.program_id(1) == 0)
    def _():
        st_ref[...] = jnp.zeros_like(st_ref)

    C = GLA_CHUNK
    RB = q_ref.shape[0]
    n_chunks = RB // C
    pre = _dot(al_ref[...].astype(BF16), w2_ref[...]) + ab_ref[...]
    la = (jnp.minimum(pre, 0.0) - jnp.log(1.0 + jnp.exp(-jnp.abs(pre)))) * (1.0 / GLA_TAU)

    la_hi = la.astype(BF16)
    rest = la - la_hi.astype(F32)
    la_mid = rest.astype(BF16)
    la_lo = (rest - la_mid.astype(F32)).astype(BF16)
    r_io = lax.broadcasted_iota(jnp.int32, (C, 3 * C), 0)
    c_io = lax.broadcasted_iota(jnp.int32, (C, 3 * C), 1)
    tri3 = ((c_io & (C - 1)) <= r_io).astype(BF16)
    b_parts, bl_parts = [], []
    for c in range(n_chunks):
        rows = slice(c * C, (c + 1) * C)
        b_c = _dot(tri3, jnp.concatenate([la_hi[rows], la_mid[rows], la_lo[rows]], axis=0))
        b_parts.append(b_c)
        bl_parts.append(jnp.broadcast_to(b_c[C - 1:C, :], b_c.shape))
    b = jnp.concatenate(b_parts, axis=0)
    b_last = jnp.concatenate(bl_parts, axis=0)

    k = k_ref[...]
    q_dec = (q_ref[...] * (jnp.exp(b) * (GLA_DK ** -0.5))).astype(BF16)
    k_intra = (k * jnp.exp(-b)).astype(BF16)
    k_state = (k * jnp.exp(b_last - b)).astype(BF16)

    rr = lax.broadcasted_iota(jnp.int32, (RB, RB), 0)
    cc = lax.broadcasted_iota(jnp.int32, (RB, RB), 1)
    same_chunk_causal = (cc <= rr) & ((rr & -C) == (cc & -C))
    heads = [(h, slice(h * GLA_DK, (h + 1) * GLA_DK), slice(h * GLA_DV, (h + 1) * GLA_DV))
             for h in range(GLA_HEADS)]
    chunks = [(c, slice(c * C, (c + 1) * C)) for c in range(n_chunks)]
    for h, kc, vc in heads:
        v = v_ref[:, vc].astype(BF16)
        for c, rows in chunks:
            u_ref[h, c] = _dot_tn(k_state[rows, kc], v[rows])
    for h, kc, vc in heads:
        st = st_ref[h]
        for c, rows in chunks:
            sb_ref[h, c] = st.astype(BF16)
            decay = jnp.exp(b[(c + 1) * C - 8:(c + 1) * C, kc].T[:, 7:8])
            st = st * decay + u_ref[h, c]
        st_ref[h] = st
    for h, kc, vc in heads:
        v = v_ref[:, vc].astype(BF16)
        a = jnp.where(same_chunk_causal, _dot_nt(q_dec[:, kc], k_intra[:, kc]), 0.0)
        o_intra = _dot(a.astype(BF16), v)
        o_inter = jnp.concatenate([_dot(q_dec[rows, kc], sb_ref[h, c]) for c, rows in chunks], axis=0)
        o = _rms(o_intra + o_inter, gn_ref[:, vc])
        r = r_ref[:, vc]
        o_ref[:, vc] = (o * (r * jax.nn.sigmoid(r))).astype(o_ref.dtype)


def _gla(proj3, w2, ab, gn, layer):
    B, S, _ = proj3.shape
    RB = GLA_STEP
    HK = GLA_HEADS * GLA_DK
    HV = GLA_HEADS * GLA_DV
    assert S % RB == 0 and RB % GLA_CHUNK == 0

    def col(width, offset):
        assert offset % width == 0
        return pl.BlockSpec((None, RB, width), lambda b, s: (b, s, offset // width))

    return pl.pallas_call(
        _gla_body,
        grid=(B, S // RB),
        in_specs=[
            col(HK, C_QB), col(HK, C_KB), col(HV, C_VB), col(HV, C_RB), col(LANES, C_AL),
            pl.BlockSpec((None, LANES, HK), lambda b, s: (layer, 0, 0)),
            pl.BlockSpec((None, 1, HK), lambda b, s: (layer, 0, 0)),
            pl.BlockSpec((None, 1, HV), lambda b, s: (layer, 0, 0)),
        ],
        out_specs=pl.BlockSpec((None, RB, HV), lambda b, s: (b, s, 0)),
        out_shape=jax.ShapeDtypeStruct((B, S, HV), BF16),
        scratch_shapes=[pltpu.VMEM((GLA_HEADS, GLA_DK, GLA_DV), F32),
                        pltpu.VMEM((GLA_HEADS, RB // GLA_CHUNK, GLA_DK, GLA_DV), F32),
                        pltpu.VMEM((GLA_HEADS, RB // GLA_CHUNK, GLA_DK, GLA_DV), BF16)],
        compiler_params=_params("parallel", "arbitrary"),
        name="gla",
    )(proj3, proj3, proj3, proj3, proj3, w2, ab, gn)


def _merge_body(x_ref, oa_ref, ob_ref, gm_ref, wa_ref, wb_ref, wo_ref, o_ref):
    D = x_ref.shape[1]
    gm = gm_ref[...]
    y = (jax.nn.sigmoid(gm[:, :D]) * _dot(oa_ref[...], wa_ref[...])
         + jax.nn.sigmoid(gm[:, D:]) * _dot(ob_ref[...], wb_ref[...]))
    o_ref[...] = x_ref[...] + _dot(y.astype(BF16), wo_ref[...])


def _merge(x, o_a, o_b, proj, wa, wb, wo, layer, tm=512):
    T, D = x.shape
    DA = o_a.shape[1]
    DB = o_b.shape[1]
    assert T % tm == 0 and C_GM == 0
    return pl.pallas_call(
        _merge_body,
        grid=(T // tm,),
        in_specs=[
            pl.BlockSpec((tm, D), lambda i: (i, 0)),
            pl.BlockSpec((tm, DA), lambda i: (i, 0)),
            pl.BlockSpec((tm, DB), lambda i: (i, 0)),
            pl.BlockSpec((tm, 2 * D), lambda i: (i, 0)),
            pl.BlockSpec((None, DA, D), lambda i: (layer, 0, 0)),
            pl.BlockSpec((None, DB, D), lambda i: (layer, 0, 0)),
            pl.BlockSpec((None, D, D), lambda i: (layer, 0, 0)),
        ],
        out_specs=pl.BlockSpec((tm, D), lambda i: (i, 0)),
        out_shape=jax.ShapeDtypeStruct((T, D), F32),
        compiler_params=_params("parallel"),
        name="merge",
    )(x, o_a, o_b, proj, wa, wb, wo)


def _rel_bucket(dist):
    n = jnp.maximum(dist, 0)
    exact = REL_BUCKETS // 2
    nf = jnp.maximum(n, 1).astype(jnp.float32)
    log_b = exact + (jnp.log(nf / exact) / math.log(REL_MAX_DIST / exact)
                     * (REL_BUCKETS - exact)).astype(jnp.int32)
    return jnp.where(n < exact, n, jnp.minimum(log_b, REL_BUCKETS - 1))


def _regroup_w_in(w_in):
    widths = (NSA_HEADS * NSA_DH, 6 * NSA_GROUPS * NSA_DH, 3 * NSA_HEADS, GLA_HEADS * GLA_DK,
              GLA_HEADS * GLA_DK, GLA_HEADS * GLA_DV, GLA_RANK, GLA_HEADS * GLA_DV, 2 * D_MODEL)
    offs = np.concatenate([[0], np.cumsum(widths)])
    w = w_in.astype(BF16)
    q_a, kv_a, g_a, q_b, k_b, v_b, a_lr, r_b, g_m = (w[..., offs[n]:offs[n + 1]] for n in range(9))

    def pad(p):
        return jnp.pad(p, ((0, 0), (0, 0), (0, LANES - p.shape[-1])))

    out = jnp.concatenate([g_m, v_b, r_b, q_a, q_b, k_b, kv_a, pad(g_a), pad(a_lr)], axis=-1)
    assert out.shape[-1] == N_PROJ
    return out


def _overlap_t(ncp, nsp, nc, ns):
    c = np.arange(ncp)[None, :] * CMP_STRIDE
    s = np.arange(nsp)[:, None] * SEL_BLOCK
    ov = (c < s + SEL_BLOCK) & (c + CMP_BLOCK > s) & (np.arange(ncp)[None, :] < nc) & (np.arange(nsp)[:, None] < ns)
    return jnp.asarray(ov.astype(np.float32))


def kernel(x, rel_table, ffn1_norm, ffn1_w_gate, ffn1_w_up, ffn1_w_down, mix_norm, w_in, cmp_pos_k, cmp_pos_v, cmp_k_w1, cmp_k_w2, cmp_v_w1, cmp_v_w2, gla_a_w2, gla_a_b, gla_out_norm, w_branch_nsa, w_branch_gla, w_out, ffn2_norm, ffn2_w_gate, ffn2_w_up, ffn2_w_down, final_norm):
    B, S, D = x.shape
    L = w_in.shape[0]
    T = B * S
    G, HPG, DH = NSA_GROUPS, NSA_HPG, NSA_DH
    nch = S // CMP_STRIDE
    nc = (S - CMP_BLOCK) // CMP_STRIDE + 1
    ns = S // SEL_BLOCK
    assert D == D_MODEL and nc == nch - 1

    w1g, w1u, w1d = ffn1_w_gate.astype(BF16), ffn1_w_up.astype(BF16), ffn1_w_down.astype(BF16)
    w2g, w2u, w2d = ffn2_w_gate.astype(BF16), ffn2_w_up.astype(BF16), ffn2_w_down.astype(BF16)
    w_proj = _regroup_w_in(w_in)
    wa, wb, wo = w_branch_nsa.astype(BF16), w_branch_gla.astype(BF16), w_out.astype(BF16)
    cmp_pos = jnp.stack([cmp_pos_k, cmp_pos_v]).reshape(2, L, 1, CMP_BLOCK * DH)
    cmp_w1 = jnp.stack([cmp_k_w1, cmp_v_w1]).astype(BF16)
    cmp_w2 = jnp.stack([cmp_k_w2, cmp_v_w2]).astype(BF16)
    gla_w2 = jnp.pad(gla_a_w2, ((0, 0), (0, LANES - GLA_RANK), (0, 0))).astype(BF16)
    gla_b = gla_a_b.reshape(L, 1, -1)
    gla_gn = gla_out_norm.reshape(L, 1, -1)
    n1 = ffn1_norm.reshape(L, 1, D)
    n2 = ffn2_norm.reshape(L, 1, D)
    nm = mix_norm.reshape(L, 1, D)

    buckets = _rel_bucket(jnp.arange(REL_MAX_DIST + 1, dtype=jnp.int32))
    thr = jnp.searchsorted(buckets, jnp.arange(REL_BUCKETS, dtype=jnp.int32), side="left").astype(jnp.int32)
    bias_toep, bias_win, bias_cmp = _bias_tables(thr, rel_table, S, nch)
    ovl_t = _overlap_t(nch, DH, nc, ns)

    xf = x.reshape(T, D)
    for l in range(L):
        xf = _ffn(xf, n1, w1g, w1u, w1d, l)
        proj, xc, ksw, vsw = _proj(xf, nm, w_proj, l, S)
        proj3 = proj.reshape(B, S, N_PROJ)

        kvc = _compress(xc, cmp_pos, cmp_w1, cmp_w2, l)
        o_a = _nsa(proj3, kvc, ksw, vsw, bias_cmp, bias_toep, bias_win, ovl_t)
        o_a = o_a.reshape(T, NSA_HEADS * DH)

        o_b = _gla(proj3, gla_w2, gla_b, gla_gn, l).reshape(T, GLA_HEADS * GLA_DV)

        xf = _merge(xf, o_a, o_b, proj, wa, wb, wo, l)
        xf = _ffn(xf, n2, w2g, w2u, w2d, l,
                  final_g=final_norm.reshape(1, D) if l == L - 1 else None)
    return xf.reshape(B, S, D)
```

```python
import functools
import math

import numpy as np
import jax
import jax.numpy as jnp
from jax import lax
from jax.experimental import pallas as pl
from jax.experimental.pallas import tpu as pltpu

F32 = jnp.float32
BF16 = jnp.bfloat16

NSA_HEADS = 8
NSA_GROUPS = 2
NSA_HPG = NSA_HEADS // NSA_GROUPS
NSA_DH = 64
CMP_BLOCK = 32
CMP_STRIDE = 16
SEL_BLOCK = 64
SEL_SHIFT = 6
SEL_TOPN = 16
WINDOW = 512
GLA_HEADS = 4
GLA_DK = 128
GLA_DV = 256
GLA_RANK = 16
GLA_TAU = 16.0
GLA_CHUNK = 64
REL_BUCKETS = 32
REL_MAX_DIST = 1024
EPS = 1e-6
NEG = -1e30
LOG2E = math.log2(math.e)

LANES = 128
VMEM_LIMIT = 56 * 1024 * 1024

ATT_TILE = 256
N_BIAS_TILES = REL_MAX_DIST // ATT_TILE + 2
N_WIN_TILES = WINDOW // ATT_TILE + 1
NSA_SUB = 128
SEL_UNROLLS = (8, 4, 2)
GLA_STEP = 512

D_MODEL = 1024
C_GM = 0
C_VB = 2048
C_RB = 3072
C_QA = 4096
C_QB = 4608
C_KB = 5120
C_KV = 5632
C_GA = 6400
C_AL = 6528
N_PROJ = 6656


def _dot(a, b, precision=None):
    return lax.dot_general(a, b, (((1,), (0,)), ((), ())), precision=precision,
                           preferred_element_type=F32)


def _dot_nt(a, b, precision=None):
    return lax.dot_general(a, b, (((1,), (1,)), ((), ())), precision=precision,
                           preferred_element_type=F32)


def _dot_tn(a, b, precision=None):
    return lax.dot_general(a, b, (((0,), (0,)), ((), ())), precision=precision,
                           preferred_element_type=F32)


def _rms(x, g):
    return x * lax.rsqrt(jnp.mean(x * x, axis=-1, keepdims=True) + EPS) * g


def _params(*sem):
    return pltpu.CompilerParams(dimension_semantics=sem, vmem_limit_bytes=VMEM_LIMIT)


def _ffn_body(x_ref, g_ref, wg_ref, wu_ref, wd_ref, *rest, final, fc):
    if final:
        fg_ref, o_ref = rest
    else:
        (o_ref,) = rest
    x = x_ref[...]
    h = _rms(x, g_ref[...]).astype(BF16)
    acc = None
    for f0 in range(0, wg_ref.shape[1], fc):
        gate = _dot(h, wg_ref[:, f0:f0 + fc])
        up = _dot(h, wu_ref[:, f0:f0 + fc])
        act = (gate * jax.nn.sigmoid(gate) * up).astype(BF16)
        down = _dot(act, wd_ref[f0:f0 + fc, :])
        acc = down if acc is None else acc + down
    y = x + 0.5 * acc
    if final:
        y = _rms(y, fg_ref[...])
    o_ref[...] = y


def _ffn(x, g, wg, wu, wd, layer, final_g=None, tm=1024, fc=704):
    T, D = x.shape
    F = wg.shape[-1]
    assert T % tm == 0 and F % fc == 0
    final = final_g is not None
    resident = pl.Buffered(1)
    in_specs = [
        pl.BlockSpec((tm, D), lambda i: (i, 0)),
        pl.BlockSpec((None, 1, D), lambda i: (layer, 0, 0)),
        pl.BlockSpec((None, D, F), lambda i: (layer, 0, 0), pipeline_mode=resident),
        pl.BlockSpec((None, D, F), lambda i: (layer, 0, 0), pipeline_mode=resident),
        pl.BlockSpec((None, F, D), lambda i: (layer, 0, 0), pipeline_mode=resident),
    ]
    args = [x, g, wg, wu, wd]
    if final:
        in_specs.append(pl.BlockSpec((1, D), lambda i: (0, 0)))
        args.append(final_g)
    return pl.pallas_call(
        functools.partial(_ffn_body, final=final, fc=fc),
        grid=(T // tm,),
        in_specs=in_specs,
        out_specs=pl.BlockSpec((tm, D), lambda i: (i, 0)),
        out_shape=jax.ShapeDtypeStruct((T, D), F32),
        compiler_params=_params("parallel"),
        name="ffn",
    )(*args)


def _proj_body(x_ref, g_ref, w_ref, o_ref, xc_ref, ksw_ref, vsw_ref, stage_ref, *, nc, tiles_per_seq):
    h = _rms(x_ref[...], g_ref[...]).astype(BF16)
    for n0 in range(0, w_ref.shape[1], nc):
        o_ref[:, n0:n0 + nc] = _dot(h, w_ref[:, n0:n0 + nc])

    TM = o_ref.shape[0]
    DH = NSA_DH
    GW = NSA_GROUPS * DH

    def kv_cols(n, g):
        return slice(C_KV + n * GW + g * DH, C_KV + n * GW + (g + 1) * DH)

    t = (pl.program_id(0) % tiles_per_seq) * TM + lax.broadcasted_iota(jnp.int32, (TM, DH), 0)
    onehot = ((t >> SEL_SHIFT) == lax.broadcasted_iota(jnp.int32, (TM, DH), 1)).astype(BF16)
    ones = jnp.ones((TM, DH), BF16)
    zeros = jnp.zeros((TM, DH), BF16)
    for g in range(NSA_GROUPS):
        ksw_ref[0, g] = jnp.concatenate([o_ref[:, kv_cols(2, g)].astype(BF16), onehot], axis=1)
        vsw_ref[0, g] = jnp.concatenate([o_ref[:, kv_cols(3, g)].astype(BF16), ones], axis=1)
        ksw_ref[1, g] = jnp.concatenate([o_ref[:, kv_cols(4, g)].astype(BF16), zeros], axis=1)
        vsw_ref[1, g] = jnp.concatenate([o_ref[:, kv_cols(5, g)].astype(BF16), ones], axis=1)
    for s in range(2):
        stage_ref[s] = o_ref[:, C_KV + s * GW:C_KV + (s + 1) * GW]
        for l in range(CMP_STRIDE):
            x = stage_ref[s, pl.ds(l, TM // CMP_STRIDE, stride=CMP_STRIDE), :]
            for g in range(NSA_GROUPS):
                xc_ref[s, g, :, l * DH:(l + 1) * DH] = x[:, g * DH:(g + 1) * DH]


def _proj(x, g, w, layer, seq_len, tm=512, nc=1664):
    T, D = x.shape
    N = w.shape[-1]
    G, DH = NSA_GROUPS, NSA_DH
    B = T // seq_len
    nb = seq_len // tm
    assert T % seq_len == 0 and seq_len % tm == 0 and N % nc == 0 and tm % (8 * CMP_STRIDE) == 0

    def kv_out(width, rows):
        return pl.BlockSpec((2, None, G, rows, width), lambda i: (0, i // nb, 0, i % nb, 0))

    return pl.pallas_call(
        functools.partial(_proj_body, nc=nc, tiles_per_seq=nb),
        grid=(T // tm,),
        in_specs=[
            pl.BlockSpec((tm, D), lambda i: (i, 0)),
            pl.BlockSpec((None, 1, D), lambda i: (layer, 0, 0)),
            pl.BlockSpec((None, D, N), lambda i: (layer, 0, 0), pipeline_mode=pl.Buffered(1)),
        ],
        out_specs=[pl.BlockSpec((tm, N), lambda i: (i, 0)),
                   kv_out(CMP_STRIDE * DH, tm // CMP_STRIDE), kv_out(2 * DH, tm), kv_out(2 * DH, tm)],
        out_shape=[jax.ShapeDtypeStruct((T, N), F32),
                   jax.ShapeDtypeStruct((2, B, G, seq_len // CMP_STRIDE, CMP_STRIDE * DH), F32),
                   jax.ShapeDtypeStruct((2, B, G, seq_len, 2 * DH), BF16),
                   jax.ShapeDtypeStruct((2, B, G, seq_len, 2 * DH), BF16)],
        scratch_shapes=[pltpu.VMEM((2, tm, G * DH), F32)],
        compiler_params=_params("parallel"),
        name="proj",
    )(x, g, w)


def _compress_body(x_ref, pos_ref, w1_ref, w2_ref, o_ref):
    x = x_ref[...]
    half = x.shape[1]
    lo = (x + pos_ref[:, :half]).astype(BF16)
    hi = (x + pos_ref[:, half:]).astype(BF16)
    h_lo = _dot(lo, w1_ref[:half, :])
    h_hi = _dot(hi, w1_ref[half:, :])
    nch = x.shape[0]
    hid = h_lo + pltpu.roll(h_hi, nch - 1, 0)
    act = (hid * jax.nn.sigmoid(hid)).astype(BF16)
    o_ref[...] = _dot(act, w2_ref[...]).astype(o_ref.dtype)


def _compress(xc, pos, w1, w2, layer):
    _, B, G, NCH, CW = xc.shape
    HC = w1.shape[-1]
    dh = w2.shape[-1]
    return pl.pallas_call(
        _compress_body,
        grid=(2, B, G),
        in_specs=[
            pl.BlockSpec((None, None, None, NCH, CW), lambda s, b, g: (s, b, g, 0, 0)),
            pl.BlockSpec((None, None, 1, 2 * CW), lambda s, b, g: (s, layer, 0, 0)),
            pl.BlockSpec((None, None, 2 * CW, HC), lambda s, b, g: (s, layer, 0, 0)),
            pl.BlockSpec((None, None, HC, dh), lambda s, b, g: (s, layer, 0, 0)),
        ],
        out_specs=pl.BlockSpec((None, None, None, NCH, dh), lambda s, b, g: (s, b, g, 0, 0)),
        out_shape=jax.ShapeDtypeStruct((2, B, G, NCH, dh), BF16),
        compiler_params=_params("parallel", "parallel", "parallel"),
        name="compress",
    )(xc, pos, w1, w2)


def _bias_lookup(n, thr_ref, tab_ref):
    vals = [jnp.full(n.shape, tab_ref[0, h], F32) for h in range(NSA_HEADS)]
    for k in range(1, REL_BUCKETS):
        above = n >= thr_ref[k]
        vals = [jnp.where(above, tab_ref[k, h], v) for h, v in enumerate(vals)]
    return vals


def _toeplitz_body(thr_ref, tab_ref, o_ref, *, rows, window):
    dd = pl.program_id(0)
    T = o_ref.shape[-1]
    k = lax.broadcasted_iota(jnp.int32, (8, 2 * T), 1)
    dist = dd * T + (T - 1) - k
    keep = (dist >= 0) & (dist < WINDOW) if window else dist >= 0
    for h, val in enumerate(_bias_lookup(jnp.clip(dist, 0, REL_MAX_DIST), thr_ref, tab_ref)):
        row = jnp.where(keep, val * LOG2E, NEG)[0:1, :]
        for r0 in range(0, T, rows):
            rot = pltpu.roll(jnp.broadcast_to(row, (rows, 2 * T)), T + 1 + r0, 1, stride=1, stride_axis=0)
            o_ref[h, r0:r0 + rows, :] = rot[:, :T]


def _cmpbias_body(thr_ref, tab_ref, o_ref, stage_ref):
    i = pl.program_id(0)
    H, TQ, NC = o_ref.shape
    RES = CMP_STRIDE
    MR = TQ // RES
    W = NC + LANES
    res = lax.broadcasted_iota(jnp.int32, (RES, W), 0)
    k = lax.broadcasted_iota(jnp.int32, (RES, W), 1)
    n = CMP_STRIDE * (i * MR + MR - 1 - k) + res - (CMP_BLOCK - 1)
    for h, val in enumerate(_bias_lookup(jnp.clip(n, 0, REL_MAX_DIST), thr_ref, tab_ref)):
        for r in range(RES):
            rot = pltpu.roll(jnp.broadcast_to(val[r:r + 1, :], (MR, W)), W - (MR - 1), 1,
                             stride=1, stride_axis=0)
            for s in range(NC // LANES):
                stage_ref[s, pl.ds(r, MR, stride=RES), :] = rot[:, s * LANES:(s + 1) * LANES]
        for s in range(NC // LANES):
            o_ref[h, :, s * LANES:(s + 1) * LANES] = stage_ref[s]


def _bias_tables(thr, rel_table, S, ncp):
    T = ATT_TILE
    H = NSA_HEADS
    rows = 64
    assert ncp % LANES == 0 and T % CMP_STRIDE == 0 and T // CMP_STRIDE - 1 <= LANES
    smem = pl.BlockSpec(memory_space=pltpu.SMEM)

    def toeplitz(n_tiles, window, name):
        return pl.pallas_call(
            functools.partial(_toeplitz_body, rows=rows, window=window),
            grid=(n_tiles,),
            in_specs=[smem, smem],
            out_specs=pl.BlockSpec((None, H, T, T), lambda d: (d, 0, 0, 0)),
            out_shape=jax.ShapeDtypeStruct((n_tiles, H, T, T), F32),
            compiler_params=_params("parallel"),
            name=name,
        )(thr, rel_table)

    toep = toeplitz(N_BIAS_TILES, False, "bias_toeplitz")
    toep_win = toeplitz(N_WIN_TILES + 1, True, "bias_window")
    cmpb = pl.pallas_call(
        _cmpbias_body,
        grid=(S // T,),
        in_specs=[smem, smem],
        out_specs=pl.BlockSpec((H, T, ncp), lambda i: (0, i, 0)),
        out_shape=jax.ShapeDtypeStruct((H, S, ncp), F32),
        scratch_shapes=[pltpu.VMEM((ncp // LANES, T, LANES), F32)],
        compiler_params=_params("parallel"),
        name="bias_cmp",
    )(thr, rel_table)
    return toep, toep_win, cmpb


def _nsa_body(q_ref, gate_ref, kc_ref, vc_ref, ks_ref, vs_ref, kw_ref, vw_ref, bc_ref, bts_ref, btw_ref,
              ovl_ref, o_ref, qc_ref, qa_ref, m_ref, acc_ref, oacc_ref, sa_ref, sb_ref, sc_ref, alpha_ref,
              gs_ref, lc_ref, pcb_ref, ps_ref,
              *, n_sel_blocks, n_top):
    group = pl.program_id(1)
    i = pl.program_id(2)
    HPG, TQ, DH = qc_ref.shape
    TK = TQ
    SB = NSA_SUB
    NCP = kc_ref.shape[0]
    t0 = i * TQ
    R = HPG * TQ
    subs = [(h, a0) for a0 in range(0, TQ, SB) for h in range(HPG)]

    for h in range(HPG):
        q = q_ref[:, h * DH:(h + 1) * DH]
        qc_ref[h] = (q * (DH ** -0.5)).astype(BF16)
        qa_ref[h, :, :DH] = (q * (DH ** -0.5 * LOG2E)).astype(BF16)
        qa_ref[h, :, DH:] = jnp.zeros((TQ, DH), BF16)

    gates = jax.nn.sigmoid(gate_ref[...])
    gs_ref[...] = jnp.where(group == 0, gates, pltpu.roll(gates, LANES - 3 * HPG, 1))

    def gate_col(a0, col):
        return gs_ref[a0:a0 + SB, col:col + 1]

    lc_all = _dot_nt(qc_ref[...].reshape(R, DH), kc_ref[...])
    lc_ref[...] = (lc_all.reshape(HPG, TQ, NCP) + bc_ref[...]).reshape(R, NCP)

    def cmp_branch():
        c_end = lax.broadcasted_iota(jnp.int32, (SB, NCP), 1) * CMP_STRIDE + (CMP_BLOCK - 1)
        r_c = lax.broadcasted_iota(jnp.int32, (SB, NCP), 0)
        for a0 in range(0, TQ, SB):
            mc = c_end <= (t0 + a0 + r_c)
            p_heads = None
            for h in range(HPG):
                rows = slice(h * TQ + a0, h * TQ + a0 + SB)
                lc = jnp.where(mc, lc_ref[rows, :], NEG)
                pc = jnp.where(mc, jnp.exp(lc - jnp.max(lc, axis=-1, keepdims=True)), 0.0)
                den = jnp.sum(pc, axis=-1, keepdims=True)
                pc = pc * jnp.where(den > 0.0, 1.0 / den, 0.0)
                pcb_ref[rows, :] = pc.astype(BF16)
                p_heads = pc if p_heads is None else p_heads + pc
            ps_ref[a0:a0 + SB, :] = p_heads
        o_cmp = _dot(pcb_ref[...], vc_ref[...])
        for h in range(HPG):
            oacc_ref[h] = gs_ref[:, 3 * h:3 * h + 1] * o_cmp[h * TQ:(h + 1) * TQ]

    def reset():
        m_ref[...] = jnp.full(m_ref.shape, NEG, F32)
        acc_ref[...] = jnp.zeros(acc_ref.shape, F32)

    def logits(k_ref, j, bias_ref, bias_tile, s_ref):
        k = k_ref[pl.ds(pl.multiple_of(j * TK, TK), TK), :]
        qk = _dot_nt(qa_ref[...].reshape(R, 2 * DH), k)
        s_ref[...] = (qk.reshape(HPG, TQ, TK) + bias_ref[bias_tile]).reshape(R, TK)

    def update_steps(v_ref, j, s_ref):
        chunks = [slice(r0, r0 + SB) for r0 in range(0, R, SB)]

        def pass1(rows):
            m_old = m_ref[rows, :]
            m_new = jnp.maximum(m_old, jnp.max(s_ref[rows, :], axis=-1, keepdims=True))
            alpha_ref[rows, :] = jnp.exp2(m_old - m_new)
            m_ref[rows, :] = m_new

        def pass2(rows):
            v = v_ref[pl.ds(pl.multiple_of(j * TK, TK), TK), :]
            m_new = m_ref[rows, :]
            p = jnp.exp2(s_ref[rows, :] - jnp.concatenate([m_new] * (TK // LANES), axis=1))
            acc_ref[rows, :] = alpha_ref[rows, :] * acc_ref[rows, :] + _dot(p.astype(BF16), v)

        return ([functools.partial(pass1, rows) for rows in chunks]
                + [functools.partial(pass2, rows) for rows in chunks])

    def update(v_ref, j, s_ref):
        for piece in update_steps(v_ref, j, s_ref):
            piece()

    def finalize(gate_off):
        for h, a0 in subs:
            acc = acc_ref[h * TQ + a0:h * TQ + a0 + SB, :]
            o = (acc * (1.0 / pltpu.roll(acc, DH, 1)))[:, :DH]
            oacc_ref[h, a0:a0 + SB, :] += gate_col(a0, 3 * h + gate_off) * o

    reset()
    buf_a, buf_b, buf_c = sa_ref, sb_ref, sc_ref
    win_pieces = []
    for n, buf in enumerate((buf_a, buf_b, buf_c)):
        dd = N_WIN_TILES - 1 - n
        j = jnp.maximum(i - dd, 0)
        logits(kw_ref, j, btw_ref, jnp.where(i < dd, N_WIN_TILES, dd), buf)
        win_pieces += update_steps(vw_ref, j, buf)

    cmp_branch()

    imp_t = _dot_nt(ovl_ref[...], ps_ref[...], precision=lax.Precision.HIGHEST)
    s_io = lax.broadcasted_iota(jnp.int32, (DH, TQ), 0)
    jcur = (t0 + lax.broadcasted_iota(jnp.int32, (DH, TQ), 1)) >> SEL_SHIFT
    forced = (s_io == 0) | (s_io == jcur) | (s_io == jcur - 1)
    score = jnp.where(forced, 1e6, jnp.where(s_io <= jcur, imp_t, -1e6))
    sub8 = lax.broadcasted_iota(jnp.int32, (8, TQ), 0)
    cnt = [jnp.zeros((8, TQ), jnp.int32) for _ in range(DH // 8)]
    emitted = 0
    for sp in range(n_sel_blocks):
        row = score[sp:sp + 1, :]
        for g in range(DH // 8):
            blk = score[8 * g:8 * g + 8, :]
            if 8 * g > sp:
                beats = row >= blk
            elif 8 * g + 7 <= sp:
                beats = row > blk
            else:
                beats = (row > blk) | ((row == blk) & (sub8 > sp - 8 * g))
            cnt[g] = cnt[g] + jnp.where(beats, 1, 0)
        due = (sp + 1) * len(win_pieces) // n_sel_blocks
        for piece in win_pieces[emitted:due]:
            piece()
        emitted = due
    finalize(2)
    rank = jnp.concatenate(cnt, axis=0)
    drop_t = jnp.where((rank < n_top) & (s_io < n_sel_blocks), 0.0, NEG)
    drop = jnp.concatenate([jnp.zeros((DH, TQ), F32), drop_t], axis=0).T.astype(BF16)
    for h in range(HPG):
        qa_ref[h, :, DH:] = drop[:, DH:]

    def sel_logits(j, buf):
        logits(ks_ref, j, bts_ref, jnp.minimum(i - j, N_BIAS_TILES - 1), buf)

    reset()
    sel_logits(0, buf_a)

    def pipeline(j, n):
        for t in range(0, n, 2):
            sel_logits(j + t + 1, buf_b)
            update(vs_ref, j + t, buf_a)
            sel_logits(j + t + 2, buf_a)
            update(vs_ref, j + t + 1, buf_b)

    j_tail = 0
    for unroll in SEL_UNROLLS:
        trips = (i - j_tail) // unroll

        def body(n, carry, start=j_tail, unroll=unroll):
            pipeline(start + unroll * n, unroll)
            return carry

        lax.fori_loop(0, trips, body, 0)
        j_tail = j_tail + unroll * trips

    @pl.when(j_tail == i)
    def _():
        update(vs_ref, i, buf_a)

    @pl.when(j_tail < i)
    def _():
        sel_logits(i, buf_b)
        update(vs_ref, j_tail, buf_a)
        update(vs_ref, i, buf_b)

    finalize(1)

    for h in range(HPG):
        o_ref[:, h * DH:(h + 1) * DH] = oacc_ref[h].astype(o_ref.dtype)


def _nsa(proj3, kvc, ksw, vsw, bias_cmp, bias_toep, bias_win, ovl_t):
    B, S, _ = proj3.shape
    G, HPG, DH = NSA_GROUPS, NSA_HPG, NSA_DH
    GW = HPG * DH
    TQ = ATT_TILE
    NCP = kvc.shape[3]
    ns = S // SEL_BLOCK
    assert S % TQ == 0 and (HPG * TQ) % NSA_SUB == 0 and WINDOW % TQ == 0
    assert 2 * DH == LANES and ns <= DH and ovl_t.shape == (DH, NCP)
    assert C_QA % GW == 0 and C_GA % LANES == 0 and 3 * NSA_HEADS <= LANES

    def seq_spec(idx):
        return pl.BlockSpec((None, None, None, S, 2 * DH), lambda b, g, i: (idx, b, g, 0, 0))

    def cmp_spec(idx):
        return pl.BlockSpec((None, None, None, NCP, DH), lambda b, g, i: (idx, b, g, 0, 0))

    return pl.pallas_call(
        functools.partial(_nsa_body, n_sel_blocks=ns, n_top=min(SEL_TOPN, ns)),
        grid=(B, G, S // TQ),
        in_specs=[
            pl.BlockSpec((None, TQ, GW), lambda b, g, i: (b, i, C_QA // GW + g)),
            pl.BlockSpec((None, TQ, LANES), lambda b, g, i: (b, i, C_GA // LANES)),
            cmp_spec(0), cmp_spec(1),
            seq_spec(0), seq_spec(0), seq_spec(1), seq_spec(1),
            pl.BlockSpec((HPG, TQ, NCP), lambda b, g, i: (g, i, 0)),
            pl.BlockSpec((N_BIAS_TILES, HPG, TQ, TQ), lambda b, g, i: (0, g, 0, 0)),
            pl.BlockSpec((N_WIN_TILES + 1, HPG, TQ, TQ), lambda b, g, i: (0, g, 0, 0)),
            pl.BlockSpec((DH, NCP), lambda b, g, i: (0, 0)),
        ],
        out_specs=pl.BlockSpec((None, TQ, GW), lambda b, g, i: (b, i, g)),
        out_shape=jax.ShapeDtypeStruct((B, S, G * GW), BF16),
        scratch_shapes=[
            pltpu.VMEM((HPG, TQ, DH), BF16),
            pltpu.VMEM((HPG, TQ, 2 * DH), BF16),
            pltpu.VMEM((HPG * TQ, LANES), F32),
            pltpu.VMEM((HPG * TQ, 2 * DH), F32),
            pltpu.VMEM((HPG, TQ, DH), F32),
            pltpu.VMEM((HPG * TQ, TQ), F32),
            pltpu.VMEM((HPG * TQ, TQ), F32),
            pltpu.VMEM((HPG * TQ, TQ), F32),
            pltpu.VMEM((HPG * TQ, LANES), F32),
            pltpu.VMEM((TQ, LANES), F32),
            pltpu.VMEM((HPG * TQ, NCP), F32),
            pltpu.VMEM((HPG * TQ, NCP), BF16),
            pltpu.VMEM((TQ, NCP), F32),
        ],
        compiler_params=_params("parallel", "parallel", "arbitrary"),
        name="nsa",
    )(proj3, proj3, kvc, kvc, ksw, vsw, ksw, vsw, bias_cmp, bias_toep, bias_win, ovl_t)


def _gla_body(q_ref, k_ref, v_ref, r_ref, al_ref, w2_ref, ab_ref, gn_ref, o_ref, st_ref, u_ref, sb_ref):
    @pl.when(pl.program_id(1) == 0)
    def _():
        st_ref[...] = jnp.zeros_like(st_ref)

    C = GLA_CHUNK
    RB = q_ref.shape[0]
    n_chunks = RB // C
    pre = _dot(al_ref[...].astype(BF16), w2_ref[...]) + ab_ref[...]
    la = (jnp.minimum(pre, 0.0) - jnp.log(1.0 + jnp.exp(-jnp.abs(pre)))) * (1.0 / GLA_TAU)

    la_hi = la.astype(BF16)
    rest = la - la_hi.astype(F32)
    la_mid = rest.astype(BF16)
    la_lo = (rest - la_mid.astype(F32)).astype(BF16)
    r_io = lax.broadcasted_iota(jnp.int32, (C, 3 * C), 0)
    c_io = lax.broadcasted_iota(jnp.int32, (C, 3 * C), 1)
    tri3 = ((c_io & (C - 1)) <= r_io).astype(BF16)
    b_parts, bl_parts = [], []
    for c in range(n_chunks):
        rows = slice(c * C, (c + 1) * C)
        b_c = _dot(tri3, jnp.concatenate([la_hi[rows], la_mid[rows], la_lo[rows]], axis=0))
        b_parts.append(b_c)
        bl_parts.append(jnp.broadcast_to(b_c[C - 1:C, :], b_c.shape))
    b = jnp.concatenate(b_parts, axis=0)
    b_last = jnp.concatenate(bl_parts, axis=0)

    k = k_ref[...]
    q_dec = (q_ref[...] * (jnp.exp(b) * (GLA_DK ** -0.5))).astype(BF16)
    k_intra = (k * jnp.exp(-b)).astype(BF16)
    k_state = (k * jnp.exp(b_last - b)).astype(BF16)

    rr = lax.broadcasted_iota(jnp.int32, (RB, RB), 0)
    cc = lax.broadcasted_iota(jnp.int32, (RB, RB), 1)
    same_chunk_causal = (cc <= rr) & ((rr & -C) == (cc & -C))
    heads = [(h, slice(h * GLA_DK, (h + 1) * GLA_DK), slice(h * GLA_DV, (h + 1) * GLA_DV))
             for h in range(GLA_HEADS)]
    chunks = [(c, slice(c * C, (c + 1) * C)) for c in range(n_chunks)]
    for h, kc, vc in heads:
        v = v_ref[:, vc].astype(BF16)
        for c, rows in chunks:
            u_ref[h, c] = _dot_tn(k_state[rows, kc], v[rows])
    for h, kc, vc in heads:
        st = st_ref[h]
        for c, rows in chunks:
            sb_ref[h, c] = st.astype(BF16)
            decay = jnp.exp(b[(c + 1) * C - 8:(c + 1) * C, kc].T[:, 7:8])
            st = st * decay + u_ref[h, c]
        st_ref[h] = st
    for h, kc, vc in heads:
        v = v_ref[:, vc].astype(BF16)
        a = jnp.where(same_chunk_causal, _dot_nt(q_dec[:, kc], k_intra[:, kc]), 0.0)
        o_intra = _dot(a.astype(BF16), v)
        o_inter = jnp.concatenate([_dot(q_dec[rows, kc], sb_ref[h, c]) for c, rows in chunks], axis=0)
        o = _rms(o_intra + o_inter, gn_ref[:, vc])
        r = r_ref[:, vc]
        o_ref[:, vc] = (o * (r * jax.nn.sigmoid(r))).astype(o_ref.dtype)


def _gla(proj3, w2, ab, gn, layer):
    B, S, _ = proj3.shape
    RB = GLA_STEP
    HK = GLA_HEADS * GLA_DK
    HV = GLA_HEADS * GLA_DV
    assert S % RB == 0 and RB % GLA_CHUNK == 0

    def col(width, offset):
        assert offset % width == 0
        return pl.BlockSpec((None, RB, width), lambda b, s: (b, s, offset // width))

    return pl.pallas_call(
        _gla_body,
        grid=(B, S // RB),
        in_specs=[
            col(HK, C_QB), col(HK, C_KB), col(HV, C_VB), col(HV, C_RB), col(LANES, C_AL),
            pl.BlockSpec((None, LANES, HK), lambda b, s: (layer, 0, 0)),
            pl.BlockSpec((None, 1, HK), lambda b, s: (layer, 0, 0)),
            pl.BlockSpec((None, 1, HV), lambda b, s: (layer, 0, 0)),
        ],
        out_specs=pl.BlockSpec((None, RB, HV), lambda b, s: (b, s, 0)),
        out_shape=jax.ShapeDtypeStruct((B, S, HV), BF16),
        scratch_shapes=[pltpu.VMEM((GLA_HEADS, GLA_DK, GLA_DV), F32),
                        pltpu.VMEM((GLA_HEADS, RB // GLA_CHUNK, GLA_DK, GLA_DV), F32),
                        pltpu.VMEM((GLA_HEADS, RB // GLA_CHUNK, GLA_DK, GLA_DV), BF16)],
        compiler_params=_params("parallel", "arbitrary"),
        name="gla",
    )(proj3, proj3, proj3, proj3, proj3, w2, ab, gn)


def _merge_body(x_ref, oa_ref, ob_ref, gm_ref, wa_ref, wb_ref, wo_ref, o_ref):
    D = x_ref.shape[1]
    gm = gm_ref[...]
    y = (jax.nn.sigmoid(gm[:, :D]) * _dot(oa_ref[...], wa_ref[...])
         + jax.nn.sigmoid(gm[:, D:]) * _dot(ob_ref[...], wb_ref[...]))
    o_ref[...] = x_ref[...] + _dot(y.astype(BF16), wo_ref[...])


def _merge(x, o_a, o_b, proj, wa, wb, wo, layer, tm=512):
    T, D = x.shape
    DA = o_a.shape[1]
    DB = o_b.shape[1]
    assert T % tm == 0 and C_GM == 0
    return pl.pallas_call(
        _merge_body,
        grid=(T // tm,),
        in_specs=[
            pl.BlockSpec((tm, D), lambda i: (i, 0)),
            pl.BlockSpec((tm, DA), lambda i: (i, 0)),
            pl.BlockSpec((tm, DB), lambda i: (i, 0)),
            pl.BlockSpec((tm, 2 * D), lambda i: (i, 0)),
            pl.BlockSpec((None, DA, D), lambda i: (layer, 0, 0)),
            pl.BlockSpec((None, DB, D), lambda i: (layer, 0, 0)),
            pl.BlockSpec((None, D, D), lambda i: (layer, 0, 0)),
        ],
        out_specs=pl.BlockSpec((tm, D), lambda i: (i, 0)),
        out_shape=jax.ShapeDtypeStruct((T, D), F32),
        compiler_params=_params("parallel"),
        name="merge",
    )(x, o_a, o_b, proj, wa, wb, wo)


def _rel_bucket(dist):
    n = jnp.maximum(dist, 0)
    exact = REL_BUCKETS // 2
    nf = jnp.maximum(n, 1).astype(jnp.float32)
    log_b = exact + (jnp.log(nf / exact) / math.log(REL_MAX_DIST / exact)
                     * (REL_BUCKETS - exact)).astype(jnp.int32)
    return jnp.where(n < exact, n, jnp.minimum(log_b, REL_BUCKETS - 1))


def _regroup_w_in(w_in):
    widths = (NSA_HEADS * NSA_DH, 6 * NSA_GROUPS * NSA_DH, 3 * NSA_HEADS, GLA_HEADS * GLA_DK,
              GLA_HEADS * GLA_DK, GLA_HEADS * GLA_DV, GLA_RANK, GLA_HEADS * GLA_DV, 2 * D_MODEL)
    offs = np.concatenate([[0], np.cumsum(widths)])
    w = w_in.astype(BF16)
    q_a, kv_a, g_a, q_b, k_b, v_b, a_lr, r_b, g_m = (w[..., offs[n]:offs[n + 1]] for n in range(9))

    def pad(p):
        return jnp.pad(p, ((0, 0), (0, 0), (0, LANES - p.shape[-1])))

    out = jnp.concatenate([g_m, v_b, r_b, q_a, q_b, k_b, kv_a, pad(g_a), pad(a_lr)], axis=-1)
    assert out.shape[-1] == N_PROJ
    return out


def _overlap_t(ncp, nsp, nc, ns):
    c = np.arange(ncp)[None, :] * CMP_STRIDE
    s = np.arange(nsp)[:, None] * SEL_BLOCK
    ov = (c < s + SEL_BLOCK) & (c + CMP_BLOCK > s) & (np.arange(ncp)[None, :] < nc) & (np.arange(nsp)[:, None] < ns)
    return jnp.asarray(ov.astype(np.float32))


def kernel(x, rel_table, ffn1_norm, ffn1_w_gate, ffn1_w_up, ffn1_w_down, mix_norm, w_in, cmp_pos_k, cmp_pos_v, cmp_k_w1, cmp_k_w2, cmp_v_w1, cmp_v_w2, gla_a_w2, gla_a_b, gla_out_norm, w_branch_nsa, w_branch_gla, w_out, ffn2_norm, ffn2_w_gate, ffn2_w_up, ffn2_w_down, final_norm):
    B, S, D = x.shape
    L = w_in.shape[0]
    T = B * S
    G, HPG, DH = NSA_GROUPS, NSA_HPG, NSA_DH
    nch = S // CMP_STRIDE
    nc = (S - CMP_BLOCK) // CMP_STRIDE + 1
    ns = S // SEL_BLOCK
    assert D == D_MODEL and nc == nch - 1

    w1g, w1u, w1d = ffn1_w_gate.astype(BF16), ffn1_w_up.astype(BF16), ffn1_w_down.astype(BF16)
    w2g, w2u, w2d = ffn2_w_gate.astype(BF16), ffn2_w_up.astype(BF16), ffn2_w_down.astype(BF16)
    w_proj = _regroup_w_in(w_in)
    wa, wb, wo = w_branch_nsa.astype(BF16), w_branch_gla.astype(BF16), w_out.astype(BF16)
    cmp_pos = jnp.stack([cmp_pos_k, cmp_pos_v]).reshape(2, L, 1, CMP_BLOCK * DH)
    cmp_w1 = jnp.stack([cmp_k_w1, cmp_v_w1]).astype(BF16)
    cmp_w2 = jnp.stack([cmp_k_w2, cmp_v_w2]).astype(BF16)
    gla_w2 = jnp.pad(gla_a_w2, ((0, 0), (0, LANES - GLA_RANK), (0, 0))).astype(BF16)
    gla_b = gla_a_b.reshape(L, 1, -1)
    gla_gn = gla_out_norm.reshape(L, 1, -1)
    n1 = ffn1_norm.reshape(L, 1, D)
    n2 = ffn2_norm.reshape(L, 1, D)
    nm = mix_norm.reshape(L, 1, D)

    buckets = _rel_bucket(jnp.arange(REL_MAX_DIST + 1, dtype=jnp.int32))
    thr = jnp.searchsorted(buckets, jnp.arange(REL_BUCKETS, dtype=jnp.int32), side="left").astype(jnp.int32)
    bias_toep, bias_win, bias_cmp = _bias_tables(thr, rel_table, S, nch)
    ovl_t = _overlap_t(nch, DH, nc, ns)

    xf = x.reshape(T, D)
    for l in range(L):
        xf = _ffn(xf, n1, w1g, w1u, w1d, l)
        proj, xc, ksw, vsw = _proj(xf, nm, w_proj, l, S)
        proj3 = proj.reshape(B, S, N_PROJ)

        kvc = _compress(xc, cmp_pos, cmp_w1, cmp_w2, l)
        o_a = _nsa(proj3, kvc, ksw, vsw, bias_cmp, bias_toep, bias_win, ovl_t)
        o_a = o_a.reshape(T, NSA_HEADS * DH)

        o_b = _gla(proj3, gla_w2, gla_b, gla_gn, l).reshape(T, GLA_HEADS * GLA_DV)

        xf = _merge(xf, o_a, o_b, proj, wa, wb, wo, l)
        xf = _ffn(xf, n2, w2g, w2u, w2d, l,
                  final_g=final_norm.reshape(1, D) if l == L - 1 else None)
    return xf.reshape(B, S, D)
```

```python
import functools
import math

import numpy as np
import jax
import jax.numpy as jnp
from jax import lax
from jax.experimental import pallas as pl
from jax.experimental.pallas import tpu as pltpu

F32 = jnp.float32
BF16 = jnp.bfloat16

NSA_HEADS = 8
NSA_GROUPS = 2
NSA_HPG = NSA_HEADS // NSA_GROUPS
NSA_DH = 64
CMP_BLOCK = 32
CMP_STRIDE = 16
SEL_BLOCK = 64
SEL_SHIFT = SEL_BLOCK.bit_length() - 1
assert 1 << SEL_SHIFT == SEL_BLOCK
SEL_TOPN = 16
WINDOW = 512
GLA_HEADS = 4
GLA_DK = 128
GLA_DV = 256
GLA_RANK = 16
GLA_TAU = 16.0
GLA_CHUNK = 64
REL_BUCKETS = 32
REL_MAX_DIST = 1024
EPS = 1e-6
NEG = -1e30
LOG2E = math.log2(math.e)

LANES = 128
VMEM_LIMIT = 56 * 1024 * 1024

ATT_TILE = 256
N_BIAS_TILES = REL_MAX_DIST // ATT_TILE + 2
N_WIN_TILES = WINDOW // ATT_TILE + 1
NSA_SUB = 128
SEL_UNROLLS = (8, 4, 2)
GLA_STEP = 512

D_MODEL = 1024
C_GM = 0
C_VB = 2048
C_RB = 3072
C_QA = 4096
C_QB = 4608
C_KB = 5120
C_KV = 5632
C_GA = 6400
C_AL = 6528
N_PROJ = 6656


def _dot(a, b, precision=None):
    return lax.dot_general(a, b, (((1,), (0,)), ((), ())), precision=precision,
                           preferred_element_type=F32)


def _dot_nt(a, b, precision=None):
    return lax.dot_general(a, b, (((1,), (1,)), ((), ())), precision=precision,
                           preferred_element_type=F32)


def _dot_tn(a, b, precision=None):
    return lax.dot_general(a, b, (((0,), (0,)), ((), ())), precision=precision,
                           preferred_element_type=F32)


def _rms(x, g):
    return x * lax.rsqrt(jnp.mean(x * x, axis=-1, keepdims=True) + EPS) * g


def _params(*sem):
    return pltpu.CompilerParams(dimension_semantics=sem, vmem_limit_bytes=VMEM_LIMIT)


def _ffn_body(x_ref, g_ref, wg_ref, wu_ref, wd_ref, *rest, final, fc):
    if final:
        fg_ref, o_ref = rest
    else:
        (o_ref,) = rest
    x = x_ref[...]
    h = _rms(x, g_ref[...]).astype(BF16)
    acc = None
    for f0 in range(0, wg_ref.shape[1], fc):
        gate = _dot(h, wg_ref[:, f0:f0 + fc])
        up = _dot(h, wu_ref[:, f0:f0 + fc])
        act = (gate * jax.nn.sigmoid(gate) * up).astype(BF16)
        down = _dot(act, wd_ref[f0:f0 + fc, :])
        acc = down if acc is None else acc + down
    y = x + 0.5 * acc
    if final:
        y = _rms(y, fg_ref[...])
    o_ref[...] = y


def _ffn(x, g, wg, wu, wd, layer, final_g=None, tm=1024, fc=704):
    T, D = x.shape
    F = wg.shape[-1]
    assert T % tm == 0 and F % fc == 0
    final = final_g is not None
    resident = pl.Buffered(1)
    in_specs = [
        pl.BlockSpec((tm, D), lambda i: (i, 0)),
        pl.BlockSpec((None, 1, D), lambda i: (layer, 0, 0)),
        pl.BlockSpec((None, D, F), lambda i: (layer, 0, 0), pipeline_mode=resident),
        pl.BlockSpec((None, D, F), lambda i: (layer, 0, 0), pipeline_mode=resident),
        pl.BlockSpec((None, F, D), lambda i: (layer, 0, 0), pipeline_mode=resident),
    ]
    args = [x, g, wg, wu, wd]
    if final:
        in_specs.append(pl.BlockSpec((1, D), lambda i: (0, 0)))
        args.append(final_g)
    return pl.pallas_call(
        functools.partial(_ffn_body, final=final, fc=fc),
        grid=(T // tm,),
        in_specs=in_specs,
        out_specs=pl.BlockSpec((tm, D), lambda i: (i, 0)),
        out_shape=jax.ShapeDtypeStruct((T, D), F32),
        compiler_params=_params("parallel"),
        name="ffn",
    )(*args)


def _proj_body(x_ref, g_ref, w_ref, o_ref, xc_ref, ksw_ref, vsw_ref, stage_ref, *, nc, tiles_per_seq):
    h = _rms(x_ref[...], g_ref[...]).astype(BF16)
    for n0 in range(0, w_ref.shape[1], nc):
        o_ref[:, n0:n0 + nc] = _dot(h, w_ref[:, n0:n0 + nc])

    TM = o_ref.shape[0]
    DH = NSA_DH
    GW = NSA_GROUPS * DH

    def kv_cols(n, g):
        return slice(C_KV + n * GW + g * DH, C_KV + n * GW + (g + 1) * DH)

    t = (pl.program_id(0) % tiles_per_seq) * TM + lax.broadcasted_iota(jnp.int32, (TM, DH), 0)
    onehot = ((t >> SEL_SHIFT) == lax.broadcasted_iota(jnp.int32, (TM, DH), 1)).astype(BF16)
    ones = jnp.ones((TM, DH), BF16)
    zeros = jnp.zeros((TM, DH), BF16)
    for g in range(NSA_GROUPS):
        ksw_ref[0, g] = jnp.concatenate([o_ref[:, kv_cols(2, g)].astype(BF16), onehot], axis=1)
        vsw_ref[0, g] = jnp.concatenate([o_ref[:, kv_cols(3, g)].astype(BF16), ones], axis=1)
        ksw_ref[1, g] = jnp.concatenate([o_ref[:, kv_cols(4, g)].astype(BF16), zeros], axis=1)
        vsw_ref[1, g] = jnp.concatenate([o_ref[:, kv_cols(5, g)].astype(BF16), ones], axis=1)
    for s in range(2):
        stage_ref[s] = o_ref[:, C_KV + s * GW:C_KV + (s + 1) * GW]
        for l in range(CMP_STRIDE):
            x = stage_ref[s, pl.ds(l, TM // CMP_STRIDE, stride=CMP_STRIDE), :]
            for g in range(NSA_GROUPS):
                xc_ref[s, g, :, l * DH:(l + 1) * DH] = x[:, g * DH:(g + 1) * DH]


def _proj(x, g, w, layer, seq_len, tm=512, nc=1664):
    T, D = x.shape
    N = w.shape[-1]
    G, DH = NSA_GROUPS, NSA_DH
    B = T // seq_len
    nb = seq_len // tm
    assert T % seq_len == 0 and seq_len % tm == 0 and N % nc == 0 and tm % (8 * CMP_STRIDE) == 0

    def kv_out(width, rows):
        return pl.BlockSpec((2, None, G, rows, width), lambda i: (0, i // nb, 0, i % nb, 0))

    return pl.pallas_call(
        functools.partial(_proj_body, nc=nc, tiles_per_seq=nb),
        grid=(T // tm,),
        in_specs=[
            pl.BlockSpec((tm, D), lambda i: (i, 0)),
            pl.BlockSpec((None, 1, D), lambda i: (layer, 0, 0)),
            pl.BlockSpec((None, D, N), lambda i: (layer, 0, 0), pipeline_mode=pl.Buffered(1)),
        ],
        out_specs=[pl.BlockSpec((tm, N), lambda i: (i, 0)),
                   kv_out(CMP_STRIDE * DH, tm // CMP_STRIDE), kv_out(2 * DH, tm), kv_out(2 * DH, tm)],
        out_shape=[jax.ShapeDtypeStruct((T, N), F32),
                   jax.ShapeDtypeStruct((2, B, G, seq_len // CMP_STRIDE, CMP_STRIDE * DH), F32),
                   jax.ShapeDtypeStruct((2, B, G, seq_len, 2 * DH), BF16),
                   jax.ShapeDtypeStruct((2, B, G, seq_len, 2 * DH), BF16)],
        scratch_shapes=[pltpu.VMEM((2, tm, G * DH), F32)],
        compiler_params=_params("parallel"),
        name="proj",
    )(x, g, w)


def _compress_body(x_ref, pos_ref, w1_ref, w2_ref, o_ref):
    x = x_ref[...]
    half = x.shape[1]
    lo = (x + pos_ref[:, :half]).astype(BF16)
    hi = (x + pos_ref[:, half:]).astype(BF16)
    h_lo = _dot(lo, w1_ref[:half, :])
    h_hi = _dot(hi, w1_ref[half:, :])
    nch = x.shape[0]
    hid = h_lo + pltpu.roll(h_hi, nch - 1, 0)
    act = (hid * jax.nn.sigmoid(hid)).astype(BF16)
    o_ref[...] = _dot(act, w2_ref[...]).astype(o_ref.dtype)


def _compress(xc, pos, w1, w2, layer):
    _, B, G, NCH, CW = xc.shape
    HC = w1.shape[-1]
    dh = w2.shape[-1]
    return pl.pallas_call(
        _compress_body,
        grid=(2, B, G),
        in_specs=[
            pl.BlockSpec((None, None, None, NCH, CW), lambda s, b, g: (s, b, g, 0, 0)),
            pl.BlockSpec((None, None, 1, 2 * CW), lambda s, b, g: (s, layer, 0, 0)),
            pl.BlockSpec((None, None, 2 * CW, HC), lambda s, b, g: (s, layer, 0, 0)),
            pl.BlockSpec((None, None, HC, dh), lambda s, b, g: (s, layer, 0, 0)),
        ],
        out_specs=pl.BlockSpec((None, None, None, NCH, dh), lambda s, b, g: (s, b, g, 0, 0)),
        out_shape=jax.ShapeDtypeStruct((2, B, G, NCH, dh), BF16),
        compiler_params=_params("parallel", "parallel", "parallel"),
        name="compress",
    )(xc, pos, w1, w2)


def _bias_lookup(n, thr_ref, tab_ref):
    vals = [jnp.full(n.shape, tab_ref[0, h], F32) for h in range(NSA_HEADS)]
    for k in range(1, REL_BUCKETS):
        above = n >= thr_ref[k]
        vals = [jnp.where(above, tab_ref[k, h], v) for h, v in enumerate(vals)]
    return vals


def _toeplitz_body(thr_ref, tab_ref, o_ref, *, rows, window):
    dd = pl.program_id(0)
    T = o_ref.shape[-1]
    k = lax.broadcasted_iota(jnp.int32, (8, 2 * T), 1)
    dist = dd * T + (T - 1) - k
    keep = (dist >= 0) & (dist < WINDOW) if window else dist >= 0
    for h, val in enumerate(_bias_lookup(jnp.clip(dist, 0, REL_MAX_DIST), thr_ref, tab_ref)):
        row = jnp.where(keep, val * LOG2E, NEG)[0:1, :]
        for r0 in range(0, T, rows):
            rot = pltpu.roll(jnp.broadcast_to(row, (rows, 2 * T)), T + 1 + r0, 1, stride=1, stride_axis=0)
            o_ref[h, r0:r0 + rows, :] = rot[:, :T]


def _cmpbias_body(thr_ref, tab_ref, o_ref, stage_ref):
    i = pl.program_id(0)
    H, TQ, NC = o_ref.shape
    RES = CMP_STRIDE
    MR = TQ // RES
    W = NC + LANES
    res = lax.broadcasted_iota(jnp.int32, (RES, W), 0)
    k = lax.broadcasted_iota(jnp.int32, (RES, W), 1)
    n = CMP_STRIDE * (i * MR + MR - 1 - k) + res - (CMP_BLOCK - 1)
    for h, val in enumerate(_bias_lookup(jnp.clip(n, 0, REL_MAX_DIST), thr_ref, tab_ref)):
        for r in range(RES):
            rot = pltpu.roll(jnp.broadcast_to(val[r:r + 1, :], (MR, W)), W - (MR - 1), 1,
                             stride=1, stride_axis=0)
            for s in range(NC // LANES):
                stage_ref[s, pl.ds(r, MR, stride=RES), :] = rot[:, s * LANES:(s + 1) * LANES]
        for s in range(NC // LANES):
            o_ref[h, :, s * LANES:(s + 1) * LANES] = stage_ref[s]


def _bias_tables(thr, rel_table, S, ncp):
    T = ATT_TILE
    H = NSA_HEADS
    rows = 64
    assert ncp % LANES == 0 and T % CMP_STRIDE == 0 and T // CMP_STRIDE - 1 <= LANES
    smem = pl.BlockSpec(memory_space=pltpu.SMEM)

    def toeplitz(n_tiles, window, name):
        return pl.pallas_call(
            functools.partial(_toeplitz_body, rows=rows, window=window),
            grid=(n_tiles,),
            in_specs=[smem, smem],
            out_specs=pl.BlockSpec((None, H, T, T), lambda d: (d, 0, 0, 0)),
            out_shape=jax.ShapeDtypeStruct((n_tiles, H, T, T), F32),
            compiler_params=_params("parallel"),
            name=name,
        )(thr, rel_table)

    toep = toeplitz(N_BIAS_TILES, False, "bias_toeplitz")
    toep_win = toeplitz(N_WIN_TILES + 1, True, "bias_window")
    cmpb = pl.pallas_call(
        _cmpbias_body,
        grid=(S // T,),
        in_specs=[smem, smem],
        out_specs=pl.BlockSpec((H, T, ncp), lambda i: (0, i, 0)),
        out_shape=jax.ShapeDtypeStruct((H, S, ncp), F32),
        scratch_shapes=[pltpu.VMEM((ncp // LANES, T, LANES), F32)],
        compiler_params=_params("parallel"),
        name="bias_cmp",
    )(thr, rel_table)
    return toep, toep_win, cmpb


def _nsa_body(q_ref, gate_ref, kc_ref, vc_ref, ks_ref, vs_ref, kw_ref, vw_ref, bc_ref, bts_ref, btw_ref,
              ovl_ref, o_ref, qc_ref, qa_ref, m_ref, acc_ref, oacc_ref, sa_ref, sb_ref, sc_ref, alpha_ref,
              gs_ref, lc_ref, pcb_ref, ps_ref,
              *, n_sel_blocks, n_top):
    group = pl.program_id(1)
    i = pl.program_id(2)
    HPG, TQ, DH = qc_ref.shape
    TK = TQ
    SB = NSA_SUB
    NCP = kc_ref.shape[0]
    t0 = i * TQ
    R = HPG * TQ
    subs = [(h, a0) for a0 in range(0, TQ, SB) for h in range(HPG)]

    for h in range(HPG):
        q = q_ref[:, h * DH:(h + 1) * DH]
        qc_ref[h] = (q * (DH ** -0.5)).astype(BF16)
        qa_ref[h, :, :DH] = (q * (DH ** -0.5 * LOG2E)).astype(BF16)
        qa_ref[h, :, DH:] = jnp.zeros((TQ, DH), BF16)

    gates = jax.nn.sigmoid(gate_ref[...])
    gs_ref[...] = jnp.where(group == 0, gates, pltpu.roll(gates, LANES - 3 * HPG, 1))

    def gate_col(a0, col):
        return gs_ref[a0:a0 + SB, col:col + 1]

    def cmp_branch():
        lc_all = _dot_nt(qc_ref[...].reshape(R, DH), kc_ref[...])
        lc_ref[...] = (lc_all.reshape(HPG, TQ, NCP) + bc_ref[...]).reshape(R, NCP)
        c_end = lax.broadcasted_iota(jnp.int32, (SB, NCP), 1) * CMP_STRIDE + (CMP_BLOCK - 1)
        r_c = lax.broadcasted_iota(jnp.int32, (SB, NCP), 0)
        for a0 in range(0, TQ, SB):
            mc = c_end <= (t0 + a0 + r_c)
            p_heads = None
            for h in range(HPG):
                rows = slice(h * TQ + a0, h * TQ + a0 + SB)
                lc = jnp.where(mc, lc_ref[rows, :], NEG)
                pc = jnp.where(mc, jnp.exp(lc - jnp.max(lc, axis=-1, keepdims=True)), 0.0)
                den = jnp.sum(pc, axis=-1, keepdims=True)
                pc = pc * jnp.where(den > 0.0, 1.0 / den, 0.0)
                pcb_ref[rows, :] = pc.astype(BF16)
                p_heads = pc if p_heads is None else p_heads + pc
            ps_ref[a0:a0 + SB, :] = p_heads
        o_cmp = _dot(pcb_ref[...], vc_ref[...])
        for h in range(HPG):
            oacc_ref[h] = gs_ref[:, 3 * h:3 * h + 1] * o_cmp[h * TQ:(h + 1) * TQ]

    def reset():
        m_ref[...] = jnp.full(m_ref.shape, NEG, F32)
        acc_ref[...] = jnp.zeros(acc_ref.shape, F32)

    def logits(k_ref, j, bias_ref, bias_tile, s_ref):
        k = k_ref[pl.ds(pl.multiple_of(j * TK, TK), TK), :]
        qk = _dot_nt(qa_ref[...].reshape(R, 2 * DH), k)
        s_ref[...] = (qk.reshape(HPG, TQ, TK) + bias_ref[bias_tile]).reshape(R, TK)

    def update_steps(v_ref, j, s_ref):
        chunks = [slice(r0, r0 + SB) for r0 in range(0, R, SB)]

        def pass1(rows):
            m_old = m_ref[rows, :]
            m_new = jnp.maximum(m_old, jnp.max(s_ref[rows, :], axis=-1, keepdims=True))
            alpha_ref[rows, :] = jnp.exp2(m_old - m_new)
            m_ref[rows, :] = m_new

        def pass2(rows):
            v = v_ref[pl.ds(pl.multiple_of(j * TK, TK), TK), :]
            m_new = m_ref[rows, :]
            p = jnp.exp2(s_ref[rows, :] - jnp.concatenate([m_new] * (TK // LANES), axis=1))
            acc_ref[rows, :] = alpha_ref[rows, :] * acc_ref[rows, :] + _dot(p.astype(BF16), v)

        return ([functools.partial(pass1, rows) for rows in chunks]
                + [functools.partial(pass2, rows) for rows in chunks])

    def update(v_ref, j, s_ref):
        for piece in update_steps(v_ref, j, s_ref):
            piece()

    def finalize(gate_off):
        for h, a0 in subs:
            acc = acc_ref[h * TQ + a0:h * TQ + a0 + SB, :]
            o = (acc * (1.0 / pltpu.roll(acc, DH, 1)))[:, :DH]
            oacc_ref[h, a0:a0 + SB, :] += gate_col(a0, 3 * h + gate_off) * o

    reset()
    buf_a, buf_b, buf_c = sa_ref, sb_ref, sc_ref
    win_pieces = []
    for n, buf in enumerate((buf_a, buf_b, buf_c)):
        dd = N_WIN_TILES - 1 - n
        j = jnp.maximum(i - dd, 0)
        logits(kw_ref, j, btw_ref, jnp.where(i < dd, N_WIN_TILES, dd), buf)
        win_pieces += update_steps(vw_ref, j, buf)

    cmp_branch()

    imp_t = _dot_nt(ovl_ref[...], ps_ref[...], precision=lax.Precision.HIGHEST)
    s_io = lax.broadcasted_iota(jnp.int32, (DH, TQ), 0)
    jcur = (t0 + lax.broadcasted_iota(jnp.int32, (DH, TQ), 1)) >> SEL_SHIFT
    forced = (s_io == 0) | (s_io == jcur) | (s_io == jcur - 1)
    score = jnp.where(forced, 1e6, jnp.where(s_io <= jcur, imp_t, -1e6))
    sub8 = lax.broadcasted_iota(jnp.int32, (8, TQ), 0)
    cnt = [jnp.zeros((8, TQ), jnp.int32) for _ in range(DH // 8)]
    emitted = 0
    for sp in range(n_sel_blocks):
        row = score[sp:sp + 1, :]
        for g in range(DH // 8):
            blk = score[8 * g:8 * g + 8, :]
            if 8 * g > sp:
                beats = row >= blk
            elif 8 * g + 7 <= sp:
                beats = row > blk
            else:
                beats = (row > blk) | ((row >= blk) & (sub8 > sp - 8 * g))
            cnt[g] = cnt[g] + jnp.where(beats, 1, 0)
        due = (sp + 1) * len(win_pieces) // n_sel_blocks
        for piece in win_pieces[emitted:due]:
            piece()
        emitted = due
    finalize(2)
    rank = jnp.concatenate(cnt, axis=0)
    drop_t = jnp.where((rank < n_top) & (s_io < n_sel_blocks), 0.0, NEG)
    drop = jnp.concatenate([jnp.zeros((DH, TQ), F32), drop_t], axis=0).T.astype(BF16)
    for h in range(HPG):
        qa_ref[h, :, DH:] = drop[:, DH:]

    def sel_logits(j, buf):
        logits(ks_ref, j, bts_ref, jnp.minimum(i - j, N_BIAS_TILES - 1), buf)

    reset()
    sel_logits(0, buf_a)

    def pipeline(j, n):
        for t in range(0, n, 2):
            sel_logits(j + t + 1, buf_b)
            update(vs_ref, j + t, buf_a)
            sel_logits(j + t + 2, buf_a)
            update(vs_ref, j + t + 1, buf_b)

    j_tail = 0
    for unroll in SEL_UNROLLS:
        trips = (i - j_tail) // unroll

        def body(n, carry, start=j_tail, unroll=unroll):
            pipeline(start + unroll * n, unroll)
            return carry

        lax.fori_loop(0, trips, body, 0)
        j_tail = j_tail + unroll * trips

    @pl.when(j_tail == i)
    def _():
        update(vs_ref, i, buf_a)

    @pl.when(j_tail < i)
    def _():
        sel_logits(i, buf_b)
        update(vs_ref, j_tail, buf_a)
        update(vs_ref, i, buf_b)

    finalize(1)

    for h in range(HPG):
        o_ref[:, h * DH:(h + 1) * DH] = oacc_ref[h].astype(o_ref.dtype)


def _nsa(proj3, kvc, ksw, vsw, bias_cmp, bias_toep, bias_win, ovl_t):
    B, S, _ = proj3.shape
    G, HPG, DH = NSA_GROUPS, NSA_HPG, NSA_DH
    GW = HPG * DH
    TQ = ATT_TILE
    NCP = kvc.shape[3]
    ns = S // SEL_BLOCK
    assert S % TQ == 0 and (HPG * TQ) % NSA_SUB == 0 and WINDOW % TQ == 0
    assert 2 * DH == LANES and ns <= DH and ovl_t.shape == (DH, NCP)
    assert C_QA % GW == 0 and C_GA % LANES == 0 and 3 * NSA_HEADS <= LANES

    def seq_spec(idx):
        return pl.BlockSpec((None, None, None, S, 2 * DH), lambda b, g, i: (idx, b, g, 0, 0))

    def cmp_spec(idx):
        return pl.BlockSpec((None, None, None, NCP, DH), lambda b, g, i: (idx, b, g, 0, 0))

    return pl.pallas_call(
        functools.partial(_nsa_body, n_sel_blocks=ns, n_top=min(SEL_TOPN, ns)),
        grid=(B, G, S // TQ),
        in_specs=[
            pl.BlockSpec((None, TQ, GW), lambda b, g, i: (b, i, C_QA // GW + g)),
            pl.BlockSpec((None, TQ, LANES), lambda b, g, i: (b, i, C_GA // LANES)),
            cmp_spec(0), cmp_spec(1),
            seq_spec(0), seq_spec(0), seq_spec(1), seq_spec(1),
            pl.BlockSpec((HPG, TQ, NCP), lambda b, g, i: (g, i, 0)),
            pl.BlockSpec((N_BIAS_TILES, HPG, TQ, TQ), lambda b, g, i: (0, g, 0, 0)),
            pl.BlockSpec((N_WIN_TILES + 1, HPG, TQ, TQ), lambda b, g, i: (0, g, 0, 0)),
            pl.BlockSpec((DH, NCP), lambda b, g, i: (0, 0)),
        ],
        out_specs=pl.BlockSpec((None, TQ, GW), lambda b, g, i: (b, i, g)),
        out_shape=jax.ShapeDtypeStruct((B, S, G * GW), BF16),
        scratch_shapes=[
            pltpu.VMEM((HPG, TQ, DH), BF16),
            pltpu.VMEM((HPG, TQ, 2 * DH), BF16),
            pltpu.VMEM((HPG * TQ, LANES), F32),
            pltpu.VMEM((HPG * TQ, 2 * DH), F32),
            pltpu.VMEM((HPG, TQ, DH), F32),
            pltpu.VMEM((HPG * TQ, TQ), F32),
            pltpu.VMEM((HPG * TQ, TQ), F32),
            pltpu.VMEM((HPG * TQ, TQ), F32),
            pltpu.VMEM((HPG * TQ, LANES), F32),
            pltpu.VMEM((TQ, LANES), F32),
            pltpu.VMEM((HPG * TQ, NCP), F32),
            pltpu.VMEM((HPG * TQ, NCP), BF16),
            pltpu.VMEM((TQ, NCP), F32),
        ],
        compiler_params=_params("parallel", "parallel", "arbitrary"),
        name="nsa",
    )(proj3, proj3, kvc, kvc, ksw, vsw, ksw, vsw, bias_cmp, bias_toep, bias_win, ovl_t)


def _gla_body(q_ref, k_ref, v_ref, r_ref, al_ref, w2_ref, ab_ref, gn_ref, o_ref, st_ref, u_ref, sb_ref):
    @pl.when(pl.program_id(1) == 0)
    def _():
        st_ref[...] = jnp.zeros_like(st_ref)

    C = GLA_CHUNK
    RB = q_ref.shape[0]
    n_chunks = RB // C
    pre = _dot(al_ref[...].astype(BF16), w2_ref[...]) + ab_ref[...]
    la = (jnp.minimum(pre, 0.0) - jnp.log(1.0 + jnp.exp(-jnp.abs(pre)))) * (1.0 / GLA_TAU)

    la_hi = la.astype(BF16)
    rest = la - la_hi.astype(F32)
    la_mid = rest.astype(BF16)
    la_lo = (rest - la_mid.astype(F32)).astype(BF16)
    r_io = lax.broadcasted_iota(jnp.int32, (C, 3 * C), 0)
    c_io = lax.broadcasted_iota(jnp.int32, (C, 3 * C), 1)
    tri3 = ((c_io & (C - 1)) <= r_io).astype(BF16)
    b_parts, bl_parts = [], []
    for c in range(n_chunks):
        rows = slice(c * C, (c + 1) * C)
        b_c = _dot(tri3, jnp.concatenate([la_hi[rows], la_mid[rows], la_lo[rows]], axis=0))
        b_parts.append(b_c)
        bl_parts.append(jnp.broadcast_to(b_c[C - 1:C, :], b_c.shape))
    b = jnp.concatenate(b_parts, axis=0)
    b_last = jnp.concatenate(bl_parts, axis=0)

    k = k_ref[...]
    q_dec = (q_ref[...] * (jnp.exp(b) * (GLA_DK ** -0.5))).astype(BF16)
    k_intra = (k * jnp.exp(-b)).astype(BF16)
    k_state = (k * jnp.exp(b_last - b)).astype(BF16)

    rr = lax.broadcasted_iota(jnp.int32, (RB, RB), 0)
    cc = lax.broadcasted_iota(jnp.int32, (RB, RB), 1)
    same_chunk_causal = (cc <= rr) & ((rr & -C) == (cc & -C))
    heads = [(h, slice(h * GLA_DK, (h + 1) * GLA_DK), slice(h * GLA_DV, (h + 1) * GLA_DV))
             for h in range(GLA_HEADS)]
    chunks = [(c, slice(c * C, (c + 1) * C)) for c in range(n_chunks)]
    for h, kc, vc in heads:
        v = v_ref[:, vc].astype(BF16)
        for c, rows in chunks:
            u_ref[h, c] = _dot_tn(k_state[rows, kc], v[rows])
    for h, kc, vc in heads:
        st = st_ref[h]
        for c, rows in chunks:
            sb_ref[h, c] = st.astype(BF16)
            decay = jnp.exp(b[(c + 1) * C - 8:(c + 1) * C, kc].T[:, 7:8])
            st = st * decay + u_ref[h, c]
        st_ref[h] = st
    for h, kc, vc in heads:
        v = v_ref[:, vc].astype(BF16)
        a = jnp.where(same_chunk_causal, _dot_nt(q_dec[:, kc], k_intra[:, kc]), 0.0)
        o_intra = _dot(a.astype(BF16), v)
        o_inter = jnp.concatenate([_dot(q_dec[rows, kc], sb_ref[h, c]) for c, rows in chunks], axis=0)
        o = _rms(o_intra + o_inter, gn_ref[:, vc])
        r = r_ref[:, vc]
        o_ref[:, vc] = (o * (r * jax.nn.sigmoid(r))).astype(o_ref.dtype)


def _gla(proj3, w2, ab, gn, layer):
    B, S, _ = proj3.shape
    RB = GLA_STEP
    HK = GLA_HEADS * GLA_DK
    HV = GLA_HEADS * GLA_DV
    assert S % RB == 0 and RB % GLA_CHUNK == 0

    def col(width, offset):
        assert offset % width == 0
        return pl.BlockSpec((None, RB, width), lambda b, s: (b, s, offset // width))

    return pl.pallas_call(
        _gla_body,
        grid=(B, S // RB),
        in_specs=[
            col(HK, C_QB), col(HK, C_KB), col(HV, C_VB), col(HV, C_RB), col(LANES, C_AL),
            pl.BlockSpec((None, LANES, HK), lambda b, s: (layer, 0, 0)),
            pl.BlockSpec((None, 1, HK), lambda b, s: (layer, 0, 0)),
            pl.BlockSpec((None, 1, HV), lambda b, s: (layer, 0, 0)),
        ],
        out_specs=pl.BlockSpec((None, RB, HV), lambda b, s: (b, s, 0)),
        out_shape=jax.ShapeDtypeStruct((B, S, HV), BF16),
        scratch_shapes=[pltpu.VMEM((GLA_HEADS, GLA_DK, GLA_DV), F32),
                        pltpu.VMEM((GLA_HEADS, RB // GLA_CHUNK, GLA_DK, GLA_DV), F32),
                        pltpu.VMEM((GLA_HEADS, RB // GLA_CHUNK, GLA_DK, GLA_DV), BF16)],
        compiler_params=_params("parallel", "arbitrary"),
        name="gla",
    )(proj3, proj3, proj3, proj3, proj3, w2, ab, gn)


def _merge_body(x_ref, oa_ref, ob_ref, gm_ref, wa_ref, wb_ref, wo_ref, o_ref):
    D = x_ref.shape[1]
    gm = gm_ref[...]
    y = (jax.nn.sigmoid(gm[:, :D]) * _dot(oa_ref[...], wa_ref[...])
         + jax.nn.sigmoid(gm[:, D:]) * _dot(ob_ref[...], wb_ref[...]))
    o_ref[...] = x_ref[...] + _dot(y.astype(BF16), wo_ref[...])


def _merge(x, o_a, o_b, proj, wa, wb, wo, layer, tm=512):
    T, D = x.shape
    DA = o_a.shape[1]
    DB = o_b.shape[1]
    assert T % tm == 0 and C_GM == 0
    return pl.pallas_call(
        _merge_body,
        grid=(T // tm,),
        in_specs=[
            pl.BlockSpec((tm, D), lambda i: (i, 0)),
            pl.BlockSpec((tm, DA), lambda i: (i, 0)),
            pl.BlockSpec((tm, DB), lambda i: (i, 0)),
            pl.BlockSpec((tm, 2 * D), lambda i: (i, 0)),
            pl.BlockSpec((None, DA, D), lambda i: (layer, 0, 0)),
            pl.BlockSpec((None, DB, D), lambda i: (layer, 0, 0)),
            pl.BlockSpec((None, D, D), lambda i: (layer, 0, 0)),
        ],
        out_specs=pl.BlockSpec((tm, D), lambda i: (i, 0)),
        out_shape=jax.ShapeDtypeStruct((T, D), F32),
        compiler_params=_params("parallel"),
        name="merge",
    )(x, o_a, o_b, proj, wa, wb, wo)


def _rel_bucket(dist):
    n = jnp.maximum(dist, 0)
    exact = REL_BUCKETS // 2
    nf = jnp.maximum(n, 1).astype(jnp.float32)
    log_b = exact + (jnp.log(nf / exact) / math.log(REL_MAX_DIST / exact)
                     * (REL_BUCKETS - exact)).astype(jnp.int32)
    return jnp.where(n < exact, n, jnp.minimum(log_b, REL_BUCKETS - 1))


def _regroup_w_in(w_in):
    widths = (NSA_HEADS * NSA_DH, 6 * NSA_GROUPS * NSA_DH, 3 * NSA_HEADS, GLA_HEADS * GLA_DK,
              GLA_HEADS * GLA_DK, GLA_HEADS * GLA_DV, GLA_RANK, GLA_HEADS * GLA_DV, 2 * D_MODEL)
    offs = np.concatenate([[0], np.cumsum(widths)])
    w = w_in.astype(BF16)
    q_a, kv_a, g_a, q_b, k_b, v_b, a_lr, r_b, g_m = (w[..., offs[n]:offs[n + 1]] for n in range(9))

    def pad(p):
        return jnp.pad(p, ((0, 0), (0, 0), (0, LANES - p.shape[-1])))

    out = jnp.concatenate([g_m, v_b, r_b, q_a, q_b, k_b, kv_a, pad(g_a), pad(a_lr)], axis=-1)
    assert out.shape[-1] == N_PROJ
    return out


def _overlap_t(ncp, nsp, nc, ns):
    c = np.arange(ncp)[None, :] * CMP_STRIDE
    s = np.arange(nsp)[:, None] * SEL_BLOCK
    ov = (c < s + SEL_BLOCK) & (c + CMP_BLOCK > s) & (np.arange(ncp)[None, :] < nc) & (np.arange(nsp)[:, None] < ns)
    return jnp.asarray(ov.astype(np.float32))


def kernel(x, rel_table, ffn1_norm, ffn1_w_gate, ffn1_w_up, ffn1_w_down, mix_norm, w_in, cmp_pos_k, cmp_pos_v, cmp_k_w1, cmp_k_w2, cmp_v_w1, cmp_v_w2, gla_a_w2, gla_a_b, gla_out_norm, w_branch_nsa, w_branch_gla, w_out, ffn2_norm, ffn2_w_gate, ffn2_w_up, ffn2_w_down, final_norm):
    B, S, D = x.shape
    L = w_in.shape[0]
    T = B * S
    G, HPG, DH = NSA_GROUPS, NSA_HPG, NSA_DH
    nch = S // CMP_STRIDE
    nc = (S - CMP_BLOCK) // CMP_STRIDE + 1
    ns = S // SEL_BLOCK
    assert D == D_MODEL and nc == nch - 1

    w1g, w1u, w1d = ffn1_w_gate.astype(BF16), ffn1_w_up.astype(BF16), ffn1_w_down.astype(BF16)
    w2g, w2u, w2d = ffn2_w_gate.astype(BF16), ffn2_w_up.astype(BF16), ffn2_w_down.astype(BF16)
    w_proj = _regroup_w_in(w_in)
    wa, wb, wo = w_branch_nsa.astype(BF16), w_branch_gla.astype(BF16), w_out.astype(BF16)
    cmp_pos = jnp.stack([cmp_pos_k, cmp_pos_v]).reshape(2, L, 1, CMP_BLOCK * DH)
    cmp_w1 = jnp.stack([cmp_k_w1, cmp_v_w1]).astype(BF16)
    cmp_w2 = jnp.stack([cmp_k_w2, cmp_v_w2]).astype(BF16)
    gla_w2 = jnp.pad(gla_a_w2, ((0, 0), (0, LANES - GLA_RANK), (0, 0))).astype(BF16)
    gla_b = gla_a_b.reshape(L, 1, -1)
    gla_gn = gla_out_norm.reshape(L, 1, -1)
    n1 = ffn1_norm.reshape(L, 1, D)
    n2 = ffn2_norm.reshape(L, 1, D)
    nm = mix_norm.reshape(L, 1, D)

    buckets = _rel_bucket(jnp.arange(REL_MAX_DIST + 1, dtype=jnp.int32))
    thr = jnp.searchsorted(buckets, jnp.arange(REL_BUCKETS, dtype=jnp.int32), side="left").astype(jnp.int32)
    bias_toep, bias_win, bias_cmp = _bias_tables(thr, rel_table, S, nch)
    ovl_t = _overlap_t(nch, DH, nc, ns)

    xf = x.reshape(T, D)
    for l in range(L):
        xf = _ffn(xf, n1, w1g, w1u, w1d, l)
        proj, xc, ksw, vsw = _proj(xf, nm, w_proj, l, S)
        proj3 = proj.reshape(B, S, N_PROJ)

        kvc = _compress(xc, cmp_pos, cmp_w1, cmp_w2, l)
        o_a = _nsa(proj3, kvc, ksw, vsw, bias_cmp, bias_toep, bias_win, ovl_t)
        o_a = o_a.reshape(T, NSA_HEADS * DH)

        o_b = _gla(proj3, gla_w2, gla_b, gla_gn, l).reshape(T, GLA_HEADS * GLA_DV)

        xf = _merge(xf, o_a, o_b, proj, wa, wb, wo, l)
        xf = _ffn(xf, n2, w2g, w2u, w2d, l,
                  final_g=final_norm.reshape(1, D) if l == L - 1 else None)
    return xf.reshape(B, S, D)
```

```python
import functools
import math

import numpy as np
import jax
import jax.numpy as jnp
from jax import lax
from jax.experimental import pallas as pl
from jax.experimental.pallas import tpu as pltpu

F32 = jnp.float32
BF16 = jnp.bfloat16

NSA_HEADS = 8
NSA_GROUPS = 2
NSA_HPG = NSA_HEADS // NSA_GROUPS
NSA_DH = 64
CMP_BLOCK = 32
CMP_STRIDE = 16
SEL_BLOCK = 64
SEL_SHIFT = SEL_BLOCK.bit_length() - 1
assert 1 << SEL_SHIFT == SEL_BLOCK
SEL_TOPN = 16
WINDOW = 512
GLA_HEADS = 4
GLA_DK = 128
GLA_DV = 256
GLA_RANK = 16
GLA_TAU = 16.0
GLA_CHUNK = 64
REL_BUCKETS = 32
REL_MAX_DIST = 1024
EPS = 1e-6
NEG = -1e30
LOG2E = math.log2(math.e)

LANES = 128
VMEM_LIMIT = 56 * 1024 * 1024

ATT_TILE = 256
N_BIAS_TILES = REL_MAX_DIST // ATT_TILE + 2
N_WIN_TILES = WINDOW // ATT_TILE + 1
NSA_SUB = 128
SEL_UNROLLS = (8, 4, 2)
GLA_STEP = 512

D_MODEL = 1024
C_GM = 0
C_VB = 2048
C_RB = 3072
C_QA = 4096
C_QB = 4608
C_KB = 5120
C_KV = 5632
C_GA = 6400
C_AL = 6528
N_PROJ = 6656


def _dot(a, b, precision=None):
    return lax.dot_general(a, b, (((1,), (0,)), ((), ())), precision=precision,
                           preferred_element_type=F32)


def _dot_nt(a, b, precision=None):
    return lax.dot_general(a, b, (((1,), (1,)), ((), ())), precision=precision,
                           preferred_element_type=F32)


def _dot_tn(a, b, precision=None):
    return lax.dot_general(a, b, (((0,), (0,)), ((), ())), precision=precision,
                           preferred_element_type=F32)


def _rms(x, g):
    return x * lax.rsqrt(jnp.mean(x * x, axis=-1, keepdims=True) + EPS) * g


def _params(*sem):
    return pltpu.CompilerParams(dimension_semantics=sem, vmem_limit_bytes=VMEM_LIMIT)


def _ffn_body(x_ref, g_ref, wg_ref, wu_ref, wd_ref, *rest, final, fc):
    if final:
        fg_ref, o_ref = rest
    else:
        (o_ref,) = rest
    x = x_ref[...]
    h = _rms(x, g_ref[...]).astype(BF16)
    acc = None
    for f0 in range(0, wg_ref.shape[1], fc):
        gate = _dot(h, wg_ref[:, f0:f0 + fc])
        up = _dot(h, wu_ref[:, f0:f0 + fc])
        act = (gate * jax.nn.sigmoid(gate) * up).astype(BF16)
        down = _dot(act, wd_ref[f0:f0 + fc, :])
        acc = down if acc is None else acc + down
    y = x + 0.5 * acc
    if final:
        y = _rms(y, fg_ref[...])
    o_ref[...] = y


def _ffn(x, g, wg, wu, wd, layer, final_g=None, tm=1024, fc=704):
    T, D = x.shape
    F = wg.shape[-1]
    assert T % tm == 0 and F % fc == 0
    final = final_g is not None
    resident = pl.Buffered(1)
    in_specs = [
        pl.BlockSpec((tm, D), lambda i: (i, 0)),
        pl.BlockSpec((None, 1, D), lambda i: (layer, 0, 0)),
        pl.BlockSpec((None, D, F), lambda i: (layer, 0, 0), pipeline_mode=resident),
        pl.BlockSpec((None, D, F), lambda i: (layer, 0, 0), pipeline_mode=resident),
        pl.BlockSpec((None, F, D), lambda i: (layer, 0, 0), pipeline_mode=resident),
    ]
    args = [x, g, wg, wu, wd]
    if final:
        in_specs.append(pl.BlockSpec((1, D), lambda i: (0, 0)))
        args.append(final_g)
    return pl.pallas_call(
        functools.partial(_ffn_body, final=final, fc=fc),
        grid=(T // tm,),
        in_specs=in_specs,
        out_specs=pl.BlockSpec((tm, D), lambda i: (i, 0)),
        out_shape=jax.ShapeDtypeStruct((T, D), F32),
        compiler_params=_params("parallel"),
        name="ffn",
    )(*args)


def _proj_body(x_ref, g_ref, *rest, n_groups, tiles_per_seq):
    w_refs = rest[:n_groups]
    o_ref, xc_ref, ksw_ref, vsw_ref, stage_ref = rest[n_groups:]
    h = _rms(x_ref[...], g_ref[...]).astype(BF16)
    n0 = 0
    for w_ref in w_refs:
        o_ref[:, n0:n0 + w_ref.shape[1]] = _dot(h, w_ref[...])
        n0 += w_ref.shape[1]

    TM = o_ref.shape[0]
    DH = NSA_DH
    GW = NSA_GROUPS * DH

    def kv_cols(n, g):
        return slice(C_KV + n * GW + g * DH, C_KV + n * GW + (g + 1) * DH)

    t = (pl.program_id(0) % tiles_per_seq) * TM + lax.broadcasted_iota(jnp.int32, (TM, DH), 0)
    onehot = ((t >> SEL_SHIFT) == lax.broadcasted_iota(jnp.int32, (TM, DH), 1)).astype(BF16)
    ones = jnp.ones((TM, DH), BF16)
    zeros = jnp.zeros((TM, DH), BF16)
    for g in range(NSA_GROUPS):
        ksw_ref[0, g] = jnp.concatenate([o_ref[:, kv_cols(2, g)].astype(BF16), onehot], axis=1)
        vsw_ref[0, g] = jnp.concatenate([o_ref[:, kv_cols(3, g)].astype(BF16), ones], axis=1)
        ksw_ref[1, g] = jnp.concatenate([o_ref[:, kv_cols(4, g)].astype(BF16), zeros], axis=1)
        vsw_ref[1, g] = jnp.concatenate([o_ref[:, kv_cols(5, g)].astype(BF16), ones], axis=1)
    for s in range(2):
        stage_ref[s] = o_ref[:, C_KV + s * GW:C_KV + (s + 1) * GW]
        for l in range(CMP_STRIDE):
            x = stage_ref[s, pl.ds(l, TM // CMP_STRIDE, stride=CMP_STRIDE), :]
            for g in range(NSA_GROUPS):
                xc_ref[s, g, :, l * DH:(l + 1) * DH] = x[:, g * DH:(g + 1) * DH]


def _proj(x, g, ws, layer, seq_len, tm=512):
    T, D = x.shape
    N = sum(w.shape[-1] for w in ws)
    G, DH = NSA_GROUPS, NSA_DH
    B = T // seq_len
    nb = seq_len // tm
    assert T % seq_len == 0 and seq_len % tm == 0 and tm % (8 * CMP_STRIDE) == 0
    assert all(w.shape[-1] % LANES == 0 for w in ws)

    def kv_out(width, rows):
        return pl.BlockSpec((2, None, G, rows, width), lambda i: (0, i // nb, 0, i % nb, 0))

    return pl.pallas_call(
        functools.partial(_proj_body, n_groups=len(ws), tiles_per_seq=nb),
        grid=(T // tm,),
        in_specs=[
            pl.BlockSpec((tm, D), lambda i: (i, 0)),
            pl.BlockSpec((None, 1, D), lambda i: (layer, 0, 0)),
        ] + [pl.BlockSpec((None, D, w.shape[-1]), lambda i: (layer, 0, 0), pipeline_mode=pl.Buffered(1))
             for w in ws],
        out_specs=[pl.BlockSpec((tm, N), lambda i: (i, 0)),
                   kv_out(CMP_STRIDE * DH, tm // CMP_STRIDE), kv_out(2 * DH, tm), kv_out(2 * DH, tm)],
        out_shape=[jax.ShapeDtypeStruct((T, N), F32),
                   jax.ShapeDtypeStruct((2, B, G, seq_len // CMP_STRIDE, CMP_STRIDE * DH), F32),
                   jax.ShapeDtypeStruct((2, B, G, seq_len, 2 * DH), BF16),
                   jax.ShapeDtypeStruct((2, B, G, seq_len, 2 * DH), BF16)],
        scratch_shapes=[pltpu.VMEM((2, tm, G * DH), F32)],
        compiler_params=_params("parallel"),
        name="proj",
    )(x, g, *ws)


def _compress_body(x_ref, pos_ref, w1_ref, w2_ref, o_ref):
    x = x_ref[...]
    half = x.shape[1]
    lo = (x + pos_ref[:, :half]).astype(BF16)
    hi = (x + pos_ref[:, half:]).astype(BF16)
    h_lo = _dot(lo, w1_ref[:half, :])
    h_hi = _dot(hi, w1_ref[half:, :])
    nch = x.shape[0]
    hid = h_lo + pltpu.roll(h_hi, nch - 1, 0)
    act = (hid * jax.nn.sigmoid(hid)).astype(BF16)
    o_ref[...] = _dot(act, w2_ref[...]).astype(o_ref.dtype)


def _compress(xc, pos, w1, w2, layer):
    _, B, G, NCH, CW = xc.shape
    HC = w1.shape[-1]
    dh = w2.shape[-1]
    return pl.pallas_call(
        _compress_body,
        grid=(2, B, G),
        in_specs=[
            pl.BlockSpec((None, None, None, NCH, CW), lambda s, b, g: (s, b, g, 0, 0)),
            pl.BlockSpec((None, None, 1, 2 * CW), lambda s, b, g: (s, layer, 0, 0)),
            pl.BlockSpec((None, None, 2 * CW, HC), lambda s, b, g: (s, layer, 0, 0)),
            pl.BlockSpec((None, None, HC, dh), lambda s, b, g: (s, layer, 0, 0)),
        ],
        out_specs=pl.BlockSpec((None, None, None, NCH, dh), lambda s, b, g: (s, b, g, 0, 0)),
        out_shape=jax.ShapeDtypeStruct((2, B, G, NCH, dh), BF16),
        compiler_params=_params("parallel", "parallel", "parallel"),
        name="compress",
    )(xc, pos, w1, w2)


def _bias_lookup(n, thr_ref, tab_ref):
    vals = [jnp.full(n.shape, tab_ref[0, h], F32) for h in range(NSA_HEADS)]
    for k in range(1, REL_BUCKETS):
        above = n >= thr_ref[k]
        vals = [jnp.where(above, tab_ref[k, h], v) for h, v in enumerate(vals)]
    return vals


def _toeplitz_body(thr_ref, tab_ref, o_ref, *, rows, window):
    dd = pl.program_id(0)
    T = o_ref.shape[-1]
    k = lax.broadcasted_iota(jnp.int32, (8, 2 * T), 1)
    dist = dd * T + (T - 1) - k
    keep = (dist >= 0) & (dist < WINDOW) if window else dist >= 0
    for h, val in enumerate(_bias_lookup(jnp.clip(dist, 0, REL_MAX_DIST), thr_ref, tab_ref)):
        row = jnp.where(keep, val * LOG2E, NEG)[0:1, :]
        for r0 in range(0, T, rows):
            rot = pltpu.roll(jnp.broadcast_to(row, (rows, 2 * T)), T + 1 + r0, 1, stride=1, stride_axis=0)
            o_ref[h, r0:r0 + rows, :] = rot[:, :T]


def _cmpbias_body(thr_ref, tab_ref, o_ref, stage_ref):
    i = pl.program_id(0)
    H, TQ, NC = o_ref.shape
    RES = CMP_STRIDE
    MR = TQ // RES
    W = NC + LANES
    res = lax.broadcasted_iota(jnp.int32, (RES, W), 0)
    k = lax.broadcasted_iota(jnp.int32, (RES, W), 1)
    n = CMP_STRIDE * (i * MR + MR - 1 - k) + res - (CMP_BLOCK - 1)
    for h, val in enumerate(_bias_lookup(jnp.clip(n, 0, REL_MAX_DIST), thr_ref, tab_ref)):
        for r in range(RES):
            rot = pltpu.roll(jnp.broadcast_to(val[r:r + 1, :], (MR, W)), W - (MR - 1), 1,
                             stride=1, stride_axis=0)
            for s in range(NC // LANES):
                stage_ref[s, pl.ds(r, MR, stride=RES), :] = rot[:, s * LANES:(s + 1) * LANES]
        for s in range(NC // LANES):
            o_ref[h, :, s * LANES:(s + 1) * LANES] = stage_ref[s]


def _bias_tables(thr, rel_table, S, ncp):
    T = ATT_TILE
    H = NSA_HEADS
    rows = 64
    assert ncp % LANES == 0 and T % CMP_STRIDE == 0 and T // CMP_STRIDE - 1 <= LANES
    smem = pl.BlockSpec(memory_space=pltpu.SMEM)

    def toeplitz(n_tiles, window, name):
        return pl.pallas_call(
            functools.partial(_toeplitz_body, rows=rows, window=window),
            grid=(n_tiles,),
            in_specs=[smem, smem],
            out_specs=pl.BlockSpec((None, H, T, T), lambda d: (d, 0, 0, 0)),
            out_shape=jax.ShapeDtypeStruct((n_tiles, H, T, T), F32),
            compiler_params=_params("parallel"),
            name=name,
        )(thr, rel_table)

    toep = toeplitz(N_BIAS_TILES, False, "bias_toeplitz")
    toep_win = toeplitz(N_WIN_TILES + 1, True, "bias_window")
    cmpb = pl.pallas_call(
        _cmpbias_body,
        grid=(S // T,),
        in_specs=[smem, smem],
        out_specs=pl.BlockSpec((H, T, ncp), lambda i: (0, i, 0)),
        out_shape=jax.ShapeDtypeStruct((H, S, ncp), F32),
        scratch_shapes=[pltpu.VMEM((ncp // LANES, T, LANES), F32)],
        compiler_params=_params("parallel"),
        name="bias_cmp",
    )(thr, rel_table)
    return toep, toep_win, cmpb


def _nsa_body(q_ref, gate_ref, kc_ref, vc_ref, ks_ref, vs_ref, kw_ref, vw_ref, bc_ref, bts_ref, btw_ref,
              ovl_ref, o_ref, qc_ref, qa_ref, m_ref, acc_ref, oacc_ref, sa_ref, sb_ref, sc_ref, alpha_ref,
              gs_ref, lc_ref, pcb_ref, ps_ref,
              *, n_sel_blocks, n_top):
    group = pl.program_id(1)
    i = pl.program_id(2)
    HPG, TQ, DH = qc_ref.shape
    TK = TQ
    SB = NSA_SUB
    NCP = kc_ref.shape[0]
    t0 = i * TQ
    R = HPG * TQ
    subs = [(h, a0) for a0 in range(0, TQ, SB) for h in range(HPG)]

    for h in range(HPG):
        q = q_ref[:, h * DH:(h + 1) * DH]
        qc_ref[h] = (q * (DH ** -0.5)).astype(BF16)
        qa_ref[h, :, :DH] = (q * (DH ** -0.5 * LOG2E)).astype(BF16)
        qa_ref[h, :, DH:] = jnp.zeros((TQ, DH), BF16)

    gates = jax.nn.sigmoid(gate_ref[...])
    gs_ref[...] = jnp.where(group == 0, gates, pltpu.roll(gates, LANES - 3 * HPG, 1))

    def gate_col(a0, col):
        return gs_ref[a0:a0 + SB, col:col + 1]

    def cmp_branch():
        lc_all = _dot_nt(qc_ref[...].reshape(R, DH), kc_ref[...])
        lc_ref[...] = (lc_all.reshape(HPG, TQ, NCP) + bc_ref[...]).reshape(R, NCP)
        c_end = lax.broadcasted_iota(jnp.int32, (SB, NCP), 1) * CMP_STRIDE + (CMP_BLOCK - 1)
        r_c = lax.broadcasted_iota(jnp.int32, (SB, NCP), 0)
        for a0 in range(0, TQ, SB):
            mc = c_end <= (t0 + a0 + r_c)
            p_heads = None
            for h in range(HPG):
                rows = slice(h * TQ + a0, h * TQ + a0 + SB)
                lc = jnp.where(mc, lc_ref[rows, :], NEG)
                pc = jnp.where(mc, jnp.exp(lc - jnp.max(lc, axis=-1, keepdims=True)), 0.0)
                den = jnp.sum(pc, axis=-1, keepdims=True)
                pc = pc * jnp.where(den > 0.0, 1.0 / den, 0.0)
                pcb_ref[rows, :] = pc.astype(BF16)
                p_heads = pc if p_heads is None else p_heads + pc
            ps_ref[a0:a0 + SB, :] = p_heads
        o_cmp = _dot(pcb_ref[...], vc_ref[...])
        for h in range(HPG):
            oacc_ref[h] = gs_ref[:, 3 * h:3 * h + 1] * o_cmp[h * TQ:(h + 1) * TQ]

    def reset():
        m_ref[...] = jnp.full(m_ref.shape, NEG, F32)
        acc_ref[...] = jnp.zeros(acc_ref.shape, F32)

    def logits(k_ref, j, bias_ref, bias_tile, s_ref):
        k = k_ref[pl.ds(pl.multiple_of(j * TK, TK), TK), :]
        qk = _dot_nt(qa_ref[...].reshape(R, 2 * DH), k)
        s_ref[...] = (qk.reshape(HPG, TQ, TK) + bias_ref[bias_tile]).reshape(R, TK)

    def update_steps(v_ref, j, s_ref):
        chunks = [slice(r0, r0 + SB) for r0 in range(0, R, SB)]

        def pass1(rows):
            m_old = m_ref[rows, :]
            m_new = jnp.maximum(m_old, jnp.max(s_ref[rows, :], axis=-1, keepdims=True))
            alpha_ref[rows, :] = jnp.exp2(m_old - m_new)
            m_ref[rows, :] = m_new

        def pass2(rows):
            v = v_ref[pl.ds(pl.multiple_of(j * TK, TK), TK), :]
            m_new = m_ref[rows, :]
            p = jnp.exp2(s_ref[rows, :] - jnp.concatenate([m_new] * (TK // LANES), axis=1))
            acc_ref[rows, :] = alpha_ref[rows, :] * acc_ref[rows, :] + _dot(p.astype(BF16), v)

        return ([functools.partial(pass1, rows) for rows in chunks]
                + [functools.partial(pass2, rows) for rows in chunks])

    def update(v_ref, j, s_ref):
        for piece in update_steps(v_ref, j, s_ref):
            piece()

    def finalize(gate_off):
        for h, a0 in subs:
            acc = acc_ref[h * TQ + a0:h * TQ + a0 + SB, :]
            o = (acc * (1.0 / pltpu.roll(acc, DH, 1)))[:, :DH]
            oacc_ref[h, a0:a0 + SB, :] += gate_col(a0, 3 * h + gate_off) * o

    reset()
    buf_a, buf_b, buf_c = sa_ref, sb_ref, sc_ref
    win_pieces = []
    for n, buf in enumerate((buf_a, buf_b, buf_c)):
        dd = N_WIN_TILES - 1 - n
        j = jnp.maximum(i - dd, 0)
        logits(kw_ref, j, btw_ref, jnp.where(i < dd, N_WIN_TILES, dd), buf)
        win_pieces += update_steps(vw_ref, j, buf)

    cmp_branch()

    imp_t = _dot_nt(ovl_ref[...], ps_ref[...], precision=lax.Precision.HIGHEST)
    s_io = lax.broadcasted_iota(jnp.int32, (DH, TQ), 0)
    jcur = (t0 + lax.broadcasted_iota(jnp.int32, (DH, TQ), 1)) >> SEL_SHIFT
    forced = (s_io == 0) | (s_io == jcur) | (s_io == jcur - 1)
    score = jnp.where(forced, 1e6, jnp.where(s_io <= jcur, imp_t, -1e6))
    sub8 = lax.broadcasted_iota(jnp.int32, (8, TQ), 0)
    cnt = [jnp.zeros((8, TQ), jnp.int32) for _ in range(DH // 8)]
    emitted = 0
    for sp in range(n_sel_blocks):
        row = score[sp:sp + 1, :]
        for g in range(DH // 8):
            blk = score[8 * g:8 * g + 8, :]
            if 8 * g > sp:
                beats = row >= blk
            elif 8 * g + 7 <= sp:
                beats = row > blk
            else:
                beats = (row > blk) | ((row >= blk) & (sub8 > sp - 8 * g))
            cnt[g] = cnt[g] + jnp.where(beats, 1, 0)
        due = (sp + 1) * len(win_pieces) // n_sel_blocks
        for piece in win_pieces[emitted:due]:
            piece()
        emitted = due
    finalize(2)
    rank = jnp.concatenate(cnt, axis=0)
    drop_t = jnp.where((rank < n_top) & (s_io < n_sel_blocks), 0.0, NEG)
    drop = jnp.concatenate([jnp.zeros((DH, TQ), F32), drop_t], axis=0).T.astype(BF16)
    for h in range(HPG):
        qa_ref[h, :, DH:] = drop[:, DH:]

    def sel_logits(j, buf):
        logits(ks_ref, j, bts_ref, jnp.minimum(i - j, N_BIAS_TILES - 1), buf)

    reset()
    sel_logits(0, buf_a)

    def pipeline(j, n):
        for t in range(0, n, 2):
            sel_logits(j + t + 1, buf_b)
            update(vs_ref, j + t, buf_a)
            sel_logits(j + t + 2, buf_a)
            update(vs_ref, j + t + 1, buf_b)

    j_tail = 0
    for unroll in SEL_UNROLLS:
        trips = (i - j_tail) // unroll

        def body(n, carry, start=j_tail, unroll=unroll):
            pipeline(start + unroll * n, unroll)
            return carry

        lax.fori_loop(0, trips, body, 0)
        j_tail = j_tail + unroll * trips

    @pl.when(j_tail == i)
    def _():
        update(vs_ref, i, buf_a)

    @pl.when(j_tail < i)
    def _():
        sel_logits(i, buf_b)
        update(vs_ref, j_tail, buf_a)
        update(vs_ref, i, buf_b)

    finalize(1)

    for h in range(HPG):
        o_ref[:, h * DH:(h + 1) * DH] = oacc_ref[h].astype(o_ref.dtype)


def _nsa(proj3, kvc, ksw, vsw, bias_cmp, bias_toep, bias_win, ovl_t):
    B, S, _ = proj3.shape
    G, HPG, DH = NSA_GROUPS, NSA_HPG, NSA_DH
    GW = HPG * DH
    TQ = ATT_TILE
    NCP = kvc.shape[3]
    ns = S // SEL_BLOCK
    assert S % TQ == 0 and (HPG * TQ) % NSA_SUB == 0 and WINDOW % TQ == 0
    assert 2 * DH == LANES and ns <= DH and ovl_t.shape == (DH, NCP)
    assert C_QA % GW == 0 and C_GA % LANES == 0 and 3 * NSA_HEADS <= LANES

    def seq_spec(idx):
        return pl.BlockSpec((None, None, None, S, 2 * DH), lambda b, g, i: (idx, b, g, 0, 0))

    def cmp_spec(idx):
        return pl.BlockSpec((None, None, None, NCP, DH), lambda b, g, i: (idx, b, g, 0, 0))

    return pl.pallas_call(
        functools.partial(_nsa_body, n_sel_blocks=ns, n_top=min(SEL_TOPN, ns)),
        grid=(B, G, S // TQ),
        in_specs=[
            pl.BlockSpec((None, TQ, GW), lambda b, g, i: (b, i, C_QA // GW + g)),
            pl.BlockSpec((None, TQ, LANES), lambda b, g, i: (b, i, C_GA // LANES)),
            cmp_spec(0), cmp_spec(1),
            seq_spec(0), seq_spec(0), seq_spec(1), seq_spec(1),
            pl.BlockSpec((HPG, TQ, NCP), lambda b, g, i: (g, i, 0)),
            pl.BlockSpec((N_BIAS_TILES, HPG, TQ, TQ), lambda b, g, i: (0, g, 0, 0)),
            pl.BlockSpec((N_WIN_TILES + 1, HPG, TQ, TQ), lambda b, g, i: (0, g, 0, 0)),
            pl.BlockSpec((DH, NCP), lambda b, g, i: (0, 0)),
        ],
        out_specs=pl.BlockSpec((None, TQ, GW), lambda b, g, i: (b, i, g)),
        out_shape=jax.ShapeDtypeStruct((B, S, G * GW), BF16),
        scratch_shapes=[
            pltpu.VMEM((HPG, TQ, DH), BF16),
            pltpu.VMEM((HPG, TQ, 2 * DH), BF16),
            pltpu.VMEM((HPG * TQ, LANES), F32),
            pltpu.VMEM((HPG * TQ, 2 * DH), F32),
            pltpu.VMEM((HPG, TQ, DH), F32),
            pltpu.VMEM((HPG * TQ, TQ), F32),
            pltpu.VMEM((HPG * TQ, TQ), F32),
            pltpu.VMEM((HPG * TQ, TQ), F32),
            pltpu.VMEM((HPG * TQ, LANES), F32),
            pltpu.VMEM((TQ, LANES), F32),
            pltpu.VMEM((HPG * TQ, NCP), F32),
            pltpu.VMEM((HPG * TQ, NCP), BF16),
            pltpu.VMEM((TQ, NCP), F32),
        ],
        compiler_params=_params("parallel", "parallel", "arbitrary"),
        name="nsa",
    )(proj3, proj3, kvc, kvc, ksw, vsw, ksw, vsw, bias_cmp, bias_toep, bias_win, ovl_t)


def _gla_body(q_ref, k_ref, v_ref, r_ref, al_ref, w2_ref, ab_ref, gn_ref, o_ref, st_ref, u_ref, sb_ref):
    @pl.when(pl.program_id(1) == 0)
    def _():
        st_ref[...] = jnp.zeros_like(st_ref)

    C = GLA_CHUNK
    RB = q_ref.shape[0]
    n_chunks = RB // C
    pre = _dot(al_ref[...].astype(BF16), w2_ref[...]) + ab_ref[...]
    la = (jnp.minimum(pre, 0.0) - jnp.log(1.0 + jnp.exp(-jnp.abs(pre)))) * (1.0 / GLA_TAU)

    la_hi = la.astype(BF16)
    rest = la - la_hi.astype(F32)
    la_mid = rest.astype(BF16)
    la_lo = (rest - la_mid.astype(F32)).astype(BF16)
    r_io = lax.broadcasted_iota(jnp.int32, (C, 3 * C), 0)
    c_io = lax.broadcasted_iota(jnp.int32, (C, 3 * C), 1)
    tri3 = ((c_io & (C - 1)) <= r_io).astype(BF16)
    b_parts, bl_parts = [], []
    for c in range(n_chunks):
        rows = slice(c * C, (c + 1) * C)
        b_c = _dot(tri3, jnp.concatenate([la_hi[rows], la_mid[rows], la_lo[rows]], axis=0))
        b_parts.append(b_c)
        bl_parts.append(jnp.broadcast_to(b_c[C - 1:C, :], b_c.shape))
    b = jnp.concatenate(b_parts, axis=0)
    b_last = jnp.concatenate(bl_parts, axis=0)

    k = k_ref[...]
    q_dec = (q_ref[...] * (jnp.exp(b) * (GLA_DK ** -0.5))).astype(BF16)
    k_intra = (k * jnp.exp(-b)).astype(BF16)
    k_state = (k * jnp.exp(b_last - b)).astype(BF16)

    rr = lax.broadcasted_iota(jnp.int32, (RB, RB), 0)
    cc = lax.broadcasted_iota(jnp.int32, (RB, RB), 1)
    same_chunk_causal = (cc <= rr) & ((rr & -C) == (cc & -C))
    heads = [(h, slice(h * GLA_DK, (h + 1) * GLA_DK), slice(h * GLA_DV, (h + 1) * GLA_DV))
             for h in range(GLA_HEADS)]
    chunks = [(c, slice(c * C, (c + 1) * C)) for c in range(n_chunks)]
    for h, kc, vc in heads:
        v = v_ref[:, vc].astype(BF16)
        for c, rows in chunks:
            u_ref[h, c] = _dot_tn(k_state[rows, kc], v[rows])
    for h, kc, vc in heads:
        st = st_ref[h]
        for c, rows in chunks:
            sb_ref[h, c] = st.astype(BF16)
            decay = jnp.exp(b[(c + 1) * C - 8:(c + 1) * C, kc].T[:, 7:8])
            st = st * decay + u_ref[h, c]
        st_ref[h] = st
    for h, kc, vc in heads:
        v = v_ref[:, vc].astype(BF16)
        a = jnp.where(same_chunk_causal, _dot_nt(q_dec[:, kc], k_intra[:, kc]), 0.0)
        o_intra = _dot(a.astype(BF16), v)
        o_inter = jnp.concatenate([_dot(q_dec[rows, kc], sb_ref[h, c]) for c, rows in chunks], axis=0)
        o = _rms(o_intra + o_inter, gn_ref[:, vc])
        r = r_ref[:, vc]
        o_ref[:, vc] = (o * (r * jax.nn.sigmoid(r))).astype(o_ref.dtype)


def _gla(proj3, w2, ab, gn, layer):
    B, S, _ = proj3.shape
    RB = GLA_STEP
    HK = GLA_HEADS * GLA_DK
    HV = GLA_HEADS * GLA_DV
    assert S % RB == 0 and RB % GLA_CHUNK == 0

    def col(width, offset):
        assert offset % width == 0
        return pl.BlockSpec((None, RB, width), lambda b, s: (b, s, offset // width))

    return pl.pallas_call(
        _gla_body,
        grid=(B, S // RB),
        in_specs=[
            col(HK, C_QB), col(HK, C_KB), col(HV, C_VB), col(HV, C_RB), col(LANES, C_AL),
            pl.BlockSpec((None, LANES, HK), lambda b, s: (layer, 0, 0)),
            pl.BlockSpec((None, 1, HK), lambda b, s: (layer, 0, 0)),
            pl.BlockSpec((None, 1, HV), lambda b, s: (layer, 0, 0)),
        ],
        out_specs=pl.BlockSpec((None, RB, HV), lambda b, s: (b, s, 0)),
        out_shape=jax.ShapeDtypeStruct((B, S, HV), BF16),
        scratch_shapes=[pltpu.VMEM((GLA_HEADS, GLA_DK, GLA_DV), F32),
                        pltpu.VMEM((GLA_HEADS, RB // GLA_CHUNK, GLA_DK, GLA_DV), F32),
                        pltpu.VMEM((GLA_HEADS, RB // GLA_CHUNK, GLA_DK, GLA_DV), BF16)],
        compiler_params=_params("parallel", "arbitrary"),
        name="gla",
    )(proj3, proj3, proj3, proj3, proj3, w2, ab, gn)


def _merge_body(x_ref, oa_ref, ob_ref, gm_ref, wa_ref, wb_ref, wo_ref, o_ref):
    D = x_ref.shape[1]
    gm = gm_ref[...]
    y = (jax.nn.sigmoid(gm[:, :D]) * _dot(oa_ref[...], wa_ref[...])
         + jax.nn.sigmoid(gm[:, D:]) * _dot(ob_ref[...], wb_ref[...]))
    o_ref[...] = x_ref[...] + _dot(y.astype(BF16), wo_ref[...])


def _merge(x, o_a, o_b, proj, wa, wb, wo, layer, tm=512):
    T, D = x.shape
    DA = o_a.shape[1]
    DB = o_b.shape[1]
    assert T % tm == 0 and C_GM == 0
    return pl.pallas_call(
        _merge_body,
        grid=(T // tm,),
        in_specs=[
            pl.BlockSpec((tm, D), lambda i: (i, 0)),
            pl.BlockSpec((tm, DA), lambda i: (i, 0)),
            pl.BlockSpec((tm, DB), lambda i: (i, 0)),
            pl.BlockSpec((tm, 2 * D), lambda i: (i, 0)),
            pl.BlockSpec((None, DA, D), lambda i: (layer, 0, 0)),
            pl.BlockSpec((None, DB, D), lambda i: (layer, 0, 0)),
            pl.BlockSpec((None, D, D), lambda i: (layer, 0, 0)),
        ],
        out_specs=pl.BlockSpec((tm, D), lambda i: (i, 0)),
        out_shape=jax.ShapeDtypeStruct((T, D), F32),
        compiler_params=_params("parallel"),
        name="merge",
    )(x, o_a, o_b, proj, wa, wb, wo)


def _rel_bucket(dist):
    n = jnp.maximum(dist, 0)
    exact = REL_BUCKETS // 2
    nf = jnp.maximum(n, 1).astype(jnp.float32)
    log_b = exact + (jnp.log(nf / exact) / math.log(REL_MAX_DIST / exact)
                     * (REL_BUCKETS - exact)).astype(jnp.int32)
    return jnp.where(n < exact, n, jnp.minimum(log_b, REL_BUCKETS - 1))


def _regroup_w_in(w_in):
    widths = (NSA_HEADS * NSA_DH, 6 * NSA_GROUPS * NSA_DH, 3 * NSA_HEADS, GLA_HEADS * GLA_DK,
              GLA_HEADS * GLA_DK, GLA_HEADS * GLA_DV, GLA_RANK, GLA_HEADS * GLA_DV, 2 * D_MODEL)
    offs = np.concatenate([[0], np.cumsum(widths)])
    q_a, kv_a, g_a, q_b, k_b, v_b, a_lr, r_b, g_m = (
        w_in[..., offs[n]:offs[n + 1]].astype(BF16) for n in range(9))

    def pad(p):
        return jnp.pad(p, ((0, 0), (0, 0), (0, LANES - p.shape[-1])))

    groups = [g_m, v_b, r_b, q_a, q_b, k_b, kv_a, pad(g_a), pad(a_lr)]
    assert sum(p.shape[-1] for p in groups) == N_PROJ
    return groups


def _overlap_t(ncp, nsp, nc, ns):
    c = np.arange(ncp)[None, :] * CMP_STRIDE
    s = np.arange(nsp)[:, None] * SEL_BLOCK
    ov = (c < s + SEL_BLOCK) & (c + CMP_BLOCK > s) & (np.arange(ncp)[None, :] < nc) & (np.arange(nsp)[:, None] < ns)
    return jnp.asarray(ov.astype(np.float32))


def kernel(x, rel_table, ffn1_norm, ffn1_w_gate, ffn1_w_up, ffn1_w_down, mix_norm, w_in, cmp_pos_k, cmp_pos_v, cmp_k_w1, cmp_k_w2, cmp_v_w1, cmp_v_w2, gla_a_w2, gla_a_b, gla_out_norm, w_branch_nsa, w_branch_gla, w_out, ffn2_norm, ffn2_w_gate, ffn2_w_up, ffn2_w_down, final_norm):
    B, S, D = x.shape
    L = w_in.shape[0]
    T = B * S
    G, HPG, DH = NSA_GROUPS, NSA_HPG, NSA_DH
    nch = S // CMP_STRIDE
    nc = (S - CMP_BLOCK) // CMP_STRIDE + 1
    ns = S // SEL_BLOCK
    assert D == D_MODEL and nc == nch - 1

    w1g, w1u, w1d = ffn1_w_gate.astype(BF16), ffn1_w_up.astype(BF16), ffn1_w_down.astype(BF16)
    w2g, w2u, w2d = ffn2_w_gate.astype(BF16), ffn2_w_up.astype(BF16), ffn2_w_down.astype(BF16)
    w_proj = _regroup_w_in(w_in)
    wa, wb, wo = w_branch_nsa.astype(BF16), w_branch_gla.astype(BF16), w_out.astype(BF16)
    cmp_pos = jnp.stack([cmp_pos_k, cmp_pos_v]).reshape(2, L, 1, CMP_BLOCK * DH)
    cmp_w1 = jnp.stack([cmp_k_w1, cmp_v_w1]).astype(BF16)
    cmp_w2 = jnp.stack([cmp_k_w2, cmp_v_w2]).astype(BF16)
    gla_w2 = jnp.pad(gla_a_w2, ((0, 0), (0, LANES - GLA_RANK), (0, 0))).astype(BF16)
    gla_b = gla_a_b.reshape(L, 1, -1)
    gla_gn = gla_out_norm.reshape(L, 1, -1)
    n1 = ffn1_norm.reshape(L, 1, D)
    n2 = ffn2_norm.reshape(L, 1, D)
    nm = mix_norm.reshape(L, 1, D)

    buckets = _rel_bucket(jnp.arange(REL_MAX_DIST + 1, dtype=jnp.int32))
    thr = jnp.searchsorted(buckets, jnp.arange(REL_BUCKETS, dtype=jnp.int32), side="left").astype(jnp.int32)
    bias_toep, bias_win, bias_cmp = _bias_tables(thr, rel_table, S, nch)
    ovl_t = _overlap_t(nch, DH, nc, ns)

    xf = x.reshape(T, D)
    for l in range(L):
        xf = _ffn(xf, n1, w1g, w1u, w1d, l)
        proj, xc, ksw, vsw = _proj(xf, nm, w_proj, l, S)
        proj3 = proj.reshape(B, S, N_PROJ)

        kvc = _compress(xc, cmp_pos, cmp_w1, cmp_w2, l)
        o_a = _nsa(proj3, kvc, ksw, vsw, bias_cmp, bias_toep, bias_win, ovl_t)
        o_a = o_a.reshape(T, NSA_HEADS * DH)

        o_b = _gla(proj3, gla_w2, gla_b, gla_gn, l).reshape(T, GLA_HEADS * GLA_DV)

        xf = _merge(xf, o_a, o_b, proj, wa, wb, wo, l)
        xf = _ffn(xf, n2, w2g, w2u, w2d, l,
                  final_g=final_norm.reshape(1, D) if l == L - 1 else None)
    return xf.reshape(B, S, D)
```

```python
import functools
import math

import numpy as np
import jax
import jax.numpy as jnp
from jax import lax
from jax.experimental import pallas as pl
from jax.experimental.pallas import tpu as pltpu

F32 = jnp.float32
BF16 = jnp.bfloat16

NSA_HEADS = 8
NSA_GROUPS = 2
NSA_HPG = NSA_HEADS // NSA_GROUPS
NSA_DH = 64
CMP_BLOCK = 32
CMP_STRIDE = 16
SEL_BLOCK = 64
SEL_SHIFT = SEL_BLOCK.bit_length() - 1
assert 1 << SEL_SHIFT == SEL_BLOCK
SEL_TOPN = 16
WINDOW = 512
GLA_HEADS = 4
GLA_DK = 128
GLA_DV = 256
GLA_RANK = 16
GLA_TAU = 16.0
GLA_CHUNK = 64
REL_BUCKETS = 32
REL_MAX_DIST = 1024
EPS = 1e-6
NEG = -1e30
LOG2E = math.log2(math.e)

LANES = 128
MXU_COLS = 256
VMEM_LIMIT = 56 * 1024 * 1024

ATT_TILE = 256
N_BIAS_TILES = REL_MAX_DIST // ATT_TILE + 2
N_WIN_TILES = WINDOW // ATT_TILE + 1
NSA_SUB = 128
SEL_UNROLLS = (8, 4, 2)
GLA_STEP = 512

D_MODEL = 1024
C_GM = 0
C_VB = 2048
C_RB = 3072
C_QA = 4096
C_QB = 4608
C_KB = 5120
C_KV = 5632
C_GA = 6400
C_AL = 6528
N_PROJ = 6656


def _dot(a, b, precision=None):
    return lax.dot_general(a, b, (((1,), (0,)), ((), ())), precision=precision,
                           preferred_element_type=F32)


def _dot_nt(a, b, precision=None):
    return lax.dot_general(a, b, (((1,), (1,)), ((), ())), precision=precision,
                           preferred_element_type=F32)


def _dot_tn(a, b, precision=None):
    return lax.dot_general(a, b, (((0,), (0,)), ((), ())), precision=precision,
                           preferred_element_type=F32)


def _rms(x, g):
    return x * lax.rsqrt(jnp.mean(x * x, axis=-1, keepdims=True) + EPS) * g


def _params(*sem):
    return pltpu.CompilerParams(dimension_semantics=sem, vmem_limit_bytes=VMEM_LIMIT)


def _ffn_body(x_ref, g_ref, wg_ref, wu_ref, wd_ref, *rest, final, fc):
    if final:
        fg_ref, o_ref = rest
    else:
        (o_ref,) = rest
    x = x_ref[...]
    h = _rms(x, g_ref[...]).astype(BF16)
    acc = None
    for f0 in range(0, wg_ref.shape[1], fc):
        gate = _dot(h, wg_ref[:, f0:f0 + fc])
        up = _dot(h, wu_ref[:, f0:f0 + fc])
        act = (gate * jax.nn.sigmoid(gate) * up).astype(BF16)
        down = _dot(act, wd_ref[f0:f0 + fc, :])
        acc = down if acc is None else acc + down
    y = x + 0.5 * acc
    if final:
        y = _rms(y, fg_ref[...])
    o_ref[...] = y


def _ffn(x, g, wg, wu, wd, layer, final_g=None, tm=1024, fc=MXU_COLS):
    T, D = x.shape
    F = wg.shape[-1]
    assert T % tm == 0 and F % fc == 0
    final = final_g is not None
    resident = pl.Buffered(1)
    in_specs = [
        pl.BlockSpec((tm, D), lambda i: (i, 0)),
        pl.BlockSpec((None, 1, D), lambda i: (layer, 0, 0)),
        pl.BlockSpec((None, D, F), lambda i: (layer, 0, 0), pipeline_mode=resident),
        pl.BlockSpec((None, D, F), lambda i: (layer, 0, 0), pipeline_mode=resident),
        pl.BlockSpec((None, F, D), lambda i: (layer, 0, 0), pipeline_mode=resident),
    ]
    args = [x, g, wg, wu, wd]
    if final:
        in_specs.append(pl.BlockSpec((1, D), lambda i: (0, 0)))
        args.append(final_g)
    return pl.pallas_call(
        functools.partial(_ffn_body, final=final, fc=fc),
        grid=(T // tm,),
        in_specs=in_specs,
        out_specs=pl.BlockSpec((tm, D), lambda i: (i, 0)),
        out_shape=jax.ShapeDtypeStruct((T, D), F32),
        compiler_params=_params("parallel"),
        name="ffn",
    )(*args)


def _proj_body(x_ref, g_ref, *rest, n_groups, tiles_per_seq):
    w_refs = rest[:n_groups]
    o_ref, xc_ref, ksw_ref, vsw_ref, stage_ref = rest[n_groups:]
    h = _rms(x_ref[...], g_ref[...]).astype(BF16)
    n0 = 0
    for w_ref in w_refs:
        o_ref[:, n0:n0 + w_ref.shape[1]] = _dot(h, w_ref[...])
        n0 += w_ref.shape[1]

    TM = o_ref.shape[0]
    DH = NSA_DH
    GW = NSA_GROUPS * DH

    def kv_cols(n, g):
        return slice(C_KV + n * GW + g * DH, C_KV + n * GW + (g + 1) * DH)

    t = (pl.program_id(0) % tiles_per_seq) * TM + lax.broadcasted_iota(jnp.int32, (TM, DH), 0)
    onehot = ((t >> SEL_SHIFT) == lax.broadcasted_iota(jnp.int32, (TM, DH), 1)).astype(BF16)
    ones = jnp.ones((TM, DH), BF16)
    zeros = jnp.zeros((TM, DH), BF16)
    for g in range(NSA_GROUPS):
        ksw_ref[0, g] = jnp.concatenate([o_ref[:, kv_cols(2, g)].astype(BF16), onehot], axis=1)
        vsw_ref[0, g] = jnp.concatenate([o_ref[:, kv_cols(3, g)].astype(BF16), ones], axis=1)
        ksw_ref[1, g] = jnp.concatenate([o_ref[:, kv_cols(4, g)].astype(BF16), zeros], axis=1)
        vsw_ref[1, g] = jnp.concatenate([o_ref[:, kv_cols(5, g)].astype(BF16), ones], axis=1)
    for s in range(2):
        stage_ref[s] = o_ref[:, C_KV + s * GW:C_KV + (s + 1) * GW]
        for l in range(CMP_STRIDE):
            x = stage_ref[s, pl.ds(l, TM // CMP_STRIDE, stride=CMP_STRIDE), :]
            for g in range(NSA_GROUPS):
                xc_ref[s, g, :, l * DH:(l + 1) * DH] = x[:, g * DH:(g + 1) * DH]


def _proj(x, g, ws, layer, seq_len, tm=512):
    T, D = x.shape
    N = sum(w.shape[-1] for w in ws)
    G, DH = NSA_GROUPS, NSA_DH
    B = T // seq_len
    nb = seq_len // tm
    assert T % seq_len == 0 and seq_len % tm == 0 and tm % (8 * CMP_STRIDE) == 0
    assert all(w.shape[-1] % LANES == 0 for w in ws)

    def kv_out(width, rows):
        return pl.BlockSpec((2, None, G, rows, width), lambda i: (0, i // nb, 0, i % nb, 0))

    return pl.pallas_call(
        functools.partial(_proj_body, n_groups=len(ws), tiles_per_seq=nb),
        grid=(T // tm,),
        in_specs=[
            pl.BlockSpec((tm, D), lambda i: (i, 0)),
            pl.BlockSpec((None, 1, D), lambda i: (layer, 0, 0)),
        ] + [pl.BlockSpec((None, D, w.shape[-1]), lambda i: (layer, 0, 0), pipeline_mode=pl.Buffered(1))
             for w in ws],
        out_specs=[pl.BlockSpec((tm, N), lambda i: (i, 0)),
                   kv_out(CMP_STRIDE * DH, tm // CMP_STRIDE), kv_out(2 * DH, tm), kv_out(2 * DH, tm)],
        out_shape=[jax.ShapeDtypeStruct((T, N), F32),
                   jax.ShapeDtypeStruct((2, B, G, seq_len // CMP_STRIDE, CMP_STRIDE * DH), F32),
                   jax.ShapeDtypeStruct((2, B, G, seq_len, 2 * DH), BF16),
                   jax.ShapeDtypeStruct((2, B, G, seq_len, 2 * DH), BF16)],
        scratch_shapes=[pltpu.VMEM((2, tm, G * DH), F32)],
        compiler_params=_params("parallel"),
        name="proj",
    )(x, g, *ws)


def _compress_body(x_ref, pos_ref, w1_ref, w2_ref, o_ref):
    x = x_ref[...]
    half = x.shape[1]
    lo = (x + pos_ref[:, :half]).astype(BF16)
    hi = (x + pos_ref[:, half:]).astype(BF16)
    h_lo = _dot(lo, w1_ref[:half, :])
    h_hi = _dot(hi, w1_ref[half:, :])
    nch = x.shape[0]
    hid = h_lo + pltpu.roll(h_hi, nch - 1, 0)
    act = (hid * jax.nn.sigmoid(hid)).astype(BF16)
    o_ref[...] = _dot(act, w2_ref[...]).astype(o_ref.dtype)


def _compress(xc, pos, w1, w2, layer):
    _, B, G, NCH, CW = xc.shape
    HC = w1.shape[-1]
    dh = w2.shape[-1]
    return pl.pallas_call(
        _compress_body,
        grid=(2, B, G),
        in_specs=[
            pl.BlockSpec((None, None, None, NCH, CW), lambda s, b, g: (s, b, g, 0, 0)),
            pl.BlockSpec((None, None, 1, 2 * CW), lambda s, b, g: (s, layer, 0, 0)),
            pl.BlockSpec((None, None, 2 * CW, HC), lambda s, b, g: (s, layer, 0, 0)),
            pl.BlockSpec((None, None, HC, dh), lambda s, b, g: (s, layer, 0, 0)),
        ],
        out_specs=pl.BlockSpec((None, None, None, NCH, dh), lambda s, b, g: (s, b, g, 0, 0)),
        out_shape=jax.ShapeDtypeStruct((2, B, G, NCH, dh), BF16),
        compiler_params=_params("parallel", "parallel", "parallel"),
        name="compress",
    )(xc, pos, w1, w2)


def _bias_lookup(n, thr_ref, tab_ref):
    vals = [jnp.full(n.shape, tab_ref[0, h], F32) for h in range(NSA_HEADS)]
    for k in range(1, REL_BUCKETS):
        above = n >= thr_ref[k]
        vals = [jnp.where(above, tab_ref[k, h], v) for h, v in enumerate(vals)]
    return vals


def _toeplitz_body(thr_ref, tab_ref, o_ref, *, rows, window):
    dd = pl.program_id(0)
    T = o_ref.shape[-1]
    k = lax.broadcasted_iota(jnp.int32, (8, 2 * T), 1)
    dist = dd * T + (T - 1) - k
    keep = (dist >= 0) & (dist < WINDOW) if window else dist >= 0
    for h, val in enumerate(_bias_lookup(jnp.clip(dist, 0, REL_MAX_DIST), thr_ref, tab_ref)):
        row = jnp.where(keep, val * LOG2E, NEG)[0:1, :]
        for r0 in range(0, T, rows):
            rot = pltpu.roll(jnp.broadcast_to(row, (rows, 2 * T)), T + 1 + r0, 1, stride=1, stride_axis=0)
            o_ref[h, r0:r0 + rows, :] = rot[:, :T]


def _cmpbias_body(thr_ref, tab_ref, o_ref, stage_ref):
    i = pl.program_id(0)
    H, TQ, NC = o_ref.shape
    RES = CMP_STRIDE
    MR = TQ // RES
    W = NC + LANES
    res = lax.broadcasted_iota(jnp.int32, (RES, W), 0)
    k = lax.broadcasted_iota(jnp.int32, (RES, W), 1)
    n = CMP_STRIDE * (i * MR + MR - 1 - k) + res - (CMP_BLOCK - 1)
    for h, val in enumerate(_bias_lookup(jnp.clip(n, 0, REL_MAX_DIST), thr_ref, tab_ref)):
        for r in range(RES):
            rot = pltpu.roll(jnp.broadcast_to(val[r:r + 1, :], (MR, W)), W - (MR - 1), 1,
                             stride=1, stride_axis=0)
            for s in range(NC // LANES):
                stage_ref[s, pl.ds(r, MR, stride=RES), :] = rot[:, s * LANES:(s + 1) * LANES]
        for s in range(NC // LANES):
            o_ref[h, :, s * LANES:(s + 1) * LANES] = stage_ref[s]


def _bias_tables(thr, rel_table, S, ncp):
    T = ATT_TILE
    H = NSA_HEADS
    rows = 64
    assert ncp % LANES == 0 and T % CMP_STRIDE == 0 and T // CMP_STRIDE - 1 <= LANES
    smem = pl.BlockSpec(memory_space=pltpu.SMEM)

    def toeplitz(n_tiles, window, name):
        return pl.pallas_call(
            functools.partial(_toeplitz_body, rows=rows, window=window),
            grid=(n_tiles,),
            in_specs=[smem, smem],
            out_specs=pl.BlockSpec((None, H, T, T), lambda d: (d, 0, 0, 0)),
            out_shape=jax.ShapeDtypeStruct((n_tiles, H, T, T), F32),
            compiler_params=_params("parallel"),
            name=name,
        )(thr, rel_table)

    toep = toeplitz(N_BIAS_TILES, False, "bias_toeplitz")
    toep_win = toeplitz(N_WIN_TILES + 1, True, "bias_window")
    cmpb = pl.pallas_call(
        _cmpbias_body,
        grid=(S // T,),
        in_specs=[smem, smem],
        out_specs=pl.BlockSpec((H, T, ncp), lambda i: (0, i, 0)),
        out_shape=jax.ShapeDtypeStruct((H, S, ncp), F32),
        scratch_shapes=[pltpu.VMEM((ncp // LANES, T, LANES), F32)],
        compiler_params=_params("parallel"),
        name="bias_cmp",
    )(thr, rel_table)
    return toep, toep_win, cmpb


def _nsa_body(q_ref, gate_ref, kc_ref, vc_ref, ks_ref, vs_ref, kw_ref, vw_ref, bc_ref, bts_ref, btw_ref,
              ovl_ref, o_ref, qc_ref, qa_ref, m_ref, acc_ref, oacc_ref, sa_ref, sb_ref, sc_ref, alpha_ref,
              gs_ref, lc_ref, pcb_ref, ps_ref,
              *, n_sel_blocks, n_top):
    group = pl.program_id(1)
    i = pl.program_id(2)
    HPG, TQ, DH = qc_ref.shape
    TK = TQ
    SB = NSA_SUB
    NCP = kc_ref.shape[0]
    t0 = i * TQ
    R = HPG * TQ
    subs = [(h, a0) for a0 in range(0, TQ, SB) for h in range(HPG)]

    for h in range(HPG):
        q = q_ref[:, h * DH:(h + 1) * DH]
        qc_ref[h] = (q * (DH ** -0.5)).astype(BF16)
        qa_ref[h, :, :DH] = (q * (DH ** -0.5 * LOG2E)).astype(BF16)
        qa_ref[h, :, DH:] = jnp.zeros((TQ, DH), BF16)

    gates = jax.nn.sigmoid(gate_ref[...])
    gs_ref[...] = jnp.where(group == 0, gates, pltpu.roll(gates, LANES - 3 * HPG, 1))

    def gate_col(a0, col):
        return gs_ref[a0:a0 + SB, col:col + 1]

    def cmp_branch():
        lc_all = _dot_nt(qc_ref[...].reshape(R, DH), kc_ref[...])
        lc_ref[...] = (lc_all.reshape(HPG, TQ, NCP) + bc_ref[...]).reshape(R, NCP)
        c_end = lax.broadcasted_iota(jnp.int32, (SB, NCP), 1) * CMP_STRIDE + (CMP_BLOCK - 1)
        r_c = lax.broadcasted_iota(jnp.int32, (SB, NCP), 0)
        for a0 in range(0, TQ, SB):
            mc = c_end <= (t0 + a0 + r_c)
            p_heads = None
            for h in range(HPG):
                rows = slice(h * TQ + a0, h * TQ + a0 + SB)
                lc = jnp.where(mc, lc_ref[rows, :], NEG)
                pc = jnp.where(mc, jnp.exp(lc - jnp.max(lc, axis=-1, keepdims=True)), 0.0)
                den = jnp.sum(pc, axis=-1, keepdims=True)
                pc = pc * jnp.where(den > 0.0, 1.0 / den, 0.0)
                pcb_ref[rows, :] = pc.astype(BF16)
                p_heads = pc if p_heads is None else p_heads + pc
            ps_ref[a0:a0 + SB, :] = p_heads
        o_cmp = _dot(pcb_ref[...], vc_ref[...])
        for h in range(HPG):
            oacc_ref[h] = gs_ref[:, 3 * h:3 * h + 1] * o_cmp[h * TQ:(h + 1) * TQ]

    def reset():
        m_ref[...] = jnp.full(m_ref.shape, NEG, F32)
        acc_ref[...] = jnp.zeros(acc_ref.shape, F32)

    def logits(k_ref, j, bias_ref, bias_tile, s_ref):
        k = k_ref[pl.ds(pl.multiple_of(j * TK, TK), TK), :]
        qk = _dot_nt(qa_ref[...].reshape(R, 2 * DH), k)
        s_ref[...] = (qk.reshape(HPG, TQ, TK) + bias_ref[bias_tile]).reshape(R, TK)

    def update_steps(v_ref, j, s_ref):
        chunks = [slice(r0, r0 + SB) for r0 in range(0, R, SB)]

        def pass1(rows):
            m_old = m_ref[rows, :]
            m_new = jnp.maximum(m_old, jnp.max(s_ref[rows, :], axis=-1, keepdims=True))
            alpha_ref[rows, :] = jnp.exp2(m_old - m_new)
            m_ref[rows, :] = m_new

        def pass2(rows):
            v = v_ref[pl.ds(pl.multiple_of(j * TK, TK), TK), :]
            m_new = m_ref[rows, :]
            p = jnp.exp2(s_ref[rows, :] - jnp.concatenate([m_new] * (TK // LANES), axis=1))
            acc_ref[rows, :] = alpha_ref[rows, :] * acc_ref[rows, :] + _dot(p.astype(BF16), v)

        return ([functools.partial(pass1, rows) for rows in chunks]
                + [functools.partial(pass2, rows) for rows in chunks])

    def update(v_ref, j, s_ref):
        for piece in update_steps(v_ref, j, s_ref):
            piece()

    def finalize(gate_off):
        for h, a0 in subs:
            acc = acc_ref[h * TQ + a0:h * TQ + a0 + SB, :]
            o = (acc * (1.0 / pltpu.roll(acc, DH, 1)))[:, :DH]
            oacc_ref[h, a0:a0 + SB, :] += gate_col(a0, 3 * h + gate_off) * o

    reset()
    buf_a, buf_b, buf_c = sa_ref, sb_ref, sc_ref
    win_pieces = []
    for n, buf in enumerate((buf_a, buf_b, buf_c)):
        dd = N_WIN_TILES - 1 - n
        j = jnp.maximum(i - dd, 0)
        logits(kw_ref, j, btw_ref, jnp.where(i < dd, N_WIN_TILES, dd), buf)
        win_pieces += update_steps(vw_ref, j, buf)

    cmp_branch()

    imp_t = _dot_nt(ovl_ref[...], ps_ref[...], precision=lax.Precision.HIGHEST)
    s_io = lax.broadcasted_iota(jnp.int32, (DH, TQ), 0)
    jcur = (t0 + lax.broadcasted_iota(jnp.int32, (DH, TQ), 1)) >> SEL_SHIFT
    forced = (s_io == 0) | (s_io == jcur) | (s_io == jcur - 1)
    score = jnp.where(forced, 1e6, jnp.where(s_io <= jcur, imp_t, -1e6))
    sub8 = lax.broadcasted_iota(jnp.int32, (8, TQ), 0)
    cnt = [jnp.zeros((8, TQ), jnp.int32) for _ in range(DH // 8)]
    emitted = 0
    for sp in range(n_sel_blocks):
        row = score[sp:sp + 1, :]
        for g in range(DH // 8):
            blk = score[8 * g:8 * g + 8, :]
            if 8 * g > sp:
                beats = row >= blk
            elif 8 * g + 7 <= sp:
                beats = row > blk
            else:
                beats = (row > blk) | ((row >= blk) & (sub8 > sp - 8 * g))
            cnt[g] = cnt[g] + jnp.where(beats, 1, 0)
        due = (sp + 1) * len(win_pieces) // n_sel_blocks
        for piece in win_pieces[emitted:due]:
            piece()
        emitted = due
    finalize(2)
    rank = jnp.concatenate(cnt, axis=0)
    drop_t = jnp.where((rank < n_top) & (s_io < n_sel_blocks), 0.0, NEG)
    drop = jnp.concatenate([jnp.zeros((DH, TQ), F32), drop_t], axis=0).T.astype(BF16)
    for h in range(HPG):
        qa_ref[h, :, DH:] = drop[:, DH:]

    def sel_logits(j, buf):
        logits(ks_ref, j, bts_ref, jnp.minimum(i - j, N_BIAS_TILES - 1), buf)

    reset()
    sel_logits(0, buf_a)

    def pipeline(j, n):
        for t in range(0, n, 2):
            sel_logits(j + t + 1, buf_b)
            update(vs_ref, j + t, buf_a)
            sel_logits(j + t + 2, buf_a)
            update(vs_ref, j + t + 1, buf_b)

    j_tail = 0
    for unroll in SEL_UNROLLS:
        trips = (i - j_tail) // unroll

        def body(n, carry, start=j_tail, unroll=unroll):
            pipeline(start + unroll * n, unroll)
            return carry

        lax.fori_loop(0, trips, body, 0)
        j_tail = j_tail + unroll * trips

    @pl.when(j_tail == i)
    def _():
        update(vs_ref, i, buf_a)

    @pl.when(j_tail < i)
    def _():
        sel_logits(i, buf_b)
        update(vs_ref, j_tail, buf_a)
        update(vs_ref, i, buf_b)

    finalize(1)

    for h in range(HPG):
        o_ref[:, h * DH:(h + 1) * DH] = oacc_ref[h].astype(o_ref.dtype)


def _nsa(proj3, kvc, ksw, vsw, bias_cmp, bias_toep, bias_win, ovl_t):
    B, S, _ = proj3.shape
    G, HPG, DH = NSA_GROUPS, NSA_HPG, NSA_DH
    GW = HPG * DH
    TQ = ATT_TILE
    NCP = kvc.shape[3]
    ns = S // SEL_BLOCK
    assert S % TQ == 0 and (HPG * TQ) % NSA_SUB == 0 and WINDOW % TQ == 0
    assert 2 * DH == LANES and ns <= DH and ovl_t.shape == (DH, NCP)
    assert C_QA % GW == 0 and C_GA % LANES == 0 and 3 * NSA_HEADS <= LANES

    def seq_spec(idx):
        return pl.BlockSpec((None, None, None, S, 2 * DH), lambda b, g, i: (idx, b, g, 0, 0))

    def cmp_spec(idx):
        return pl.BlockSpec((None, None, None, NCP, DH), lambda b, g, i: (idx, b, g, 0, 0))

    return pl.pallas_call(
        functools.partial(_nsa_body, n_sel_blocks=ns, n_top=min(SEL_TOPN, ns)),
        grid=(B, G, S // TQ),
        in_specs=[
            pl.BlockSpec((None, TQ, GW), lambda b, g, i: (b, i, C_QA // GW + g)),
            pl.BlockSpec((None, TQ, LANES), lambda b, g, i: (b, i, C_GA // LANES)),
            cmp_spec(0), cmp_spec(1),
            seq_spec(0), seq_spec(0), seq_spec(1), seq_spec(1),
            pl.BlockSpec((HPG, TQ, NCP), lambda b, g, i: (g, i, 0)),
            pl.BlockSpec((N_BIAS_TILES, HPG, TQ, TQ), lambda b, g, i: (0, g, 0, 0)),
            pl.BlockSpec((N_WIN_TILES + 1, HPG, TQ, TQ), lambda b, g, i: (0, g, 0, 0)),
            pl.BlockSpec((DH, NCP), lambda b, g, i: (0, 0)),
        ],
        out_specs=pl.BlockSpec((None, TQ, GW), lambda b, g, i: (b, i, g)),
        out_shape=jax.ShapeDtypeStruct((B, S, G * GW), BF16),
        scratch_shapes=[
            pltpu.VMEM((HPG, TQ, DH), BF16),
            pltpu.VMEM((HPG, TQ, 2 * DH), BF16),
            pltpu.VMEM((HPG * TQ, LANES), F32),
            pltpu.VMEM((HPG * TQ, 2 * DH), F32),
            pltpu.VMEM((HPG, TQ, DH), F32),
            pltpu.VMEM((HPG * TQ, TQ), F32),
            pltpu.VMEM((HPG * TQ, TQ), F32),
            pltpu.VMEM((HPG * TQ, TQ), F32),
            pltpu.VMEM((HPG * TQ, LANES), F32),
            pltpu.VMEM((TQ, LANES), F32),
            pltpu.VMEM((HPG * TQ, NCP), F32),
            pltpu.VMEM((HPG * TQ, NCP), BF16),
            pltpu.VMEM((TQ, NCP), F32),
        ],
        compiler_params=_params("parallel", "parallel", "arbitrary"),
        name="nsa",
    )(proj3, proj3, kvc, kvc, ksw, vsw, ksw, vsw, bias_cmp, bias_toep, bias_win, ovl_t)


def _gla_body(q_ref, k_ref, v_ref, r_ref, al_ref, w2_ref, ab_ref, gn_ref, o_ref, st_ref, u_ref, sb_ref):
    @pl.when(pl.program_id(1) == 0)
    def _():
        st_ref[...] = jnp.zeros_like(st_ref)

    C = GLA_CHUNK
    RB = q_ref.shape[0]
    n_chunks = RB // C
    pre = _dot(al_ref[...].astype(BF16), w2_ref[...]) + ab_ref[...]
    la = (jnp.minimum(pre, 0.0) - jnp.log(1.0 + jnp.exp(-jnp.abs(pre)))) * (1.0 / GLA_TAU)

    la_hi = la.astype(BF16)
    rest = la - la_hi.astype(F32)
    la_mid = rest.astype(BF16)
    la_lo = (rest - la_mid.astype(F32)).astype(BF16)
    r_io = lax.broadcasted_iota(jnp.int32, (C, 3 * C), 0)
    c_io = lax.broadcasted_iota(jnp.int32, (C, 3 * C), 1)
    tri3 = ((c_io & (C - 1)) <= r_io).astype(BF16)
    b_parts, bl_parts = [], []
    for c in range(n_chunks):
        rows = slice(c * C, (c + 1) * C)
        b_c = _dot(tri3, jnp.concatenate([la_hi[rows], la_mid[rows], la_lo[rows]], axis=0))
        b_parts.append(b_c)
        bl_parts.append(jnp.broadcast_to(b_c[C - 1:C, :], b_c.shape))
    b = jnp.concatenate(b_parts, axis=0)
    b_last = jnp.concatenate(bl_parts, axis=0)

    k = k_ref[...]
    q_dec = (q_ref[...] * (jnp.exp(b) * (GLA_DK ** -0.5))).astype(BF16)
    k_intra = (k * jnp.exp(-b)).astype(BF16)
    k_state = (k * jnp.exp(b_last - b)).astype(BF16)

    rr = lax.broadcasted_iota(jnp.int32, (RB, RB), 0)
    cc = lax.broadcasted_iota(jnp.int32, (RB, RB), 1)
    same_chunk_causal = (cc <= rr) & ((rr & -C) == (cc & -C))
    heads = [(h, slice(h * GLA_DK, (h + 1) * GLA_DK), slice(h * GLA_DV, (h + 1) * GLA_DV))
             for h in range(GLA_HEADS)]
    chunks = [(c, slice(c * C, (c + 1) * C)) for c in range(n_chunks)]
    for h, kc, vc in heads:
        v = v_ref[:, vc].astype(BF16)
        for c, rows in chunks:
            u_ref[h, c] = _dot_tn(k_state[rows, kc], v[rows])
    for h, kc, vc in heads:
        st = st_ref[h]
        for c, rows in chunks:
            sb_ref[h, c] = st.astype(BF16)
            decay = jnp.exp(b[(c + 1) * C - 8:(c + 1) * C, kc].T[:, 7:8])
            st = st * decay + u_ref[h, c]
        st_ref[h] = st
    for h, kc, vc in heads:
        v = v_ref[:, vc].astype(BF16)
        a = jnp.where(same_chunk_causal, _dot_nt(q_dec[:, kc], k_intra[:, kc]), 0.0)
        o_intra = _dot(a.astype(BF16), v)
        o_inter = jnp.concatenate([_dot(q_dec[rows, kc], sb_ref[h, c]) for c, rows in chunks], axis=0)
        o = _rms(o_intra + o_inter, gn_ref[:, vc])
        r = r_ref[:, vc]
        o_ref[:, vc] = (o * (r * jax.nn.sigmoid(r))).astype(o_ref.dtype)


def _gla(proj3, w2, ab, gn, layer):
    B, S, _ = proj3.shape
    RB = GLA_STEP
    HK = GLA_HEADS * GLA_DK
    HV = GLA_HEADS * GLA_DV
    assert S % RB == 0 and RB % GLA_CHUNK == 0

    def col(width, offset):
        assert offset % width == 0
        return pl.BlockSpec((None, RB, width), lambda b, s: (b, s, offset // width))

    return pl.pallas_call(
        _gla_body,
        grid=(B, S // RB),
        in_specs=[
            col(HK, C_QB), col(HK, C_KB), col(HV, C_VB), col(HV, C_RB), col(LANES, C_AL),
            pl.BlockSpec((None, LANES, HK), lambda b, s: (layer, 0, 0)),
            pl.BlockSpec((None, 1, HK), lambda b, s: (layer, 0, 0)),
            pl.BlockSpec((None, 1, HV), lambda b, s: (layer, 0, 0)),
        ],
        out_specs=pl.BlockSpec((None, RB, HV), lambda b, s: (b, s, 0)),
        out_shape=jax.ShapeDtypeStruct((B, S, HV), BF16),
        scratch_shapes=[pltpu.VMEM((GLA_HEADS, GLA_DK, GLA_DV), F32),
                        pltpu.VMEM((GLA_HEADS, RB // GLA_CHUNK, GLA_DK, GLA_DV), F32),
                        pltpu.VMEM((GLA_HEADS, RB // GLA_CHUNK, GLA_DK, GLA_DV), BF16)],
        compiler_params=_params("parallel", "arbitrary"),
        name="gla",
    )(proj3, proj3, proj3, proj3, proj3, w2, ab, gn)


def _merge_body(x_ref, oa_ref, ob_ref, gm_ref, wa_ref, wb_ref, wo_ref, o_ref):
    D = x_ref.shape[1]
    gm = gm_ref[...]
    y = (jax.nn.sigmoid(gm[:, :D]) * _dot(oa_ref[...], wa_ref[...])
         + jax.nn.sigmoid(gm[:, D:]) * _dot(ob_ref[...], wb_ref[...]))
    o_ref[...] = x_ref[...] + _dot(y.astype(BF16), wo_ref[...])


def _merge(x, o_a, o_b, proj, wa, wb, wo, layer, tm=512):
    T, D = x.shape
    DA = o_a.shape[1]
    DB = o_b.shape[1]
    assert T % tm == 0 and C_GM == 0
    return pl.pallas_call(
        _merge_body,
        grid=(T // tm,),
        in_specs=[
            pl.BlockSpec((tm, D), lambda i: (i, 0)),
            pl.BlockSpec((tm, DA), lambda i: (i, 0)),
            pl.BlockSpec((tm, DB), lambda i: (i, 0)),
            pl.BlockSpec((tm, 2 * D), lambda i: (i, 0)),
            pl.BlockSpec((None, DA, D), lambda i: (layer, 0, 0)),
            pl.BlockSpec((None, DB, D), lambda i: (layer, 0, 0)),
            pl.BlockSpec((None, D, D), lambda i: (layer, 0, 0)),
        ],
        out_specs=pl.BlockSpec((tm, D), lambda i: (i, 0)),
        out_shape=jax.ShapeDtypeStruct((T, D), F32),
        compiler_params=_params("parallel"),
        name="merge",
    )(x, o_a, o_b, proj, wa, wb, wo)


def _rel_bucket(dist):
    n = jnp.maximum(dist, 0)
    exact = REL_BUCKETS // 2
    nf = jnp.maximum(n, 1).astype(jnp.float32)
    log_b = exact + (jnp.log(nf / exact) / math.log(REL_MAX_DIST / exact)
                     * (REL_BUCKETS - exact)).astype(jnp.int32)
    return jnp.where(n < exact, n, jnp.minimum(log_b, REL_BUCKETS - 1))


def _regroup_w_in(w_in):
    widths = (NSA_HEADS * NSA_DH, 6 * NSA_GROUPS * NSA_DH, 3 * NSA_HEADS, GLA_HEADS * GLA_DK,
              GLA_HEADS * GLA_DK, GLA_HEADS * GLA_DV, GLA_RANK, GLA_HEADS * GLA_DV, 2 * D_MODEL)
    offs = np.concatenate([[0], np.cumsum(widths)])
    q_a, kv_a, g_a, q_b, k_b, v_b, a_lr, r_b, g_m = (
        w_in[..., offs[n]:offs[n + 1]].astype(BF16) for n in range(9))

    def pad(p):
        return jnp.pad(p, ((0, 0), (0, 0), (0, LANES - p.shape[-1])))

    groups = [g_m, v_b, r_b, q_a, q_b, k_b, kv_a, jnp.concatenate([pad(g_a), pad(a_lr)], axis=-1)]
    assert sum(p.shape[-1] for p in groups) == N_PROJ
    return groups


def _overlap_t(ncp, nsp, nc, ns):
    c = np.arange(ncp)[None, :] * CMP_STRIDE
    s = np.arange(nsp)[:, None] * SEL_BLOCK
    ov = (c < s + SEL_BLOCK) & (c + CMP_BLOCK > s) & (np.arange(ncp)[None, :] < nc) & (np.arange(nsp)[:, None] < ns)
    return jnp.asarray(ov.astype(np.float32))


def kernel(x, rel_table, ffn1_norm, ffn1_w_gate, ffn1_w_up, ffn1_w_down, mix_norm, w_in, cmp_pos_k, cmp_pos_v, cmp_k_w1, cmp_k_w2, cmp_v_w1, cmp_v_w2, gla_a_w2, gla_a_b, gla_out_norm, w_branch_nsa, w_branch_gla, w_out, ffn2_norm, ffn2_w_gate, ffn2_w_up, ffn2_w_down, final_norm):
    B, S, D = x.shape
    L = w_in.shape[0]
    T = B * S
    G, HPG, DH = NSA_GROUPS, NSA_HPG, NSA_DH
    nch = S // CMP_STRIDE
    nc = (S - CMP_BLOCK) // CMP_STRIDE + 1
    ns = S // SEL_BLOCK
    assert D == D_MODEL and nc == nch - 1

    w1g, w1u, w1d = ffn1_w_gate.astype(BF16), ffn1_w_up.astype(BF16), ffn1_w_down.astype(BF16)
    w2g, w2u, w2d = ffn2_w_gate.astype(BF16), ffn2_w_up.astype(BF16), ffn2_w_down.astype(BF16)
    w_proj = _regroup_w_in(w_in)
    wa, wb, wo = w_branch_nsa.astype(BF16), w_branch_gla.astype(BF16), w_out.astype(BF16)
    cmp_pos = jnp.stack([cmp_pos_k, cmp_pos_v]).reshape(2, L, 1, CMP_BLOCK * DH)
    cmp_w1 = jnp.stack([cmp_k_w1, cmp_v_w1]).astype(BF16)
    cmp_w2 = jnp.stack([cmp_k_w2, cmp_v_w2]).astype(BF16)
    gla_w2 = jnp.pad(gla_a_w2, ((0, 0), (0, LANES - GLA_RANK), (0, 0))).astype(BF16)
    gla_b = gla_a_b.reshape(L, 1, -1)
    gla_gn = gla_out_norm.reshape(L, 1, -1)
    n1 = ffn1_norm.reshape(L, 1, D)
    n2 = ffn2_norm.reshape(L, 1, D)
    nm = mix_norm.reshape(L, 1, D)

    buckets = _rel_bucket(jnp.arange(REL_MAX_DIST + 1, dtype=jnp.int32))
    thr = jnp.searchsorted(buckets, jnp.arange(REL_BUCKETS, dtype=jnp.int32), side="left").astype(jnp.int32)
    bias_toep, bias_win, bias_cmp = _bias_tables(thr, rel_table, S, nch)
    ovl_t = _overlap_t(nch, DH, nc, ns)

    xf = x.reshape(T, D)
    for l in range(L):
        xf = _ffn(xf, n1, w1g, w1u, w1d, l)
        proj, xc, ksw, vsw = _proj(xf, nm, w_proj, l, S)
        proj3 = proj.reshape(B, S, N_PROJ)

        kvc = _compress(xc, cmp_pos, cmp_w1, cmp_w2, l)
        o_a = _nsa(proj3, kvc, ksw, vsw, bias_cmp, bias_toep, bias_win, ovl_t)
        o_a = o_a.reshape(T, NSA_HEADS * DH)

        o_b = _gla(proj3, gla_w2, gla_b, gla_gn, l).reshape(T, GLA_HEADS * GLA_DV)

        xf = _merge(xf, o_a, o_b, proj, wa, wb, wo, l)
        xf = _ffn(xf, n2, w2g, w2u, w2d, l,
                  final_g=final_norm.reshape(1, D) if l == L - 1 else None)
    return xf.reshape(B, S, D)
```
